```python
import jax
import jax.numpy as jnp
from jax import lax
import numpy as np

D_MODEL = 1024
BATCH = 16
SEQ = 2048
DEPTH = 2

GRID_W = 64
CTX_LEN = 256
BLOCK = 128
WINDOW = 128
ROPE_THETA = 10000.0
EPS = 1e-6
NEG_INF = -1e30
N_MOD = 6

MLA_HEADS = 8
MLA_NOPE = 64
MLA_ROPE = 32
MLA_QK = MLA_NOPE + MLA_ROPE
MLA_V = 64
MLA_Q_RANK = 256
MLA_KV_RANK = 128

LRU_WIDTH = 512
LRU_BLOCKS = 8
LRU_BLOCK_DIM = LRU_WIDTH // LRU_BLOCKS
LRU_C = 8.0
CONV_W = 4

SWA_HEADS = 8
SWA_KV_HEADS = 2
SWA_GROUP = SWA_HEADS // SWA_KV_HEADS
SWA_HEAD_DIM = 64
SWA_KV_DIM = SWA_KV_HEADS * SWA_HEAD_DIM

GROUP_WIDTH = 512
MIX_WIDTH = 3 * GROUP_WIDTH
D_FF = 4 * D_MODEL
IN_SIZES = (MLA_Q_RANK, MLA_KV_RANK, MLA_ROPE, LRU_WIDTH, LRU_WIDTH, SWA_HEADS * SWA_HEAD_DIM, SWA_KV_DIM, SWA_KV_DIM)
IN_WIDTH = sum(IN_SIZES)

kernel_name = "hybrid_mla_rglru_swa_dit_block"


def rms_norm(x, g):
    xf = x.astype(jnp.float32)
    y = xf * lax.rsqrt(jnp.mean(xf * xf, axis=-1, keepdims=True) + EPS)
    return (y * g.astype(jnp.float32)).astype(x.dtype)


def modulate(x, g, shift, scale):
    return rms_norm(x, g) * (1 + scale) + shift


def axial_rope_tables(rows, dim):
    quarter = dim // 4
    n = rows * GRID_W
    row = jnp.repeat(jnp.arange(rows), GRID_W)
    col = jnp.tile(jnp.arange(GRID_W), rows)
    inv_freq = ROPE_THETA ** (-jnp.arange(quarter, dtype=jnp.float32) / quarter)
    ang = jnp.stack([row, col], axis=-1).astype(jnp.float32)[:, :, None] * inv_freq
    ang = jnp.broadcast_to(ang[:, :, None, :], (n, 2, 2, quarter)).reshape(n, dim)
    return jnp.cos(ang), jnp.sin(ang)


def apply_axial_rope(x, rope):
    cos, sin = rope
    d = x.shape[-1]
    xa = x.reshape(x.shape[:-1] + (2, 2, d // 4))
    rot = jnp.concatenate([-xa[..., 1:, :], xa[..., :1, :]], axis=-2).reshape(x.shape)
    return (x * cos[:, None, :] + rot * sin[:, None, :]).astype(x.dtype)


def split_columns(p):
    parts, start = [], 0
    for size in IN_SIZES:
        parts.append(p[..., start:start + size])
        start += size
    return parts


def mla_query(cq, q_a_g, w_uq, q_g, rope):
    B, L, _ = cq.shape
    q = rms_norm((rms_norm(cq, q_a_g) @ w_uq).reshape(B, L, MLA_HEADS, MLA_QK), q_g)
    if rope is not None:
        q = jnp.concatenate([q[..., :MLA_NOPE], apply_axial_rope(q[..., MLA_NOPE:], rope)], axis=-1)
    return q


def mla_key_value(ckv, k_rope, kv_a_g, w_ukv, k_g, rope):
    B, L, _ = ckv.shape
    kv = (rms_norm(ckv, kv_a_g) @ w_ukv).reshape(B, L, MLA_HEADS, MLA_NOPE + MLA_V)
    k_shared = jnp.broadcast_to(k_rope[:, :, None, :], (B, L, MLA_HEADS, MLA_ROPE))
    k = rms_norm(jnp.concatenate([kv[..., :MLA_NOPE], k_shared], axis=-1), k_g)
    if rope is not None:
        k = jnp.concatenate([k[..., :MLA_NOPE], apply_axial_rope(k[..., MLA_NOPE:], rope)], axis=-1)
    return k, kv[..., MLA_NOPE:]


def dense_block_attention(q, k, v):
    B, L, H, dq = q.shape
    nb = L // BLOCK
    scale = dq ** -0.5
    qb = q.reshape(B, nb, BLOCK, H, dq).transpose(1, 0, 2, 3, 4)

    def one_block(q_blk):
        s = jnp.einsum('bqhd,bkhd->bhqk', q_blk, k).astype(jnp.float32) * scale
        p = jax.nn.softmax(s, axis=-1).astype(v.dtype)
        return jnp.einsum('bhqk,bkhd->bqhd', p, v)

    o = lax.map(one_block, qb)
    return o.transpose(1, 0, 2, 3, 4).reshape(B, L, H * v.shape[-1])


def centred_depthwise_conv(x, w, b):
    left = CONV_W // 2
    y = lax.conv_general_dilated(x, w[:, None, :], window_strides=(1,), padding=[(left, CONV_W - 1 - left)],
                                 dimension_numbers=('NWC', 'WIO', 'NWC'), feature_group_count=x.shape[-1])
    return y + b


def rglru_coefficients(x, gate_w, gate_b, lam):
    B, L, W = x.shape
    xb = x.reshape(B, L, LRU_BLOCKS, LRU_BLOCK_DIM)
    g = jnp.einsum('blnc,zgncm->zgblnm', xb, gate_w).reshape(2, 2, B, L, W)
    g = jax.nn.sigmoid((g + gate_b[:, :, None, None, :]).astype(jnp.float32))
    r, i = g[:, 0], g[:, 1]
    log_a = -LRU_C * r * jax.nn.softplus(-lam.astype(jnp.float32))[:, None, None, :]
    a = jnp.exp(log_a)
    u = jnp.sqrt(-jnp.expm1(2.0 * log_a)) * (i * x.astype(jnp.float32)[None])
    return a, u


def linear_scan(a, u, h0, reverse):
    def combine(e1, e2):
        a1, u1 = e1
        a2, u2 = e2
        return a1 * a2, a2 * u1 + u2

    a_cum, h = lax.associative_scan(combine, (a, u), reverse=reverse, axis=1)
    return a_cum * h0[:, None, :] + h


def swa_query(sq, q_g, rope):
    B, L, _ = sq.shape
    q = rms_norm(sq.reshape(B, L, SWA_HEADS, SWA_HEAD_DIM), q_g)
    if rope is not None:
        q = apply_axial_rope(q, rope)
    return q.reshape(B, L, SWA_KV_HEADS, SWA_GROUP, SWA_HEAD_DIM)


def swa_key_value(sk, sv, k_g, rope):
    B, L, _ = sk.shape
    k = rms_norm(sk.reshape(B, L, SWA_KV_HEADS, SWA_HEAD_DIM), k_g)
    if rope is not None:
        k = apply_axial_rope(k, rope)
    return k, sv.reshape(B, L, SWA_KV_HEADS, SWA_HEAD_DIM)


def sink_column(sink, s):
    return jnp.broadcast_to(sink.astype(jnp.float32)[None, :, :, None, None], s.shape[:-1] + (1,))


def window_attention(q, k, v, k_ctx, v_ctx, sink):
    B, L, KV, G, d = q.shape
    nb = L // BLOCK
    span = BLOCK + 2 * WINDOW
    n_ctx = k_ctx.shape[1]
    scale = d ** -0.5
    pad = ((0, 0), (WINDOW, WINDOW), (0, 0), (0, 0))
    kp, vp = jnp.pad(k, pad), jnp.pad(v, pad)
    qb = q.reshape(B, nb, BLOCK, KV, G, d).transpose(1, 0, 2, 3, 4, 5)
    qi = jnp.arange(BLOCK)[:, None]
    kj = jnp.arange(span)[None, :]
    band = (kj >= qi) & (kj <= qi + 2 * WINDOW)

    def one_block(args):
        b, q_blk = args
        start = b * BLOCK
        k_win = lax.dynamic_slice_in_dim(kp, start, span, axis=1)
        v_win = lax.dynamic_slice_in_dim(vp, start, span, axis=1)
        j = start - WINDOW + kj
        mask = band & (j >= 0) & (j < L)
        s_win = jnp.einsum('bqkgd,bskd->bkgqs', q_blk, k_win).astype(jnp.float32) * scale
        s_win = jnp.where(mask, s_win, NEG_INF)
        s_ctx = jnp.einsum('bqkgd,bckd->bkgqc', q_blk, k_ctx).astype(jnp.float32) * scale
        s = jnp.concatenate([s_win, s_ctx, sink_column(sink, s_win)], axis=-1)
        p = jax.nn.softmax(s, axis=-1).astype(v.dtype)
        return (jnp.einsum('bkgqs,bskd->bqkgd', p[..., :span], v_win)
                + jnp.einsum('bkgqc,bckd->bqkgd', p[..., span:span + n_ctx], v_ctx))

    o = lax.map(one_block, (jnp.arange(nb), qb))
    return o.transpose(1, 0, 2, 3, 4, 5).reshape(B, L, KV * G * d)


def context_attention(q, k, v, sink):
    B, L, KV, G, d = q.shape
    s = jnp.einsum('bqkgd,bckd->bkgqc', q, k).astype(jnp.float32) * d ** -0.5
    p = jax.nn.softmax(jnp.concatenate([s, sink_column(sink, s)], axis=-1), axis=-1)[..., :-1].astype(v.dtype)
    return jnp.einsum('bkgqc,bckd->bqkgd', p, v).reshape(B, L, KV * G * d)


def merge_groups(o_a, o_b, o_c, g, w_out):
    B, L, _ = o_a.shape
    o = jnp.concatenate([o_a, o_b, o_c], axis=-1).reshape(B, L, 3, GROUP_WIDTH)
    return rms_norm(o, g.reshape(3, GROUP_WIDTH)).reshape(B, L, MIX_WIDTH) @ w_out


def sq_relu_mlp(h, w1, w2):
    return jnp.square(jax.nn.relu(h @ w1)) @ w2


def token_mixers(h, hc, w_in, q_a_g, w_uq, kv_a_g, w_ukv, mla_q_g, mla_k_g, conv_w, conv_b,
                 lru_gate_w, lru_gate_b, lru_lambda, swa_q_g, swa_k_g, swa_sink, group_g, w_out,
                 rope_mla, rope_swa, with_ctx_out):
    B = h.shape[0]
    cq, ckv, kr, lx, lg, sq, sk, sv = split_columns(h @ w_in)
    c_cq, c_ckv, c_kr, c_lx, c_lg, c_sq, c_sk, c_sv = split_columns(hc @ w_in)

    k_a, v_a = mla_key_value(ckv, kr, kv_a_g, w_ukv, mla_k_g, rope_mla)
    kc_a, vc_a = mla_key_value(c_ckv, c_kr, kv_a_g, w_ukv, mla_k_g, None)
    o_a = dense_block_attention(mla_query(cq, q_a_g, w_uq, mla_q_g, rope_mla),
                                jnp.concatenate([kc_a, k_a], axis=1), jnp.concatenate([vc_a, v_a], axis=1))

    a, u = rglru_coefficients(centred_depthwise_conv(lx, conv_w, conv_b), lru_gate_w, lru_gate_b, lru_lambda)
    ac, uc = rglru_coefficients(centred_depthwise_conv(c_lx, conv_w, conv_b), lru_gate_w, lru_gate_b, lru_lambda)
    zero = jnp.zeros((B, LRU_WIDTH), jnp.float32)
    hc_f = linear_scan(ac[0], uc[0], zero, False)
    hc_b = linear_scan(ac[1], uc[1], zero, True)
    h_f = linear_scan(a[0], u[0], hc_f[:, -1], False)
    h_b = linear_scan(a[1], u[1], hc_b[:, 0], True)
    o_b = (h_f + h_b).astype(h.dtype) * jax.nn.gelu(lg)

    sink = swa_sink.reshape(SWA_KV_HEADS, SWA_GROUP)
    k_c, v_c = swa_key_value(sk, sv, swa_k_g, rope_swa)
    kc_c, vc_c = swa_key_value(c_sk, c_sv, swa_k_g, None)
    o_c = window_attention(swa_query(sq, swa_q_g, rope_swa), k_c, v_c, kc_c, vc_c, sink)

    y = merge_groups(o_a, o_b, o_c, group_g, w_out)
    if not with_ctx_out:
        return y, None
    oc_a = dense_block_attention(mla_query(c_cq, q_a_g, w_uq, mla_q_g, None), kc_a, vc_a)
    oc_b = (hc_f + hc_b).astype(hc.dtype) * jax.nn.gelu(c_lg)
    oc_c = context_attention(swa_query(c_sq, swa_q_g, None), kc_c, vc_c, sink)
    return y, merge_groups(oc_a, oc_b, oc_c, group_g, w_out)


def setup_inputs(seed: int = 0) -> dict:
    key = jax.random.key(seed)
    ks = jax.random.split(key, 32)

    def nrm(k, shape, fan_in, gain=1.0):
        return gain * fan_in ** -0.5 * jax.random.normal(k, shape, jnp.float32)

    def gains(k, shape):
        return 1.0 + 0.05 * jax.random.normal(k, shape, jnp.float32)

    u = jax.random.uniform(ks[18], (DEPTH, 2, LRU_WIDTH), jnp.float32, minval=0.9, maxval=0.999)
    a_base = u ** (1.0 / LRU_C)
    return {
        'x': jax.random.normal(ks[0], (BATCH, SEQ, D_MODEL), jnp.float32),
        'c': jax.random.normal(ks[1], (BATCH, D_MODEL), jnp.float32),
        'ctx': jax.random.normal(ks[2], (BATCH, CTX_LEN, D_MODEL), jnp.float32),
        'c_ctx': jax.random.normal(ks[3], (D_MODEL,), jnp.float32),
        'w_mod': nrm(ks[4], (DEPTH, D_MODEL, N_MOD * D_MODEL), D_MODEL, 0.5),
        'b_mod': 0.02 * jax.random.normal(ks[5], (DEPTH, N_MOD * D_MODEL), jnp.float32),
        'norm1_g': gains(ks[6], (DEPTH, D_MODEL)),
        'w_in': nrm(ks[7], (DEPTH, D_MODEL, IN_WIDTH), D_MODEL),
        'q_a_g': gains(ks[8], (DEPTH, MLA_Q_RANK)),
        'w_uq': nrm(ks[9], (DEPTH, MLA_Q_RANK, MLA_HEADS * MLA_QK), MLA_Q_RANK),
        'kv_a_g': gains(ks[10], (DEPTH, MLA_KV_RANK)),
        'w_ukv': nrm(ks[11], (DEPTH, MLA_KV_RANK, MLA_HEADS * (MLA_NOPE + MLA_V)), MLA_KV_RANK),
        'mla_q_g': gains(ks[12], (DEPTH, MLA_QK)),
        'mla_k_g': gains(ks[13], (DEPTH, MLA_QK)),
        'conv_w': nrm(ks[14], (DEPTH, CONV_W, LRU_WIDTH), CONV_W),
        'conv_b': 0.02 * jax.random.normal(ks[15], (DEPTH, LRU_WIDTH), jnp.float32),
        'lru_gate_w': nrm(ks[16], (DEPTH, 2, 2, LRU_BLOCKS, LRU_BLOCK_DIM, LRU_BLOCK_DIM), LRU_BLOCK_DIM),
        'lru_gate_b': 0.02 * jax.random.normal(ks[17], (DEPTH, 2, 2, LRU_WIDTH), jnp.float32),
        'lru_lambda': jnp.log(a_base) - jnp.log1p(-a_base),
        'swa_q_g': gains(ks[19], (DEPTH, SWA_HEAD_DIM)),
        'swa_k_g': gains(ks[20], (DEPTH, SWA_HEAD_DIM)),
        'swa_sink': jax.random.normal(ks[21], (DEPTH, SWA_HEADS), jnp.float32),
        'group_g': gains(ks[22], (DEPTH, MIX_WIDTH)),
        'w_out': nrm(ks[23], (DEPTH, MIX_WIDTH, D_MODEL), MIX_WIDTH),
        'norm2_g': gains(ks[24], (DEPTH, D_MODEL)),
        'w_ff1': nrm(ks[25], (DEPTH, D_MODEL, D_FF), D_MODEL),
        'w_ff2': nrm(ks[26], (DEPTH, D_FF, D_MODEL), D_FF),
    }


def reference(x, c, ctx, c_ctx, w_mod, b_mod, norm1_g, w_in, q_a_g, w_uq, kv_a_g, w_ukv, mla_q_g, mla_k_g,
              conv_w, conv_b, lru_gate_w, lru_gate_b, lru_lambda, swa_q_g, swa_k_g, swa_sink, group_g, w_out,
              norm2_g, w_ff1, w_ff2):
    rows = x.shape[1] // GRID_W
    rope_mla = axial_rope_tables(rows, MLA_ROPE)
    rope_swa = axial_rope_tables(rows, SWA_HEAD_DIM)
    act_c = jax.nn.silu(c)
    act_cc = jax.nn.silu(c_ctx)
    xc = ctx
    for layer in range(DEPTH):
        last = layer == DEPTH - 1
        sh1, sc1, g1, sh2, sc2, g2 = jnp.split((act_c @ w_mod[layer] + b_mod[layer])[:, None, :], N_MOD, axis=-1)
        csh1, csc1, cg1, csh2, csc2, cg2 = jnp.split(act_cc @ w_mod[layer] + b_mod[layer], N_MOD, axis=-1)
        y, yc = token_mixers(modulate(x, norm1_g[layer], sh1, sc1), modulate(xc, norm1_g[layer], csh1, csc1),
                             w_in[layer], q_a_g[layer], w_uq[layer], kv_a_g[layer], w_ukv[layer],
                             mla_q_g[layer], mla_k_g[layer], conv_w[layer], conv_b[layer],
                             lru_gate_w[layer], lru_gate_b[layer], lru_lambda[layer],
                             swa_q_g[layer], swa_k_g[layer], swa_sink[layer], group_g[layer], w_out[layer],
                             rope_mla, rope_swa, not last)
        x = x + g1 * y
        x = x + g2 * sq_relu_mlp(modulate(x, norm2_g[layer], sh2, sc2), w_ff1[layer], w_ff2[layer])
        if not last:
            xc = xc + cg1 * yc
            xc = xc + cg2 * sq_relu_mlp(modulate(xc, norm2_g[layer], csh2, csc2), w_ff1[layer], w_ff2[layer])
    return x
```

```python
import functools

import jax
import jax.numpy as jnp
from jax import lax
from jax.experimental import pallas as pl
from jax.experimental.pallas import tpu as pltpu

F32 = jnp.float32
BF16 = jnp.bfloat16

GRID_W = 64
WINDOW = 128
ROPE_THETA = 10000.0
EPS = 1e-6
NEG_INF = -1e30
N_MOD = 6
MLA_HEADS = 8
MLA_NOPE = 64
MLA_ROPE = 32
MLA_QK = MLA_NOPE + MLA_ROPE
MLA_V = 64
MLA_Q_RANK = 256
MLA_KV_RANK = 128
LRU_WIDTH = 512
LRU_BLOCK_DIM = 64
LRU_C = 8.0
CONV_W = 4
SWA_HEADS = 8
SWA_KV_HEADS = 2
SWA_GROUP = SWA_HEADS // SWA_KV_HEADS
SWA_HEAD_DIM = 64
GROUP_WIDTH = 512

LANES = 128
SUBLANES = 8
V7X_VMEM_BYTES = 64 * 1024 * 1024
VMEM_LIMIT_BYTES = V7X_VMEM_BYTES - 8 * 1024 * 1024

OFF_CQ = 0
OFF_CKV = OFF_CQ + MLA_Q_RANK
OFF_LX = OFF_CKV + MLA_KV_RANK
OFF_LG = OFF_LX + LRU_WIDTH
OFF_SQ = OFF_LG + LRU_WIDTH
OFF_SK = OFF_SQ + SWA_HEADS * SWA_HEAD_DIM
OFF_SV = OFF_SK + SWA_KV_HEADS * SWA_HEAD_DIM
OFF_KR = OFF_SV + SWA_KV_HEADS * SWA_HEAD_DIM
W_IN_PAD = OFF_KR + LANES

MLA_QW = MLA_HEADS * LANES
MLA_VW = MLA_HEADS * MLA_V
MLA_HEADS_PER_STEP = 4
SWA_QW = SWA_HEADS * SWA_HEAD_DIM
SWA_KVW = 4 * LANES
FF_CHUNK = 1024


def _dot(a, b):
    return jnp.dot(a, b, preferred_element_type=F32)


def _dot_nt(a, b):
    return lax.dot_general(a, b, (((1,), (1,)), ((), ())), preferred_element_type=F32)


def _rms(x):
    return x * lax.rsqrt(jnp.mean(x * x, axis=-1, keepdims=True) + EPS)


def _params(*sem):
    return pltpu.CompilerParams(dimension_semantics=sem, vmem_limit_bytes=VMEM_LIMIT_BYTES)


def _const_spec(shape):
    zeros = (0,) * len(shape)
    return pl.BlockSpec(shape, lambda *_: zeros)


def _mod_kernel(c_ref, w_ref, b_ref, o_ref):
    c = c_ref[...]
    a = c * jax.nn.sigmoid(c)
    w = w_ref[0]
    a_hi = a.astype(BF16)
    a_lo = (a - a_hi.astype(F32)).astype(BF16)
    w_hi = w.astype(BF16)
    w_lo = (w - w_hi.astype(F32)).astype(BF16)
    o_ref[0] = _dot(a_hi, w_hi) + _dot(a_hi, w_lo) + _dot(a_lo, w_hi) + b_ref[0]


def _modulation(cc, w_mod, b_mod):
    depth, d, n = w_mod.shape
    rows = cc.shape[0]
    tn = n // 4
    return pl.pallas_call(
        _mod_kernel,
        grid=(depth, n // tn),
        in_specs=[
            pl.BlockSpec((rows, d), lambda l, j: (0, 0)),
            pl.BlockSpec((1, d, tn), lambda l, j: (l, 0, j)),
            pl.BlockSpec((1, 1, tn), lambda l, j: (l, 0, j)),
        ],
        out_specs=pl.BlockSpec((1, rows, tn), lambda l, j: (l, 0, j)),
        out_shape=jax.ShapeDtypeStruct((depth, rows, n), F32),
        compiler_params=_params("arbitrary", "arbitrary"),
        name="modulation",
    )(cc, w_mod, b_mod.reshape(depth, 1, n))


def _rope(y, cos, sin_a, sin_b, quarter):
    return y * cos + pltpu.roll(y, LANES - quarter, 1) * sin_a + pltpu.roll(y, quarter, 1) * sin_b


def _premix_kernel(x_ref, mod_ref, n1g_ref, win_ref, qag_ref, wuq_ref, kvag_ref, wukv_ref,
                   gq_ref, gk_ref, gqs_ref, gks_ref, cm_ref, sam_ref, sbm_ref, cs_ref, sas_ref, sbs_ref,
                   qm_ref, km_ref, vm_ref, lx_ref, lg_ref, qs_ref, kvs_ref):
    d = x_ref.shape[-1]
    tm = x_ref.shape[1]
    x = x_ref[0]
    shift = mod_ref[0, :, 0:d]
    scale = mod_ref[0, :, d:2 * d]
    h = _rms(x) * n1g_ref[...] * (1.0 + scale) + shift
    p = _dot(h.astype(BF16), win_ref[...])

    lx_ref[0] = p[:, OFF_LX:OFF_LX + LRU_WIDTH]
    lg_ref[0] = p[:, OFF_LG:OFF_LG + LRU_WIDTH]

    cm, sam, sbm = cm_ref[...], sam_ref[...], sbm_ref[...]
    q_quarter = MLA_ROPE // 4

    def head_norm_rope(xh, g):
        r = lax.rsqrt(jnp.sum(xh * xh, axis=-1, keepdims=True) * (1.0 / MLA_QK) + EPS)
        return _rope(xh * r * g, cm, sam, sbm, q_quarter)

    qn = _rms(p[:, OFF_CQ:OFF_CQ + MLA_Q_RANK]) * qag_ref[...]
    qu = _dot(qn.astype(BF16), wuq_ref[...])
    for hd in range(MLA_HEADS):
        sl = slice(hd * LANES, (hd + 1) * LANES)
        qm_ref[0, :, sl] = head_norm_rope(qu[:, sl], gq_ref[:, sl]).astype(BF16)

    kvn = _rms(p[:, OFF_CKV:OFF_CKV + MLA_KV_RANK]) * kvag_ref[...]
    kvu = _dot(kvn.astype(BF16), wukv_ref[...])
    kr = pltpu.roll(p[:, OFF_KR:OFF_KR + LANES], MLA_NOPE, 1)
    for hd in range(MLA_HEADS):
        sl = slice(hd * LANES, (hd + 1) * LANES)
        km_ref[0, :, sl] = head_norm_rope(kvu[:, sl] + kr, gk_ref[:, sl]).astype(BF16)
    vm_ref[0] = kvu[:, MLA_QW:MLA_QW + MLA_VW].astype(BF16)

    cs, sas, sbs = cs_ref[...], sas_ref[...], sbs_ref[...]
    s_quarter = SWA_HEAD_DIM // 4
    lo = lax.broadcasted_iota(jnp.int32, (tm, LANES), 1) < SWA_HEAD_DIM

    def pair_norm_rope(xg, g):
        sq = xg * xg
        s_lo = jnp.sum(jnp.where(lo, sq, 0.0), axis=-1, keepdims=True)
        s_hi = jnp.sum(jnp.where(lo, 0.0, sq), axis=-1, keepdims=True)
        inv = 1.0 / SWA_HEAD_DIM
        r = jnp.where(lo, lax.rsqrt(s_lo * inv + EPS), lax.rsqrt(s_hi * inv + EPS))
        return _rope(xg * r * g, cs, sas, sbs, s_quarter)

    for g in range(SWA_QW // LANES):
        sl = slice(g * LANES, (g + 1) * LANES)
        qs_ref[0, :, sl] = pair_norm_rope(p[:, OFF_SQ + g * LANES:OFF_SQ + (g + 1) * LANES], gqs_ref[:, sl]).astype(BF16)
    kn = pair_norm_rope(p[:, OFF_SK:OFF_SK + LANES], gks_ref[...])
    vn = p[:, OFF_SV:OFF_SV + LANES]
    kvs_ref[0, :, 0:LANES] = kn.astype(BF16)
    kvs_ref[0, :, LANES:2 * LANES] = pltpu.roll(kn, SWA_HEAD_DIM, 1).astype(BF16)
    kvs_ref[0, :, 2 * LANES:3 * LANES] = vn.astype(BF16)
    kvs_ref[0, :, 3 * LANES:4 * LANES] = pltpu.roll(vn, SWA_HEAD_DIM, 1).astype(BF16)


def _premix(x, mod, wts, rope_m, rope_s):
    b, l, d = x.shape
    tm = min(l, 512)
    consts = [wts["n1g"], wts["w_in"], wts["qag"], wts["w_uq"], wts["kvag"], wts["w_ukv"],
              wts["gq"], wts["gk"], wts["gqs"], wts["gks"]]
    tables = list(rope_m) + list(rope_s)
    tok = lambda w: pl.BlockSpec((1, tm, w), lambda bi, i: (bi, i, 0))
    out_w = [(MLA_QW, BF16), (MLA_QW, BF16), (MLA_VW, BF16), (LRU_WIDTH, F32), (LRU_WIDTH, F32),
             (SWA_QW, BF16), (SWA_KVW, BF16)]
    return pl.pallas_call(
        _premix_kernel,
        grid=(b, l // tm),
        in_specs=[tok(d), pl.BlockSpec((1, 1, N_MOD * d), lambda bi, i: (bi, 0, 0))]
        + [_const_spec(c.shape) for c in consts]
        + [pl.BlockSpec((tm, LANES), lambda bi, i: (i, 0)) for _ in tables],
        out_specs=[tok(w) for w, _ in out_w],
        out_shape=[jax.ShapeDtypeStruct((b, l, w), dt) for w, dt in out_w],
        compiler_params=_params("arbitrary", "arbitrary"),
        name="premix",
    )(x, mod, *consts, *tables)


def _gelu_tanh(x):
    return 0.5 * x * (1.0 + jnp.tanh(0.7978845608028654 * (x + 0.044715 * (x * x * x))))


def _lru_kernel(lxc_ref, lxl_ref, lgc_ref, lgl_ref, cw_ref, cb_ref, wg_ref, bg_ref, lam_ref,
                obl_ref, obc_ref, af, uf, ab, ub):
    lc = lxc_ref.shape[1]
    ll = lxl_ref.shape[1]
    cw = cw_ref[...]
    cb = cb_ref[...]
    nlam = -lam_ref[...]
    softplus = jnp.maximum(nlam, 0.0) + jnp.log1p(jnp.exp(-jnp.abs(nlam)))

    def coefficients(x):
        t = x.shape[0]
        row = lax.broadcasted_iota(jnp.int32, (t, LANES), 0)
        xm2 = jnp.where(row >= 2, pltpu.roll(x, 2, 0), 0.0)
        xm1 = jnp.where(row >= 1, pltpu.roll(x, 1, 0), 0.0)
        xp1 = jnp.where(row < t - 1, pltpu.roll(x, t - 1, 0), 0.0)
        xc = cw[0:1] * xm2 + cw[1:2] * xm1 + cw[2:3] * x + cw[3:4] * xp1 + cb
        gates = jax.nn.sigmoid(_dot(xc.astype(BF16), wg_ref[0]) + bg_ref[0])
        out = []
        for z in range(2):
            r = gates[:, 2 * z * LANES:(2 * z + 1) * LANES]
            i = gates[:, (2 * z + 1) * LANES:(2 * z + 2) * LANES]
            log_a = (-LRU_C) * r * softplus[z:z + 1]
            a = jnp.exp(log_a)
            out.append((a, jnp.sqrt(1.0 - a * a) * (i * xc)))
        return out

    def scan8(a, u, reverse):
        t = a.shape[0]
        r8 = lax.broadcasted_iota(jnp.int32, (t, LANES), 0) & (SUBLANES - 1)
        for dist in (1, 2, 4):
            if reverse:
                m = r8 < SUBLANES - dist
                a_s, u_s = pltpu.roll(a, t - dist, 0), pltpu.roll(u, t - dist, 0)
            else:
                m = r8 >= dist
                a_s, u_s = pltpu.roll(a, dist, 0), pltpu.roll(u, dist, 0)
            u = jnp.where(m, a * u_s + u, u)
            a = jnp.where(m, a * a_s, a)
        return a, u

    for x_ref, f_off, b_off in ((lxc_ref, 0, ll), (lxl_ref, lc, 0)):
        t = x_ref.shape[1]
        (a_f, u_f), (a_b, u_b) = coefficients(x_ref[0])
        a_f, u_f = scan8(a_f, u_f, False)
        a_b, u_b = scan8(a_b, u_b, True)
        af[f_off:f_off + t, :] = a_f
        uf[f_off:f_off + t, :] = u_f
        ab[b_off:b_off + t, :] = a_b
        ub[b_off:b_off + t, :] = u_b

    nblk = (lc + ll) // SUBLANES

    def carry(i, state):
        h_f, h_b = state
        o_f = pl.multiple_of(i * SUBLANES, SUBLANES)
        o_b = pl.multiple_of((nblk - 1 - i) * SUBLANES, SUBLANES)
        blk_f = af[pl.ds(o_f, SUBLANES), :] * h_f + uf[pl.ds(o_f, SUBLANES), :]
        blk_b = ab[pl.ds(o_b, SUBLANES), :] * h_b + ub[pl.ds(o_b, SUBLANES), :]
        uf[pl.ds(o_f, SUBLANES), :] = blk_f
        ub[pl.ds(o_b, SUBLANES), :] = blk_b
        return blk_f[SUBLANES - 1:SUBLANES], blk_b[0:1]

    zero = jnp.zeros((1, LANES), F32)
    lax.fori_loop(0, nblk, carry, (zero, zero), unroll=4)

    obl_ref[0] = (uf[lc:lc + ll, :] + ub[0:ll, :]) * _gelu_tanh(lgl_ref[0])
    obc_ref[0] = (uf[0:lc, :] + ub[ll:ll + lc, :]) * _gelu_tanh(lgc_ref[0])


def _rglru(lx_c, lx_l, lg_c, lg_l, wts):
    b, lc, w = lx_c.shape
    ll = lx_l.shape[1]
    ng = w // LANES
    seq = lambda t: pl.BlockSpec((1, t, LANES), lambda bi, g: (bi, 0, g))
    return pl.pallas_call(
        _lru_kernel,
        grid=(b, ng),
        in_specs=[seq(lc), seq(ll), seq(lc), seq(ll),
                  pl.BlockSpec((CONV_W, LANES), lambda bi, g: (0, g)),
                  pl.BlockSpec((1, LANES), lambda bi, g: (0, g)),
                  pl.BlockSpec((1, LANES, 4 * LANES), lambda bi, g: (g, 0, 0)),
                  pl.BlockSpec((1, 1, 4 * LANES), lambda bi, g: (g, 0, 0)),
                  pl.BlockSpec((2, LANES), lambda bi, g: (0, g))],
        out_specs=[seq(ll), seq(lc)],
        out_shape=[jax.ShapeDtypeStruct((b, ll, w), F32), jax.ShapeDtypeStruct((b, lc, w), F32)],
        scratch_shapes=[pltpu.VMEM((lc + ll, LANES), F32) for _ in range(4)],
        compiler_params=_params("arbitrary", "arbitrary"),
        name="rglru",
    )(lx_c, lx_l, lg_c, lg_l, wts["conv_w"], wts["conv_b"], wts["w_gate"], wts["b_gate"], wts["lam"])


def _mla_kernel(*refs, with_latent):
    if with_latent:
        q_ref, kc_ref, vc_ref, kl_ref, vl_ref, o_ref = refs
    else:
        q_ref, kc_ref, vc_ref, o_ref = refs
    tq = q_ref.shape[1]
    vw = o_ref.shape[-1]
    lane = lax.broadcasted_iota(jnp.int32, (tq, vw), 1)
    acc = jnp.zeros((tq, vw), F32)
    for hd in range(MLA_HEADS_PER_STEP):
        sl = slice(hd * LANES, (hd + 1) * LANES)
        q = q_ref[0, :, sl]
        s_c = _dot_nt(q, kc_ref[0, :, sl])
        m = jnp.max(s_c, axis=-1, keepdims=True)
        if with_latent:
            s_l = _dot_nt(q, kl_ref[0, :, sl])
            m = jnp.maximum(m, jnp.max(s_l, axis=-1, keepdims=True))
        p_c = jnp.exp(s_c - m)
        denom = jnp.sum(p_c, axis=-1, keepdims=True)
        o = _dot(p_c.astype(BF16), vc_ref[0])
        if with_latent:
            p_l = jnp.exp(s_l - m)
            denom = denom + jnp.sum(p_l, axis=-1, keepdims=True)
            o = o + _dot(p_l.astype(BF16), vl_ref[0])
        o = o * (1.0 / denom)
        acc = jnp.where((lane >= hd * MLA_V) & (lane < (hd + 1) * MLA_V), o, acc)
    o_ref[0] = acc.astype(o_ref.dtype)


def _mla_attention(q, k_c, v_c, k_l=None, v_l=None):
    b, lq, _ = q.shape
    lc = k_c.shape[1]
    with_latent = k_l is not None
    tq = min(lq, 256)
    qw = MLA_HEADS_PER_STEP * LANES
    vw = MLA_HEADS_PER_STEP * MLA_V
    nq = MLA_HEADS // MLA_HEADS_PER_STEP
    kv = lambda t, w: pl.BlockSpec((1, t, w), lambda bi, hq, i: (bi, 0, hq))
    in_specs = [pl.BlockSpec((1, tq, qw), lambda bi, hq, i: (bi, i, hq)), kv(lc, qw), kv(lc, vw)]
    args = [q, k_c, v_c]
    if with_latent:
        in_specs += [kv(k_l.shape[1], qw), kv(k_l.shape[1], vw)]
        args += [k_l, v_l]
    return pl.pallas_call(
        functools.partial(_mla_kernel, with_latent=with_latent),
        grid=(b, nq, lq // tq),
        in_specs=in_specs,
        out_specs=pl.BlockSpec((1, tq, vw), lambda bi, hq, i: (bi, i, hq)),
        out_shape=jax.ShapeDtypeStruct((b, lq, MLA_VW), BF16),
        compiler_params=_params("arbitrary", "arbitrary", "arbitrary"),
        name="mla_attention" if with_latent else "mla_attention_ctx",
    )(*args)


def _swa_kernel(*refs, with_window):
    if with_window:
        sink_ref, q_ref, kvc_ref, kvl_ref, o_ref = refs
    else:
        sink_ref, q_ref, kvc_ref, o_ref = refs
    tq = q_ref.shape[1]
    kvc = kvc_ref[0]
    if with_window:
        ll = kvl_ref.shape[1]
        span = tq + 2 * WINDOW
        q0 = pl.program_id(1) * tq
        w0 = pl.multiple_of(jnp.clip(q0 - WINDOW, 0, ll - span), LANES)
        kvw = kvl_ref[0, pl.ds(w0, span), :]
        qi = q0 + lax.broadcasted_iota(jnp.int32, (tq, span), 0)
        kj = w0 + lax.broadcasted_iota(jnp.int32, (tq, span), 1)
        band = jnp.abs(qi - kj) <= WINDOW
    lo = lax.broadcasted_iota(jnp.int32, (tq, LANES), 1) < SWA_HEAD_DIM
    for g in range(SWA_QW // LANES):
        qg = q_ref[0, :, g * LANES:(g + 1) * LANES].astype(F32)
        halves = []
        for odd in range(2):
            head = 2 * g + odd
            kv_head = head // SWA_GROUP
            c = 0 if odd == kv_head else 1
            k_sl = slice(c * LANES, (c + 1) * LANES)
            v_sl = slice((2 + c) * LANES, (3 + c) * LANES)
            qh = (jnp.where(lo, 0.0, qg) if odd else jnp.where(lo, qg, 0.0)).astype(BF16)
            sink = sink_ref[head]
            s_c = _dot_nt(qh, kvc[:, k_sl])
            m = jnp.maximum(jnp.max(s_c, axis=-1, keepdims=True), sink)
            if with_window:
                s_w = jnp.where(band, _dot_nt(qh, kvw[:, k_sl]), NEG_INF)
                m = jnp.maximum(m, jnp.max(s_w, axis=-1, keepdims=True))
            p_c = jnp.exp(s_c - m)
            denom = jnp.sum(p_c, axis=-1, keepdims=True) + jnp.exp(sink - m)
            o = _dot(p_c.astype(BF16), kvc[:, v_sl])
            if with_window:
                p_w = jnp.exp(s_w - m)
                denom = denom + jnp.sum(p_w, axis=-1, keepdims=True)
                o = o + _dot(p_w.astype(BF16), kvw[:, v_sl])
            halves.append(o * (1.0 / denom))
        o_ref[0, :, g * LANES:(g + 1) * LANES] = jnp.where(lo, halves[0], halves[1]).astype(o_ref.dtype)


def _swa_attention(q, kv_c, sink, kv_l=None):
    b, lq, _ = q.shape
    lc = kv_c.shape[1]
    with_window = kv_l is not None
    tq = WINDOW if with_window else lq
    in_specs = [pl.BlockSpec(memory_space=pltpu.SMEM),
                pl.BlockSpec((1, tq, SWA_QW), lambda bi, i: (bi, i, 0)),
                pl.BlockSpec((1, lc, SWA_KVW), lambda bi, i: (bi, 0, 0))]
    args = [sink, q, kv_c]
    if with_window:
        in_specs.append(pl.BlockSpec((1, kv_l.shape[1], SWA_KVW), lambda bi, i: (bi, 0, 0)))
        args.append(kv_l)
    return pl.pallas_call(
        functools.partial(_swa_kernel, with_window=with_window),
        grid=(b, lq // tq),
        in_specs=in_specs,
        out_specs=pl.BlockSpec((1, tq, SWA_QW), lambda bi, i: (bi, i, 0)),
        out_shape=jax.ShapeDtypeStruct((b, lq, SWA_QW), BF16),
        compiler_params=_params("arbitrary", "arbitrary"),
        name="swa_attention" if with_window else "swa_attention_ctx",
    )(*args)


def _postmix_kernel(x_ref, oa_ref, ob_ref, oc_ref, mod_ref, gg_ref, wout_ref, n2g_ref, w1_ref, w2_ref, o_ref):
    d = x_ref.shape[-1]
    gate1 = mod_ref[0, :, 2 * d:3 * d]
    shift2 = mod_ref[0, :, 3 * d:4 * d]
    scale2 = mod_ref[0, :, 4 * d:5 * d]
    gate2 = mod_ref[0, :, 5 * d:6 * d]
    y = None
    for gi, ref in enumerate((oa_ref, ob_ref, oc_ref)):
        sl = slice(gi * GROUP_WIDTH, (gi + 1) * GROUP_WIDTH)
        og = _rms(ref[0].astype(F32)) * gg_ref[:, sl]
        part = _dot(og.astype(BF16), wout_ref[sl, :])
        y = part if y is None else y + part
    x1 = x_ref[0] + gate1 * y
    h2 = (_rms(x1) * n2g_ref[...] * (1.0 + scale2) + shift2).astype(BF16)
    ff = None
    for j in range(w1_ref.shape[1] // FF_CHUNK):
        sl = slice(j * FF_CHUNK, (j + 1) * FF_CHUNK)
        hid = jnp.maximum(_dot(h2, w1_ref[:, sl]), 0.0)
        part = _dot((hid * hid).astype(BF16), w2_ref[sl, :])
        ff = part if ff is None else ff + part
    o_ref[0] = x1 + gate2 * ff


def _postmix(x, o_a, o_b, o_c, mod, wts):
    b, l, d = x.shape
    tm = min(l, 256)
    tok = lambda w: pl.BlockSpec((1, tm, w), lambda bi, i: (bi, i, 0))
    consts = [wts["gg"], wts["w_out"], wts["n2g"], wts["w_ff1"], wts["w_ff2"]]
    return pl.pallas_call(
        _postmix_kernel,
        grid=(b, l // tm),
        in_specs=[tok(d), tok(GROUP_WIDTH), tok(GROUP_WIDTH), tok(GROUP_WIDTH),
                  pl.BlockSpec((1, 1, N_MOD * d), lambda bi, i: (bi, 0, 0))]
        + [_const_spec(c.shape) for c in consts],
        out_specs=tok(d),
        out_shape=jax.ShapeDtypeStruct((b, l, d), F32),
        compiler_params=_params("arbitrary", "arbitrary"),
        name="postmix",
    )(x, o_a, o_b, o_c, mod, *consts)


def _rope_tables(rows, dim, lead, width):
    quarter = dim // 4
    n = rows * GRID_W
    row = jnp.repeat(jnp.arange(rows), GRID_W)
    col = jnp.tile(jnp.arange(GRID_W), rows)
    inv_freq = ROPE_THETA ** (-jnp.arange(quarter, dtype=F32) / quarter)
    ang = jnp.stack([row, col], axis=-1).astype(F32)[:, :, None] * inv_freq
    ang = jnp.broadcast_to(ang[:, :, None, :], (n, 2, 2, quarter)).reshape(n, dim)
    cos, sin = jnp.cos(ang), jnp.sin(ang)
    is_a = (jnp.arange(dim) // quarter) % 2 == 0
    sin_a = jnp.where(is_a, -sin, 0.0)
    sin_b = jnp.where(is_a, 0.0, sin)

    def place(t, fill):
        t = jnp.concatenate([jnp.full((n, lead), fill, F32), t], axis=1)
        t = jnp.tile(t, (1, width // (lead + dim)))
        return jnp.concatenate([t, jnp.full((n, width - t.shape[1]), fill, F32)], axis=1)

    return place(cos, 1.0), place(sin_a, 0.0), place(sin_b, 0.0)


def _identity_tables(n, width):
    return jnp.ones((n, width), F32), jnp.zeros((n, width), F32), jnp.zeros((n, width), F32)


def _pad_heads(w, heads, width):
    lead = w.shape[:-1]
    w = w.reshape(lead + (heads, width))
    w = jnp.pad(w, [(0, 0)] * len(lead) + [(0, 0), (0, LANES - width)])
    return w.reshape(lead + (heads * LANES,))


def _layer_weights(layer, norm1_g, w_in, q_a_g, w_uq, kv_a_g, w_ukv, mla_q_g, mla_k_g, conv_w, conv_b,
                   lru_gate_w, lru_gate_b, lru_lambda, swa_q_g, swa_k_g, swa_sink, group_g, w_out,
                   norm2_g, w_ff1, w_ff2):
    d = w_in.shape[1]
    cq, ckv, kr, lx, lg, sq, sk, sv = _split_in(w_in[layer])
    w_in_p = jnp.concatenate([cq, ckv, lx, lg, sq, sk, sv, kr, jnp.zeros((d, LANES - MLA_ROPE), F32)], axis=1)
    kv = w_ukv[layer].reshape(MLA_KV_RANK, MLA_HEADS, MLA_NOPE + MLA_V)
    w_k = _pad_heads(kv[:, :, :MLA_NOPE].reshape(MLA_KV_RANK, MLA_HEADS * MLA_NOPE), MLA_HEADS, MLA_NOPE)
    w_v = kv[:, :, MLA_NOPE:].reshape(MLA_KV_RANK, MLA_VW)
    gw = lru_gate_w[layer]
    nblk = gw.shape[2]
    per_group = LANES // LRU_BLOCK_DIM
    eye = jnp.eye(per_group, dtype=F32)
    gw = gw.reshape(2, 2, nblk // per_group, per_group, LRU_BLOCK_DIM, LRU_BLOCK_DIM)
    w_gate = jnp.einsum("zgpncm,nk->pnczgkm", gw, eye).reshape(nblk // per_group, LANES, 4 * LANES)
    b_gate = lru_gate_b[layer].reshape(2, 2, nblk // per_group, LANES).transpose(2, 0, 1, 3).reshape(
        nblk // per_group, 1, 4 * LANES)
    return {
        "n1g": norm1_g[layer][None], "w_in": w_in_p.astype(BF16), "qag": q_a_g[layer][None],
        "w_uq": _pad_heads(w_uq[layer], MLA_HEADS, MLA_QK).astype(BF16), "kvag": kv_a_g[layer][None],
        "w_ukv": jnp.concatenate([w_k, w_v], axis=1).astype(BF16),
        "gq": _pad_heads(jnp.tile(mla_q_g[layer] * MLA_QK ** -0.5, MLA_HEADS)[None], MLA_HEADS, MLA_QK),
        "gk": _pad_heads(jnp.tile(mla_k_g[layer], MLA_HEADS)[None], MLA_HEADS, MLA_QK),
        "gqs": jnp.tile(swa_q_g[layer] * SWA_HEAD_DIM ** -0.5, SWA_HEADS)[None],
        "gks": jnp.tile(swa_k_g[layer], SWA_KV_HEADS)[None],
        "conv_w": conv_w[layer], "conv_b": conv_b[layer][None], "w_gate": w_gate.astype(BF16), "b_gate": b_gate,
        "lam": lru_lambda[layer], "sink": swa_sink[layer],
        "gg": group_g[layer][None], "w_out": w_out[layer].astype(BF16), "n2g": norm2_g[layer][None],
        "w_ff1": w_ff1[layer].astype(BF16), "w_ff2": w_ff2[layer].astype(BF16),
    }


def _split_in(w):
    sizes = (MLA_Q_RANK, MLA_KV_RANK, MLA_ROPE, LRU_WIDTH, LRU_WIDTH, SWA_QW,
             SWA_KV_HEADS * SWA_HEAD_DIM, SWA_KV_HEADS * SWA_HEAD_DIM)
    parts, start = [], 0
    for s in sizes:
        parts.append(w[:, start:start + s])
        start += s
    return parts


def kernel(x, c, ctx, c_ctx, w_mod, b_mod, norm1_g, w_in, q_a_g, w_uq, kv_a_g, w_ukv, mla_q_g, mla_k_g, conv_w, conv_b, lru_gate_w, lru_gate_b, lru_lambda, swa_q_g, swa_k_g, swa_sink, group_g, w_out, norm2_g, w_ff1, w_ff2):
    b, l, d = x.shape
    lc = ctx.shape[1]
    depth = w_mod.shape[0]
    rows = l // GRID_W

    n_rows = -(-(b + 1) // SUBLANES) * SUBLANES
    cc = jnp.concatenate([c, c_ctx[None], jnp.zeros((n_rows - b - 1, d), F32)], axis=0)
    mod_all = _modulation(cc, w_mod, b_mod)

    rope_m = _rope_tables(rows, MLA_ROPE, MLA_NOPE, LANES)
    rope_s = _rope_tables(rows, SWA_HEAD_DIM, 0, LANES)
    no_rope = _identity_tables(lc, LANES)

    xc = ctx
    for layer in range(depth):
        last = layer == depth - 1
        wts = _layer_weights(layer, norm1_g, w_in, q_a_g, w_uq, kv_a_g, w_ukv, mla_q_g, mla_k_g, conv_w, conv_b,
                             lru_gate_w, lru_gate_b, lru_lambda, swa_q_g, swa_k_g, swa_sink, group_g, w_out,
                             norm2_g, w_ff1, w_ff2)
        mod_l = mod_all[layer, :b][:, None, :]
        mod_c = jnp.broadcast_to(mod_all[layer, b][None, None, :], (b, 1, N_MOD * d))

        qm, km, vm, lx, lg, qs, kvs = _premix(x, mod_l, wts, rope_m, rope_s)
        qm_c, km_c, vm_c, lx_c, lg_c, qs_c, kvs_c = _premix(xc, mod_c, wts, no_rope, no_rope)

        o_b, o_b_c = _rglru(lx_c, lx, lg_c, lg, wts)
        o_a = _mla_attention(qm, km_c, vm_c, km, vm)
        o_c = _swa_attention(qs, kvs_c, wts["sink"], kvs)
        x = _postmix(x, o_a, o_b, o_c, mod_l, wts)
        if not last:
            o_a_c = _mla_attention(qm_c, km_c, vm_c)
            o_c_c = _swa_attention(qs_c, kvs_c, wts["sink"])
            xc = _postmix(xc, o_a_c, o_b_c, o_c_c, mod_c, wts)
    return x
```

```python
import functools

import numpy as np
import jax
import jax.numpy as jnp
from jax import lax
from jax.experimental import pallas as pl
from jax.experimental.pallas import tpu as pltpu

F32 = jnp.float32
BF16 = jnp.bfloat16

GRID_W = 64
WINDOW = 128
ROPE_THETA = 10000.0
EPS = 1e-6
NEG_INF = -1e30
N_MOD = 6
MLA_HEADS = 8
MLA_NOPE = 64
MLA_ROPE = 32
MLA_QK = MLA_NOPE + MLA_ROPE
MLA_V = 64
MLA_Q_RANK = 256
MLA_KV_RANK = 128
LRU_WIDTH = 512
LRU_BLOCK_DIM = 64
LRU_C = 8.0
CONV_W = 4
SWA_HEADS = 8
SWA_KV_HEADS = 2
SWA_GROUP = SWA_HEADS // SWA_KV_HEADS
SWA_HEAD_DIM = 64
GROUP_WIDTH = 512

LANES = 128
SUBLANES = 8
V7X_VMEM_BYTES = 64 * 1024 * 1024
VMEM_LIMIT_BYTES = V7X_VMEM_BYTES - 8 * 1024 * 1024

MLA_QW = MLA_HEADS * LANES
MLA_VW = MLA_HEADS * MLA_V
MLA_HEADS_PER_STEP = 4
SWA_QW = SWA_HEADS * SWA_HEAD_DIM
SWA_KW = SWA_KV_HEADS * SWA_HEAD_DIM
SWA_KVW = 2 * SWA_KW
SWA_PAIRS = SWA_QW // LANES
SWA_Q_BLOCKS_PER_STEP = 4
FF_CHUNK = 1024

OFF_CQ = 0
OFF_CKV = OFF_CQ + MLA_Q_RANK
OFF_LX = OFF_CKV + MLA_KV_RANK
OFF_LG = OFF_LX + LRU_WIDTH
OFF_SQ = OFF_LG + LRU_WIDTH
OFF_SQR = OFF_SQ + SWA_QW
OFF_SK = OFF_SQR + SWA_QW
OFF_SKR = OFF_SK + SWA_KW
OFF_SV = OFF_SKR + SWA_KW
OFF_KR = OFF_SV + SWA_KW
OFF_KRR = OFF_KR + LANES
W_IN_PAD = OFF_KRR + LANES


def _dot(a, b):
    return jnp.dot(a, b, preferred_element_type=F32)


def _dot_nt(a, b):
    return lax.dot_general(a, b, (((1,), (1,)), ((), ())), preferred_element_type=F32)


def _rms(x):
    return x * lax.rsqrt(jnp.mean(x * x, axis=-1, keepdims=True) + EPS)


def _params(*sem):
    return pltpu.CompilerParams(dimension_semantics=sem, vmem_limit_bytes=VMEM_LIMIT_BYTES)


def _const_spec(shape):
    zeros = (0,) * len(shape)
    return pl.BlockSpec(shape, lambda *_: zeros)


def _mod_kernel(c_ref, w_ref, b_ref, o_ref):
    c = c_ref[...]
    a = c * jax.nn.sigmoid(c)
    w = w_ref[0]
    a_hi = a.astype(BF16)
    a_lo = (a - a_hi.astype(F32)).astype(BF16)
    w_hi = w.astype(BF16)
    w_lo = (w - w_hi.astype(F32)).astype(BF16)
    o_ref[0] = _dot(a_hi, w_hi) + _dot(a_hi, w_lo) + _dot(a_lo, w_hi) + b_ref[0]


def _modulation(cc, w_mod, b_mod):
    depth, d, n = w_mod.shape
    rows = cc.shape[0]
    tn = n // 4
    return pl.pallas_call(
        _mod_kernel,
        grid=(depth, n // tn),
        in_specs=[
            pl.BlockSpec((rows, d), lambda l, j: (0, 0)),
            pl.BlockSpec((1, d, tn), lambda l, j: (l, 0, j)),
            pl.BlockSpec((1, 1, tn), lambda l, j: (l, 0, j)),
        ],
        out_specs=pl.BlockSpec((1, rows, tn), lambda l, j: (l, 0, j)),
        out_shape=jax.ShapeDtypeStruct((depth, rows, n), F32),
        compiler_params=_params("arbitrary", "arbitrary"),
        name="modulation",
    )(cc, w_mod, b_mod.reshape(depth, 1, n))


def _premix_kernel(x_ref, mod_ref, n1g_ref, win_ref, qag_ref, wuq_ref, kvag_ref, wukv_ref,
                   gq_ref, gqr_ref, gk_ref, gkr_ref, gqs_ref, gqsr_ref, gks_ref, gksr_ref,
                   cm_ref, sm_ref, cs_ref, ss_ref,
                   qm_ref, km_ref, vm_ref, lx_ref, lg_ref, qs_ref, kvs_ref):
    d = x_ref.shape[-1]
    tm = x_ref.shape[1]
    x = x_ref[0]
    shift = mod_ref[0, :, 0:d]
    scale = mod_ref[0, :, d:2 * d]
    h = _rms(x) * n1g_ref[...] * (1.0 + scale) + shift
    p = _dot(h.astype(BF16), win_ref[...])

    lx_ref[0] = p[:, OFF_LX:OFF_LX + LRU_WIDTH]
    lg_ref[0] = p[:, OFF_LG:OFF_LG + LRU_WIDTH]

    cm, sm = cm_ref[...], sm_ref[...]

    qn = _rms(p[:, OFF_CQ:OFF_CQ + MLA_Q_RANK]) * qag_ref[...]
    qu = _dot(qn.astype(BF16), wuq_ref[...])
    ga_q, gb_q = gq_ref[...] * cm, gqr_ref[...] * sm
    for hd in range(MLA_HEADS):
        x1 = qu[:, hd * LANES:(hd + 1) * LANES]
        x2 = qu[:, MLA_QW + hd * LANES:MLA_QW + (hd + 1) * LANES]
        r = lax.rsqrt(jnp.sum(x1 * x1, axis=-1, keepdims=True) * (1.0 / MLA_QK) + EPS)
        qm_ref[0, :, hd * LANES:(hd + 1) * LANES] = (r * (x1 * ga_q + x2 * gb_q)).astype(BF16)

    kvn = _rms(p[:, OFF_CKV:OFF_CKV + MLA_KV_RANK]) * kvag_ref[...]
    kvu = _dot(kvn.astype(BF16), wukv_ref[...])
    kr = p[:, OFF_KR:OFF_KR + LANES]
    ga_k = gk_ref[...] * cm
    shared = kr * ga_k + p[:, OFF_KRR:OFF_KRR + LANES] * (gkr_ref[...] * sm)
    ss_kr = jnp.sum(kr * kr, axis=-1, keepdims=True)
    for hd in range(MLA_HEADS):
        x1 = kvu[:, hd * LANES:(hd + 1) * LANES]
        r = lax.rsqrt((jnp.sum(x1 * x1, axis=-1, keepdims=True) + ss_kr) * (1.0 / MLA_QK) + EPS)
        km_ref[0, :, hd * LANES:(hd + 1) * LANES] = (r * (x1 * ga_k + shared)).astype(BF16)
    vm_ref[0] = kvu[:, MLA_QW:MLA_QW + MLA_VW].astype(BF16)

    cs, ss = cs_ref[...], ss_ref[...]
    lo = lax.broadcasted_iota(jnp.int32, (tm, LANES), 1) < SWA_HEAD_DIM

    def pair_norm_rope(x1, x2, ga, gb):
        sq = x1 * x1
        s_lo = jnp.sum(jnp.where(lo, sq, 0.0), axis=-1, keepdims=True)
        s_hi = jnp.sum(jnp.where(lo, 0.0, sq), axis=-1, keepdims=True)
        inv = 1.0 / SWA_HEAD_DIM
        r = jnp.where(lo, lax.rsqrt(s_lo * inv + EPS), lax.rsqrt(s_hi * inv + EPS))
        return (r * (x1 * ga + x2 * gb)).astype(BF16)

    ga_s, gb_s = gqs_ref[...] * cs, gqsr_ref[...] * ss
    for g in range(SWA_PAIRS):
        sl = slice(g * LANES, (g + 1) * LANES)
        qs_ref[0, :, sl] = pair_norm_rope(p[:, OFF_SQ + g * LANES:OFF_SQ + (g + 1) * LANES],
                                          p[:, OFF_SQR + g * LANES:OFF_SQR + (g + 1) * LANES], ga_s, gb_s)
    kvs_ref[0, :, 0:SWA_KW] = pair_norm_rope(p[:, OFF_SK:OFF_SK + SWA_KW], p[:, OFF_SKR:OFF_SKR + SWA_KW],
                                             gks_ref[...] * cs, gksr_ref[...] * ss)
    kvs_ref[0, :, SWA_KW:SWA_KVW] = p[:, OFF_SV:OFF_SV + SWA_KW].astype(BF16)


def _premix(x, mod, wts, rope_m, rope_s):
    b, l, d = x.shape
    tm = min(l, 512)
    consts = [wts[k] for k in ("n1g", "w_in", "qag", "w_uq", "kvag", "w_ukv",
                               "gq", "gq_rot", "gk", "gk_rot", "gqs", "gqs_rot", "gks", "gks_rot")]
    tables = list(rope_m) + list(rope_s)
    tok = lambda w: pl.BlockSpec((1, tm, w), lambda bi, i: (bi, i, 0))
    out_w = [(MLA_QW, BF16), (MLA_QW, BF16), (MLA_VW, BF16), (LRU_WIDTH, F32), (LRU_WIDTH, F32),
             (SWA_QW, BF16), (SWA_KVW, BF16)]
    return pl.pallas_call(
        _premix_kernel,
        grid=(b, l // tm),
        in_specs=[tok(d), pl.BlockSpec((1, 1, N_MOD * d), lambda bi, i: (bi, 0, 0))]
        + [_const_spec(c.shape) for c in consts]
        + [pl.BlockSpec((tm, LANES), lambda bi, i: (i, 0)) for _ in tables],
        out_specs=[tok(w) for w, _ in out_w],
        out_shape=[jax.ShapeDtypeStruct((b, l, w), dt) for w, dt in out_w],
        compiler_params=_params("arbitrary", "arbitrary"),
        name="premix",
    )(x, mod, *consts, *tables)


def _gelu_tanh(x):
    return 0.5 * x * (1.0 + jnp.tanh(0.7978845608028654 * (x + 0.044715 * (x * x * x))))


def _lru_kernel(lxc_ref, lxl_ref, lgc_ref, lgl_ref, cw_ref, cb_ref, wg_ref, bg_ref, lam_ref,
                obl_ref, obc_ref, af, uf, ab, ub):
    lc = lxc_ref.shape[1]
    ll = lxl_ref.shape[1]
    cw = cw_ref[...]
    cb = cb_ref[...]
    nlam = -lam_ref[...]
    softplus = jnp.maximum(nlam, 0.0) + jnp.log1p(jnp.exp(-jnp.abs(nlam)))

    def coefficients(x):
        t = x.shape[0]
        row = lax.broadcasted_iota(jnp.int32, (t, LANES), 0)
        xm2 = jnp.where(row >= 2, pltpu.roll(x, 2, 0), 0.0)
        xm1 = jnp.where(row >= 1, pltpu.roll(x, 1, 0), 0.0)
        xp1 = jnp.where(row < t - 1, pltpu.roll(x, t - 1, 0), 0.0)
        xc = cw[0:1] * xm2 + cw[1:2] * xm1 + cw[2:3] * x + cw[3:4] * xp1 + cb
        gates = jax.nn.sigmoid(_dot(xc.astype(BF16), wg_ref[0]) + bg_ref[0])
        out = []
        for z in range(2):
            r = gates[:, 2 * z * LANES:(2 * z + 1) * LANES]
            i = gates[:, (2 * z + 1) * LANES:(2 * z + 2) * LANES]
            log_a = (-LRU_C) * r * softplus[z:z + 1]
            a = jnp.exp(log_a)
            y = 1.0 - a * a
            root = y * lax.rsqrt(jnp.maximum(y, jnp.finfo(F32).tiny))
            out.append((a, root * (i * xc)))
        return out

    for x_ref, f_off, b_off in ((lxc_ref, 0, ll), (lxl_ref, lc, 0)):
        t = x_ref.shape[1]
        (a_f, u_f), (a_b, u_b) = coefficients(x_ref[0])
        af[f_off:f_off + t, :] = a_f
        uf[f_off:f_off + t, :] = u_f
        ab[b_off:b_off + t, :] = a_b
        ub[b_off:b_off + t, :] = u_b
    pitch = af.shape[0] // SUBLANES
    tail = af.shape[0] - (lc + ll)
    assert tail > 0
    for a_ref, u_ref in ((af, uf), (ab, ub)):
        a_ref[lc + ll:, :] = jnp.ones((tail, LANES), F32)
        u_ref[lc + ll:, :] = jnp.zeros((tail, LANES), F32)

    def streams(i):
        return pl.ds(i, SUBLANES, stride=pitch)

    def local_scan(i, state):
        h_f, p_f, h_b, p_b = state
        a = af[streams(i), :]
        h_f = a * h_f + uf[streams(i), :]
        p_f = a * p_f
        uf[streams(i), :] = h_f
        af[streams(i), :] = p_f
        j = pitch - 1 - i
        a = ab[streams(j), :]
        h_b = a * h_b + ub[streams(j), :]
        p_b = a * p_b
        ub[streams(j), :] = h_b
        ab[streams(j), :] = p_b
        return h_f, p_f, h_b, p_b

    zeros8 = jnp.zeros((SUBLANES, LANES), F32)
    ones8 = jnp.ones((SUBLANES, LANES), F32)
    h_f, p_f, h_b, p_b = lax.fori_loop(0, pitch, local_scan, (zeros8, ones8, zeros8, ones8), unroll=4)

    sub = lax.broadcasted_iota(jnp.int32, (SUBLANES, LANES), 0)
    c_f = zeros8
    c_b = zeros8
    row_f = jnp.zeros((1, LANES), F32)
    row_b = jnp.zeros((1, LANES), F32)
    for s in range(1, SUBLANES):
        row_f = h_f[s - 1:s] + p_f[s - 1:s] * row_f
        c_f = jnp.where(sub == s, row_f, c_f)
        sb = SUBLANES - 1 - s
        row_b = h_b[sb + 1:sb + 2] + p_b[sb + 1:sb + 2] * row_b
        c_b = jnp.where(sub == sb, row_b, c_b)

    def add_carry(i, _):
        uf[streams(i), :] = uf[streams(i), :] + af[streams(i), :] * c_f
        ub[streams(i), :] = ub[streams(i), :] + ab[streams(i), :] * c_b
        return 0

    lax.fori_loop(0, pitch, add_carry, 0, unroll=4)

    obl_ref[0] = (uf[lc:lc + ll, :] + ub[0:ll, :]) * _gelu_tanh(lgl_ref[0])
    obc_ref[0] = (uf[0:lc, :] + ub[ll:ll + lc, :]) * _gelu_tanh(lgc_ref[0])


def _rglru(lx_c, lx_l, lg_c, lg_l, wts):
    b, lc, w = lx_c.shape
    ll = lx_l.shape[1]
    ng = w // LANES
    pitch = -(-(lc + ll) // SUBLANES)
    pitch += (SUBLANES // 2 - pitch) % SUBLANES
    seq = lambda t: pl.BlockSpec((1, t, LANES), lambda bi, g: (bi, 0, g))
    return pl.pallas_call(
        _lru_kernel,
        grid=(b, ng),
        in_specs=[seq(lc), seq(ll), seq(lc), seq(ll),
                  pl.BlockSpec((CONV_W, LANES), lambda bi, g: (0, g)),
                  pl.BlockSpec((1, LANES), lambda bi, g: (0, g)),
                  pl.BlockSpec((1, LANES, 4 * LANES), lambda bi, g: (g, 0, 0)),
                  pl.BlockSpec((1, 1, 4 * LANES), lambda bi, g: (g, 0, 0)),
                  pl.BlockSpec((2, LANES), lambda bi, g: (0, g))],
        out_specs=[seq(ll), seq(lc)],
        out_shape=[jax.ShapeDtypeStruct((b, ll, w), F32), jax.ShapeDtypeStruct((b, lc, w), F32)],
        scratch_shapes=[pltpu.VMEM((SUBLANES * pitch, LANES), F32) for _ in range(4)],
        compiler_params=_params("arbitrary", "arbitrary"),
        name="rglru",
    )(lx_c, lx_l, lg_c, lg_l, wts["conv_w"], wts["conv_b"], wts["w_gate"], wts["b_gate"], wts["lam"])


def _mla_kernel(*refs, with_latent):
    if with_latent:
        q_ref, kc_ref, vc_ref, kl_ref, vl_ref, o_ref = refs
    else:
        q_ref, kc_ref, vc_ref, o_ref = refs
    tq = q_ref.shape[1]
    vw = o_ref.shape[-1]
    lane = lax.broadcasted_iota(jnp.int32, (tq, vw), 1)
    acc = jnp.zeros((tq, vw), F32)
    for hd in range(MLA_HEADS_PER_STEP):
        sl = slice(hd * LANES, (hd + 1) * LANES)
        q = q_ref[0, :, sl]
        s_c = _dot_nt(q, kc_ref[0, :, sl])
        m = jnp.max(s_c, axis=-1, keepdims=True)
        if with_latent:
            s_l = _dot_nt(q, kl_ref[0, :, sl])
            m = jnp.maximum(m, jnp.max(s_l, axis=-1, keepdims=True))
        p_c = jnp.exp(s_c - m)
        denom = jnp.sum(p_c, axis=-1, keepdims=True)
        o = _dot(p_c.astype(BF16), vc_ref[0])
        if with_latent:
            p_l = jnp.exp(s_l - m)
            denom = denom + jnp.sum(p_l, axis=-1, keepdims=True)
            o = o + _dot(p_l.astype(BF16), vl_ref[0])
        o = o * (1.0 / denom)
        acc = jnp.where((lane >= hd * MLA_V) & (lane < (hd + 1) * MLA_V), o, acc)
    o_ref[0] = acc.astype(o_ref.dtype)


def _mla_attention(q, k_c, v_c, k_l=None, v_l=None):
    b, lq, _ = q.shape
    lc = k_c.shape[1]
    with_latent = k_l is not None
    tq = min(lq, 256)
    qw = MLA_HEADS_PER_STEP * LANES
    vw = MLA_HEADS_PER_STEP * MLA_V
    nq = MLA_HEADS // MLA_HEADS_PER_STEP
    kv = lambda t, w: pl.BlockSpec((1, t, w), lambda bi, hq, i: (bi, 0, hq))
    in_specs = [pl.BlockSpec((1, tq, qw), lambda bi, hq, i: (bi, i, hq)), kv(lc, qw), kv(lc, vw)]
    args = [q, k_c, v_c]
    if with_latent:
        in_specs += [kv(k_l.shape[1], qw), kv(k_l.shape[1], vw)]
        args += [k_l, v_l]
    return pl.pallas_call(
        functools.partial(_mla_kernel, with_latent=with_latent),
        grid=(b, nq, lq // tq),
        in_specs=in_specs,
        out_specs=pl.BlockSpec((1, tq, vw), lambda bi, hq, i: (bi, i, hq)),
        out_shape=jax.ShapeDtypeStruct((b, lq, MLA_VW), BF16),
        compiler_params=_params("arbitrary", "arbitrary", "arbitrary"),
        name="mla_attention" if with_latent else "mla_attention_ctx",
    )(*args)


def _swa_kernel(*refs, with_window):
    if with_window:
        sink_ref, q_ref, kvc_ref, kvl_ref, o_ref = refs
    else:
        sink_ref, q_ref, kvc_ref, o_ref = refs
    bq = WINDOW
    rows = SWA_PAIRS * bq
    kc = kvc_ref[0, :, 0:SWA_KW]
    vc = kvc_ref[0, :, SWA_KW:SWA_KVW]
    lo = lax.broadcasted_iota(jnp.int32, (rows, LANES), 1) < SWA_HEAD_DIM
    seg = lax.broadcasted_iota(jnp.int32, (rows, 1), 0) // bq

    def block(sb, _):
        r0 = pl.multiple_of(sb * bq, bq)
        q = jnp.concatenate([q_ref[0, pl.ds(r0, bq), g * LANES:(g + 1) * LANES] for g in range(SWA_PAIRS)],
                            axis=0).astype(F32)
        if with_window:
            ll = kvl_ref.shape[1]
            span = bq + 2 * WINDOW
            q0 = pl.program_id(1) * (SWA_Q_BLOCKS_PER_STEP * bq) + r0
            w0 = pl.multiple_of(jnp.clip(q0 - WINDOW, 0, ll - span), LANES)
            kw = kvl_ref[0, pl.ds(w0, span), 0:SWA_KW]
            vw = kvl_ref[0, pl.ds(w0, span), SWA_KW:SWA_KVW]
            qi = q0 + (lax.broadcasted_iota(jnp.int32, (rows, span), 0) & (bq - 1))
            kj = w0 + lax.broadcasted_iota(jnp.int32, (rows, span), 1)
            band = jnp.abs(qi - kj) <= WINDOW
        halves = []
        for kvh in range(SWA_KV_HEADS):
            qh = (jnp.where(lo, 0.0, q) if kvh else jnp.where(lo, q, 0.0)).astype(BF16)
            sink = jnp.zeros((rows, 1), F32)
            for g in range(SWA_PAIRS):
                sink = jnp.where(seg == g, sink_ref[kvh * SWA_GROUP + g], sink)
            s_c = _dot_nt(qh, kc)
            m = jnp.maximum(jnp.max(s_c, axis=-1, keepdims=True), sink)
            if with_window:
                s_w = jnp.where(band, _dot_nt(qh, kw), NEG_INF)
                m = jnp.maximum(m, jnp.max(s_w, axis=-1, keepdims=True))
            p_c = jnp.exp(s_c - m)
            denom = jnp.sum(p_c, axis=-1, keepdims=True) + jnp.exp(sink - m)
            o = _dot(p_c.astype(BF16), vc)
            if with_window:
                p_w = jnp.exp(s_w - m)
                denom = denom + jnp.sum(p_w, axis=-1, keepdims=True)
                o = o + _dot(p_w.astype(BF16), vw)
            halves.append(o * (1.0 / denom))
        out = jnp.where(lo, halves[0], halves[1]).astype(o_ref.dtype)
        for g in range(SWA_PAIRS):
            o_ref[0, pl.ds(r0, bq), g * LANES:(g + 1) * LANES] = out[g * bq:(g + 1) * bq]
        return 0

    lax.fori_loop(0, q_ref.shape[1] // bq, block, 0)


def _swa_attention(q, kv_c, sink, kv_l=None):
    b, lq, _ = q.shape
    lc = kv_c.shape[1]
    with_window = kv_l is not None
    tq = min(lq, SWA_Q_BLOCKS_PER_STEP * WINDOW)
    in_specs = [pl.BlockSpec(memory_space=pltpu.SMEM),
                pl.BlockSpec((1, tq, SWA_QW), lambda bi, i: (bi, i, 0)),
                pl.BlockSpec((1, lc, SWA_KVW), lambda bi, i: (bi, 0, 0))]
    args = [sink, q, kv_c]
    if with_window:
        in_specs.append(pl.BlockSpec((1, kv_l.shape[1], SWA_KVW), lambda bi, i: (bi, 0, 0)))
        args.append(kv_l)
    return pl.pallas_call(
        functools.partial(_swa_kernel, with_window=with_window),
        grid=(b, lq // tq),
        in_specs=in_specs,
        out_specs=pl.BlockSpec((1, tq, SWA_QW), lambda bi, i: (bi, i, 0)),
        out_shape=jax.ShapeDtypeStruct((b, lq, SWA_QW), BF16),
        compiler_params=_params("arbitrary", "arbitrary"),
        name="swa_attention" if with_window else "swa_attention_ctx",
    )(*args)


def _postmix_kernel(x_ref, oa_ref, ob_ref, oc_ref, mod_ref, gg_ref, wout_ref, n2g_ref, w1_ref, w2_ref, o_ref):
    d = x_ref.shape[-1]
    gate1 = mod_ref[0, :, 2 * d:3 * d]
    shift2 = mod_ref[0, :, 3 * d:4 * d]
    scale2 = mod_ref[0, :, 4 * d:5 * d]
    gate2 = mod_ref[0, :, 5 * d:6 * d]
    y = None
    for gi, ref in enumerate((oa_ref, ob_ref, oc_ref)):
        sl = slice(gi * GROUP_WIDTH, (gi + 1) * GROUP_WIDTH)
        og = _rms(ref[0].astype(F32)) * gg_ref[:, sl]
        part = _dot(og.astype(BF16), wout_ref[sl, :])
        y = part if y is None else y + part
    x1 = x_ref[0] + gate1 * y
    h2 = (_rms(x1) * n2g_ref[...] * (1.0 + scale2) + shift2).astype(BF16)
    ff = None
    for j in range(w1_ref.shape[1] // FF_CHUNK):
        sl = slice(j * FF_CHUNK, (j + 1) * FF_CHUNK)
        hid = jnp.maximum(_dot(h2, w1_ref[:, sl]), 0.0)
        part = _dot((hid * hid).astype(BF16), w2_ref[sl, :])
        ff = part if ff is None else ff + part
    o_ref[0] = x1 + gate2 * ff


def _postmix(x, o_a, o_b, o_c, mod, wts):
    b, l, d = x.shape
    tm = min(l, 256)
    tok = lambda w: pl.BlockSpec((1, tm, w), lambda bi, i: (bi, i, 0))
    consts = [wts["gg"], wts["w_out"], wts["n2g"], wts["w_ff1"], wts["w_ff2"]]
    return pl.pallas_call(
        _postmix_kernel,
        grid=(b, l // tm),
        in_specs=[tok(d), tok(GROUP_WIDTH), tok(GROUP_WIDTH), tok(GROUP_WIDTH),
                  pl.BlockSpec((1, 1, N_MOD * d), lambda bi, i: (bi, 0, 0))]
        + [_const_spec(c.shape) for c in consts],
        out_specs=tok(d),
        out_shape=jax.ShapeDtypeStruct((b, l, d), F32),
        compiler_params=_params("arbitrary", "arbitrary"),
        name="postmix",
    )(x, o_a, o_b, o_c, mod, *consts)


def _rot_partner(dim):
    quarter = dim // 4
    idx = np.arange(dim)
    is_a = (idx // quarter) % 2 == 0
    return np.where(is_a, idx + quarter, idx - quarter), np.where(is_a, -1.0, 1.0).astype(np.float32)


def _rope_tables(rows, dim, lead, width):
    quarter = dim // 4
    n = rows * GRID_W
    row = jnp.repeat(jnp.arange(rows), GRID_W)
    col = jnp.tile(jnp.arange(GRID_W), rows)
    inv_freq = ROPE_THETA ** (-jnp.arange(quarter, dtype=F32) / quarter)
    ang = jnp.stack([row, col], axis=-1).astype(F32)[:, :, None] * inv_freq
    ang = jnp.broadcast_to(ang[:, :, None, :], (n, 2, 2, quarter)).reshape(n, dim)
    sign = _rot_partner(dim)[1]

    def place(t, fill):
        t = jnp.concatenate([jnp.full((n, lead), fill, F32), t], axis=1)
        t = jnp.tile(t, (1, width // (lead + dim)))
        return jnp.concatenate([t, jnp.full((n, width - t.shape[1]), fill, F32)], axis=1)

    return place(jnp.cos(ang), 1.0), place(jnp.sin(ang) * sign, 0.0)


def _identity_tables(n, width):
    return jnp.ones((n, width), F32), jnp.zeros((n, width), F32)


def _pad_heads(w, heads, width, lead=0):
    shape = w.shape[:-1]
    w = w.reshape(shape + (heads, width))
    w = jnp.pad(w, [(0, 0)] * len(shape) + [(0, 0), (lead, LANES - lead - width)])
    return w.reshape(shape + (heads * LANES,))


def _split_in(w):
    sizes = (MLA_Q_RANK, MLA_KV_RANK, MLA_ROPE, LRU_WIDTH, LRU_WIDTH, SWA_QW, SWA_KW, SWA_KW)
    parts, start = [], 0
    for s in sizes:
        parts.append(w[:, start:start + s])
        start += s
    return parts


_SWA_HEAD_ORDER = np.arange(SWA_HEADS).reshape(SWA_KV_HEADS, SWA_PAIRS).T.reshape(-1)
_SWA_LANE_ORDER = (_SWA_HEAD_ORDER[:, None] * SWA_HEAD_DIM + np.arange(SWA_HEAD_DIM)[None, :]).reshape(-1)


def _layer_weights(layer, norm1_g, w_in, q_a_g, w_uq, kv_a_g, w_ukv, mla_q_g, mla_k_g, conv_w, conv_b,
                   lru_gate_w, lru_gate_b, lru_lambda, swa_q_g, swa_k_g, swa_sink, group_g, w_out,
                   norm2_g, w_ff1, w_ff2):
    pm, _ = _rot_partner(MLA_ROPE)
    ps, _ = _rot_partner(SWA_HEAD_DIM)
    cq, ckv, kr, lx, lg, sq, sk, sv = _split_in(w_in[layer])
    sq = sq[:, _SWA_LANE_ORDER]
    per_head = lambda w, heads, perm: w.reshape(w.shape[0], heads, -1)[:, :, perm].reshape(w.shape[0], -1)
    w_in_p = jnp.concatenate(
        [cq, ckv, lx, lg, sq, per_head(sq, SWA_HEADS, ps), sk, per_head(sk, SWA_KV_HEADS, ps), sv,
         _pad_heads(kr, 1, MLA_ROPE, MLA_NOPE), _pad_heads(kr[:, pm], 1, MLA_ROPE, MLA_NOPE)], axis=1)
    uq = w_uq[layer].reshape(MLA_Q_RANK, MLA_HEADS, MLA_QK)
    uq_rot = uq[:, :, MLA_NOPE:][:, :, pm].reshape(MLA_Q_RANK, MLA_HEADS * MLA_ROPE)
    w_uq_p = jnp.concatenate([_pad_heads(w_uq[layer], MLA_HEADS, MLA_QK),
                              _pad_heads(uq_rot, MLA_HEADS, MLA_ROPE, MLA_NOPE)], axis=1)
    kv = w_ukv[layer].reshape(MLA_KV_RANK, MLA_HEADS, MLA_NOPE + MLA_V)
    w_k = _pad_heads(kv[:, :, :MLA_NOPE].reshape(MLA_KV_RANK, MLA_HEADS * MLA_NOPE), MLA_HEADS, MLA_NOPE)
    w_v = kv[:, :, MLA_NOPE:].reshape(MLA_KV_RANK, MLA_VW)
    gw = lru_gate_w[layer]
    nblk = gw.shape[2]
    per_group = LANES // LRU_BLOCK_DIM
    eye = jnp.eye(per_group, dtype=F32)
    gw = gw.reshape(2, 2, nblk // per_group, per_group, LRU_BLOCK_DIM, LRU_BLOCK_DIM)
    w_gate = jnp.einsum("zgpncm,nk->pnczgkm", gw, eye).reshape(nblk // per_group, LANES, 4 * LANES)
    b_gate = lru_gate_b[layer].reshape(2, 2, nblk // per_group, LANES).transpose(2, 0, 1, 3).reshape(
        nblk // per_group, 1, 4 * LANES)
    gq = mla_q_g[layer] * MLA_QK ** -0.5
    gk = mla_k_g[layer]
    gqs = swa_q_g[layer] * SWA_HEAD_DIM ** -0.5
    gks = swa_k_g[layer]
    rope_gain = lambda g: _pad_heads(g[MLA_NOPE:][pm][None], 1, MLA_ROPE, MLA_NOPE)
    c0 = 2 * GROUP_WIDTH
    gg = jnp.concatenate([group_g[layer][:c0], group_g[layer][c0:][_SWA_LANE_ORDER]])
    w_o = jnp.concatenate([w_out[layer][:c0], w_out[layer][c0:][_SWA_LANE_ORDER]], axis=0)
    return {
        "n1g": norm1_g[layer][None], "w_in": w_in_p.astype(BF16), "qag": q_a_g[layer][None],
        "w_uq": w_uq_p.astype(BF16), "kvag": kv_a_g[layer][None],
        "w_ukv": jnp.concatenate([w_k, w_v], axis=1).astype(BF16),
        "gq": _pad_heads(gq[None], 1, MLA_QK), "gq_rot": rope_gain(gq),
        "gk": _pad_heads(gk[None], 1, MLA_QK), "gk_rot": rope_gain(gk),
        "gqs": jnp.tile(gqs, 2)[None], "gqs_rot": jnp.tile(gqs[ps], 2)[None],
        "gks": jnp.tile(gks, 2)[None], "gks_rot": jnp.tile(gks[ps], 2)[None],
        "conv_w": conv_w[layer], "conv_b": conv_b[layer][None], "w_gate": w_gate.astype(BF16), "b_gate": b_gate,
        "lam": lru_lambda[layer], "sink": swa_sink[layer],
        "gg": gg[None], "w_out": w_o.astype(BF16), "n2g": norm2_g[layer][None],
        "w_ff1": w_ff1[layer].astype(BF16), "w_ff2": w_ff2[layer].astype(BF16),
    }


def kernel(x, c, ctx, c_ctx, w_mod, b_mod, norm1_g, w_in, q_a_g, w_uq, kv_a_g, w_ukv, mla_q_g, mla_k_g, conv_w, conv_b, lru_gate_w, lru_gate_b, lru_lambda, swa_q_g, swa_k_g, swa_sink, group_g, w_out, norm2_g, w_ff1, w_ff2):
    b, l, d = x.shape
    lc = ctx.shape[1]
    depth = w_mod.shape[0]
    rows = l // GRID_W

    n_rows = -(-(b + 1) // SUBLANES) * SUBLANES
    cc = jnp.concatenate([c, c_ctx[None], jnp.zeros((n_rows - b - 1, d), F32)], axis=0)
    mod_all = _modulation(cc, w_mod, b_mod)

    rope_m = _rope_tables(rows, MLA_ROPE, MLA_NOPE, LANES)
    rope_s = _rope_tables(rows, SWA_HEAD_DIM, 0, LANES)
    no_rope = _identity_tables(lc, LANES)

    xc = ctx
    for layer in range(depth):
        last = layer == depth - 1
        wts = _layer_weights(layer, norm1_g, w_in, q_a_g, w_uq, kv_a_g, w_ukv, mla_q_g, mla_k_g, conv_w, conv_b,
                             lru_gate_w, lru_gate_b, lru_lambda, swa_q_g, swa_k_g, swa_sink, group_g, w_out,
                             norm2_g, w_ff1, w_ff2)
        mod_l = mod_all[layer, :b][:, None, :]
        mod_c = jnp.broadcast_to(mod_all[layer, b][None, None, :], (b, 1, N_MOD * d))

        qm, km, vm, lx, lg, qs, kvs = _premix(x, mod_l, wts, rope_m, rope_s)
        qm_c, km_c, vm_c, lx_c, lg_c, qs_c, kvs_c = _premix(xc, mod_c, wts, no_rope, no_rope)

        o_b, o_b_c = _rglru(lx_c, lx, lg_c, lg, wts)
        o_a = _mla_attention(qm, km_c, vm_c, km, vm)
        o_c = _swa_attention(qs, kvs_c, wts["sink"], kvs)
        x = _postmix(x, o_a, o_b, o_c, mod_l, wts)
        if not last:
            o_a_c = _mla_attention(qm_c, km_c, vm_c)
            o_c_c = _swa_attention(qs_c, kvs_c, wts["sink"])
            xc = _postmix(xc, o_a_c, o_b_c, o_c_c, mod_c, wts)
    return x
```

```python
import functools

import numpy as np
import jax
import jax.numpy as jnp
from jax import lax
from jax.experimental import pallas as pl
from jax.experimental.pallas import tpu as pltpu

F32 = jnp.float32
BF16 = jnp.bfloat16

GRID_W = 64
WINDOW = 128
ROPE_THETA = 10000.0
EPS = 1e-6
NEG_INF = -1e30
N_MOD = 6
MLA_HEADS = 8
MLA_NOPE = 64
MLA_ROPE = 32
MLA_QK = MLA_NOPE + MLA_ROPE
MLA_V = 64
MLA_Q_RANK = 256
MLA_KV_RANK = 128
LRU_WIDTH = 512
LRU_BLOCK_DIM = 64
LRU_C = 8.0
CONV_W = 4
SWA_HEADS = 8
SWA_KV_HEADS = 2
SWA_GROUP = SWA_HEADS // SWA_KV_HEADS
SWA_HEAD_DIM = 64
GROUP_WIDTH = 512

LANES = 128
SUBLANES = 8
V7X_VMEM_BYTES = 64 * 1024 * 1024
VMEM_LIMIT_BYTES = V7X_VMEM_BYTES - 8 * 1024 * 1024

MLA_QW = MLA_HEADS * LANES
MLA_VW = MLA_HEADS * MLA_V
MLA_HEADS_PER_STEP = 8
MLA_KEY_CHUNK = 512
LOG2_E = 1.4426950408889634
SWA_QW = SWA_HEADS * SWA_HEAD_DIM
SWA_KW = SWA_KV_HEADS * SWA_HEAD_DIM
SWA_KVW = 2 * SWA_KW
SWA_PAIRS = SWA_QW // LANES
SWA_Q_BLOCKS_PER_STEP = 4
FF_CHUNK = 1024

OFF_CQ = 0
OFF_CKV = OFF_CQ + MLA_Q_RANK
OFF_LX = OFF_CKV + MLA_KV_RANK
OFF_LG = OFF_LX + LRU_WIDTH
OFF_SQ = OFF_LG + LRU_WIDTH
OFF_SQR = OFF_SQ + SWA_QW
OFF_SK = OFF_SQR + SWA_QW
OFF_SKR = OFF_SK + SWA_KW
OFF_SV = OFF_SKR + SWA_KW
OFF_KR = OFF_SV + SWA_KW
OFF_KRR = OFF_KR + LANES
W_IN_PAD = OFF_KRR + LANES


def _dot(a, b):
    return jnp.dot(a, b, preferred_element_type=F32)


def _dot_nt(a, b):
    return lax.dot_general(a, b, (((1,), (1,)), ((), ())), preferred_element_type=F32)


def _rms(x):
    return x * lax.rsqrt(jnp.mean(x * x, axis=-1, keepdims=True) + EPS)


def _params(*sem, flags=None):
    return pltpu.CompilerParams(dimension_semantics=sem, vmem_limit_bytes=VMEM_LIMIT_BYTES, flags=flags)


def _const_spec(shape):
    zeros = (0,) * len(shape)
    return pl.BlockSpec(shape, lambda *_: zeros)


def _mod_kernel(c_ref, w_ref, b_ref, o_ref):
    c = c_ref[...]
    a = c * jax.nn.sigmoid(c)
    w = w_ref[0]
    a_hi = a.astype(BF16)
    a_lo = (a - a_hi.astype(F32)).astype(BF16)
    w_hi = w.astype(BF16)
    w_lo = (w - w_hi.astype(F32)).astype(BF16)
    o_ref[0] = _dot(a_hi, w_hi) + _dot(a_hi, w_lo) + _dot(a_lo, w_hi) + b_ref[0]


def _modulation(cc, w_mod, b_mod):
    depth, d, n = w_mod.shape
    rows = cc.shape[0]
    tn = n // 4
    return pl.pallas_call(
        _mod_kernel,
        grid=(depth, n // tn),
        in_specs=[
            pl.BlockSpec((rows, d), lambda l, j: (0, 0)),
            pl.BlockSpec((1, d, tn), lambda l, j: (l, 0, j)),
            pl.BlockSpec((1, 1, tn), lambda l, j: (l, 0, j)),
        ],
        out_specs=pl.BlockSpec((1, rows, tn), lambda l, j: (l, 0, j)),
        out_shape=jax.ShapeDtypeStruct((depth, rows, n), F32),
        compiler_params=_params("arbitrary", "arbitrary"),
        name="modulation",
    )(cc, w_mod, b_mod.reshape(depth, 1, n))


def _premix_kernel(x_ref, mod_ref, n1g_ref, win_ref, qag_ref, wuq_ref, kvag_ref, wuk_ref, wvt_ref,
                   gq_ref, gqr_ref, gk_ref, gkr_ref, gqs_ref, gqsr_ref, gks_ref, gksr_ref,
                   cm_ref, sm_ref, cs_ref, ss_ref,
                   qm_ref, km_ref, vt_ref, lx_ref, lg_ref, qs_ref, kvs_ref):
    d = x_ref.shape[-1]
    tm = x_ref.shape[1]
    x = x_ref[0]
    shift = mod_ref[0, :, 0:d]
    scale = mod_ref[0, :, d:2 * d]
    h = _rms(x) * n1g_ref[...] * (1.0 + scale) + shift
    p = _dot(h.astype(BF16), win_ref[...])

    lx_ref[0] = p[:, OFF_LX:OFF_LX + LRU_WIDTH]
    lg_ref[0] = p[:, OFF_LG:OFF_LG + LRU_WIDTH]

    cm, sm = cm_ref[...], sm_ref[...]

    qn = _rms(p[:, OFF_CQ:OFF_CQ + MLA_Q_RANK]) * qag_ref[...]
    qu = _dot(qn.astype(BF16), wuq_ref[...])
    ga_q, gb_q = gq_ref[...] * cm, gqr_ref[...] * sm
    for hd in range(MLA_HEADS):
        x1 = qu[:, hd * LANES:(hd + 1) * LANES]
        x2 = qu[:, MLA_QW + hd * LANES:MLA_QW + (hd + 1) * LANES]
        r = lax.rsqrt(jnp.sum(x1 * x1, axis=-1, keepdims=True) * (1.0 / MLA_QK) + EPS)
        qm_ref[0, :, hd * LANES:(hd + 1) * LANES] = (r * (x1 * ga_q + x2 * gb_q)).astype(BF16)

    kvn = (_rms(p[:, OFF_CKV:OFF_CKV + MLA_KV_RANK]) * kvag_ref[...]).astype(BF16)
    kvu = _dot(kvn, wuk_ref[...])
    vt_ref[0] = _dot_nt(wvt_ref[...], kvn).astype(BF16)
    kr = p[:, OFF_KR:OFF_KR + LANES]
    ga_k = gk_ref[...] * cm
    shared = kr * ga_k + p[:, OFF_KRR:OFF_KRR + LANES] * (gkr_ref[...] * sm)
    ss_kr = jnp.sum(kr * kr, axis=-1, keepdims=True)
    for hd in range(MLA_HEADS):
        x1 = kvu[:, hd * LANES:(hd + 1) * LANES]
        r = lax.rsqrt((jnp.sum(x1 * x1, axis=-1, keepdims=True) + ss_kr) * (1.0 / MLA_QK) + EPS)
        km_ref[0, :, hd * LANES:(hd + 1) * LANES] = (r * (x1 * ga_k + shared)).astype(BF16)

    cs, ss = cs_ref[...], ss_ref[...]
    lo = lax.broadcasted_iota(jnp.int32, (tm, LANES), 1) < SWA_HEAD_DIM

    def pair_norm_rope(x1, x2, ga, gb):
        sq = x1 * x1
        s_lo = jnp.sum(jnp.where(lo, sq, 0.0), axis=-1, keepdims=True)
        s_hi = jnp.sum(jnp.where(lo, 0.0, sq), axis=-1, keepdims=True)
        inv = 1.0 / SWA_HEAD_DIM
        r = jnp.where(lo, lax.rsqrt(s_lo * inv + EPS), lax.rsqrt(s_hi * inv + EPS))
        return (r * (x1 * ga + x2 * gb)).astype(BF16)

    ga_s, gb_s = gqs_ref[...] * cs, gqsr_ref[...] * ss
    for g in range(SWA_PAIRS):
        sl = slice(g * LANES, (g + 1) * LANES)
        qs_ref[0, :, sl] = pair_norm_rope(p[:, OFF_SQ + g * LANES:OFF_SQ + (g + 1) * LANES],
                                          p[:, OFF_SQR + g * LANES:OFF_SQR + (g + 1) * LANES], ga_s, gb_s)
    kvs_ref[0, :, 0:SWA_KW] = pair_norm_rope(p[:, OFF_SK:OFF_SK + SWA_KW], p[:, OFF_SKR:OFF_SKR + SWA_KW],
                                             gks_ref[...] * cs, gksr_ref[...] * ss)
    kvs_ref[0, :, SWA_KW:SWA_KVW] = p[:, OFF_SV:OFF_SV + SWA_KW].astype(BF16)


def _premix(x, mod, wts, rope_m, rope_s):
    b, l, d = x.shape
    tm = min(l, 512)
    consts = [wts[k] for k in ("n1g", "w_in", "qag", "w_uq", "kvag", "w_uk", "w_vt",
                               "gq", "gq_rot", "gk", "gk_rot", "gqs", "gqs_rot", "gks", "gks_rot")]
    tables = list(rope_m) + list(rope_s)
    tok = lambda w: pl.BlockSpec((1, tm, w), lambda bi, i: (bi, i, 0))
    out_w = [(MLA_QW, BF16), (MLA_QW, BF16), None, (LRU_WIDTH, F32), (LRU_WIDTH, F32),
             (SWA_QW, BF16), (SWA_KVW, BF16)]
    vt_spec = pl.BlockSpec((1, MLA_VW, tm), lambda bi, i: (bi, 0, i))
    vt_shape = jax.ShapeDtypeStruct((b, MLA_VW, l), BF16)
    return pl.pallas_call(
        _premix_kernel,
        grid=(b, l // tm),
        in_specs=[tok(d), pl.BlockSpec((1, 1, N_MOD * d), lambda bi, i: (bi, 0, 0))]
        + [_const_spec(c.shape) for c in consts]
        + [pl.BlockSpec((tm, LANES), lambda bi, i: (i, 0)) for _ in tables],
        out_specs=[tok(o[0]) if o else vt_spec for o in out_w],
        out_shape=[jax.ShapeDtypeStruct((b, l, o[0]), o[1]) if o else vt_shape for o in out_w],
        compiler_params=_params("arbitrary", "arbitrary"),
        name="premix",
    )(x, mod, *consts, *tables)


def _gelu_tanh(x):
    return 0.5 * x * (1.0 + jnp.tanh(0.7978845608028654 * (x + 0.044715 * (x * x * x))))


def _lru_kernel(lxc_ref, lxl_ref, lgc_ref, lgl_ref, cw_ref, cb_ref, wg_ref, bg_ref, lam_ref,
                obl_ref, obc_ref, af, uf, ab, ub):
    lc = lxc_ref.shape[1]
    ll = lxl_ref.shape[1]
    cw = cw_ref[...]
    cb = cb_ref[...]
    nlam = -lam_ref[...]
    softplus = jnp.maximum(nlam, 0.0) + jnp.log1p(jnp.exp(-jnp.abs(nlam)))

    def coefficients(x):
        t = x.shape[0]
        row = lax.broadcasted_iota(jnp.int32, (t, LANES), 0)
        xm2 = jnp.where(row >= 2, pltpu.roll(x, 2, 0), 0.0)
        xm1 = jnp.where(row >= 1, pltpu.roll(x, 1, 0), 0.0)
        xp1 = jnp.where(row < t - 1, pltpu.roll(x, t - 1, 0), 0.0)
        xc = cw[0:1] * xm2 + cw[1:2] * xm1 + cw[2:3] * x + cw[3:4] * xp1 + cb
        g = _dot(xc.astype(BF16), wg_ref[0]) + bg_ref[0]
        gates = 0.5 * jnp.tanh(0.5 * g) + 0.5
        out = []
        for z in range(2):
            r = gates[:, 2 * z * LANES:(2 * z + 1) * LANES]
            i = gates[:, (2 * z + 1) * LANES:(2 * z + 2) * LANES]
            log_a = (-LRU_C) * r * softplus[z:z + 1]
            a = jnp.exp(log_a)
            y = 1.0 - a * a
            root = y * lax.rsqrt(jnp.maximum(y, jnp.finfo(F32).tiny))
            out.append((a, root * (i * xc)))
        return out

    for x_ref, f_off, b_off in ((lxc_ref, 0, ll), (lxl_ref, lc, 0)):
        t = x_ref.shape[1]
        (a_f, u_f), (a_b, u_b) = coefficients(x_ref[0])
        af[f_off:f_off + t, :] = a_f
        uf[f_off:f_off + t, :] = u_f
        ab[b_off:b_off + t, :] = a_b
        ub[b_off:b_off + t, :] = u_b
    pitch = af.shape[0] // SUBLANES
    tail = af.shape[0] - (lc + ll)
    assert tail > 0
    for a_ref, u_ref in ((af, uf), (ab, ub)):
        a_ref[lc + ll:, :] = jnp.ones((tail, LANES), F32)
        u_ref[lc + ll:, :] = jnp.zeros((tail, LANES), F32)

    def streams(i):
        return pl.ds(i, SUBLANES, stride=pitch)

    def local_scan(i, state):
        h_f, p_f, h_b, p_b = state
        a = af[streams(i), :]
        h_f = a * h_f + uf[streams(i), :]
        p_f = a * p_f
        uf[streams(i), :] = h_f
        af[streams(i), :] = p_f
        j = pitch - 1 - i
        a = ab[streams(j), :]
        h_b = a * h_b + ub[streams(j), :]
        p_b = a * p_b
        ub[streams(j), :] = h_b
        ab[streams(j), :] = p_b
        return h_f, p_f, h_b, p_b

    zeros8 = jnp.zeros((SUBLANES, LANES), F32)
    ones8 = jnp.ones((SUBLANES, LANES), F32)
    h_f, p_f, h_b, p_b = lax.fori_loop(0, pitch, local_scan, (zeros8, ones8, zeros8, ones8), unroll=4)

    sub = lax.broadcasted_iota(jnp.int32, (SUBLANES, LANES), 0)
    c_f = zeros8
    c_b = zeros8
    row_f = jnp.zeros((1, LANES), F32)
    row_b = jnp.zeros((1, LANES), F32)
    for s in range(1, SUBLANES):
        row_f = h_f[s - 1:s] + p_f[s - 1:s] * row_f
        c_f = jnp.where(sub == s, row_f, c_f)
        sb = SUBLANES - 1 - s
        row_b = h_b[sb + 1:sb + 2] + p_b[sb + 1:sb + 2] * row_b
        c_b = jnp.where(sub == sb, row_b, c_b)

    def add_carry(i, _):
        uf[streams(i), :] = uf[streams(i), :] + af[streams(i), :] * c_f
        ub[streams(i), :] = ub[streams(i), :] + ab[streams(i), :] * c_b
        return 0

    lax.fori_loop(0, pitch, add_carry, 0, unroll=4)

    obl_ref[0] = (uf[lc:lc + ll, :] + ub[0:ll, :]) * _gelu_tanh(lgl_ref[0])
    obc_ref[0] = (uf[0:lc, :] + ub[ll:ll + lc, :]) * _gelu_tanh(lgc_ref[0])


def _rglru(lx_c, lx_l, lg_c, lg_l, wts):
    b, lc, w = lx_c.shape
    ll = lx_l.shape[1]
    ng = w // LANES
    pitch = -(-(lc + ll) // SUBLANES)
    pitch += (SUBLANES // 2 - pitch) % SUBLANES
    seq = lambda t: pl.BlockSpec((1, t, LANES), lambda bi, g: (bi, 0, g))
    return pl.pallas_call(
        _lru_kernel,
        grid=(b, ng),
        in_specs=[seq(lc), seq(ll), seq(lc), seq(ll),
                  pl.BlockSpec((CONV_W, LANES), lambda bi, g: (0, g)),
                  pl.BlockSpec((1, LANES), lambda bi, g: (0, g)),
                  pl.BlockSpec((1, LANES, 4 * LANES), lambda bi, g: (g, 0, 0)),
                  pl.BlockSpec((1, 1, 4 * LANES), lambda bi, g: (g, 0, 0)),
                  pl.BlockSpec((2, LANES), lambda bi, g: (0, g))],
        out_specs=[seq(ll), seq(lc)],
        out_shape=[jax.ShapeDtypeStruct((b, ll, w), F32), jax.ShapeDtypeStruct((b, lc, w), F32)],
        scratch_shapes=[pltpu.VMEM((SUBLANES * pitch, LANES), F32) for _ in range(4)],
        compiler_params=_params("arbitrary", "arbitrary"),
        name="rglru",
    )(lx_c, lx_l, lg_c, lg_l, wts["conv_w"], wts["conv_b"], wts["w_gate"], wts["b_gate"], wts["lam"])


def _mla_kernel(*refs, with_latent):
    if with_latent:
        q_ref, kc_ref, vtc_ref, kl_ref, vtl_ref, o_ref, s_buf = refs
        sources = ((kc_ref, vtc_ref), (kl_ref, vtl_ref))
    else:
        q_ref, kc_ref, vtc_ref, o_ref, s_buf = refs
        sources = ((kc_ref, vtc_ref),)
    tq = q_ref.shape[1]
    chunks, row = [], 0
    for k_ref, vt_ref in sources:
        n = k_ref.shape[1]
        for k0 in range(0, n, MLA_KEY_CHUNK):
            kn = min(MLA_KEY_CHUNK, n - k0)
            chunks.append((k_ref, vt_ref, k0, kn, row))
            row += kn

    def add(acc, x, op=jnp.add):
        return x if acc is None else op(acc, x)

    def score_chunk(hd, chunk, m8):
        k_ref, _, k0, kn, r0 = chunk
        sl = slice(hd * LANES, (hd + 1) * LANES)
        s = _dot_nt(k_ref[0, k0:k0 + kn, sl], q_ref[0, :, sl])
        s_buf[hd % 2, r0:r0 + kn, :] = s
        return add(m8, jnp.max(s.reshape(kn // SUBLANES, SUBLANES, tq), axis=0), jnp.maximum)

    def attend_chunk(hd, chunk, m, l8, o_t):
        _, vt_ref, k0, kn, r0 = chunk
        p = jnp.exp2(s_buf[hd % 2, r0:r0 + kn, :] - m)
        l8 = add(l8, jnp.sum(p.reshape(kn // SUBLANES, SUBLANES, tq), axis=0))
        o_t = add(o_t, _dot(vt_ref[0, hd * MLA_V:(hd + 1) * MLA_V, k0:k0 + kn], p.astype(BF16)))
        return l8, o_t

    outs = []
    m8 = None
    for chunk in chunks:
        m8 = score_chunk(0, chunk, m8)
    for hd in range(MLA_HEADS_PER_STEP):
        m = jnp.max(m8, axis=0, keepdims=True)
        m8, l8, o_t = None, None, None
        if hd + 1 < MLA_HEADS_PER_STEP:
            for chunk in chunks:
                m8 = score_chunk(hd + 1, chunk, m8)
        for chunk in chunks:
            l8, o_t = attend_chunk(hd, chunk, m, l8, o_t)
        outs.append(o_t * (1.0 / jnp.sum(l8, axis=0, keepdims=True)))
    o_ref[0] = jnp.concatenate(outs, axis=0).T.astype(o_ref.dtype)


def _mla_attention(q, k_c, vt_c, k_l=None, vt_l=None):
    b, lq, _ = q.shape
    lc = k_c.shape[1]
    with_latent = k_l is not None
    tq = min(lq, 512)
    qw = MLA_HEADS_PER_STEP * LANES
    vw = MLA_HEADS_PER_STEP * MLA_V
    nq = MLA_HEADS // MLA_HEADS_PER_STEP
    keys = lambda t: pl.BlockSpec((1, t, qw), lambda bi, hq, i: (bi, 0, hq))
    vals = lambda t: pl.BlockSpec((1, vw, t), lambda bi, hq, i: (bi, hq, 0))
    in_specs = [pl.BlockSpec((1, tq, qw), lambda bi, hq, i: (bi, i, hq)), keys(lc), vals(lc)]
    args = [q, k_c, vt_c]
    if with_latent:
        in_specs += [keys(k_l.shape[1]), vals(k_l.shape[1])]
        args += [k_l, vt_l]
    return pl.pallas_call(
        functools.partial(_mla_kernel, with_latent=with_latent),
        grid=(b, nq, lq // tq),
        in_specs=in_specs,
        out_specs=pl.BlockSpec((1, tq, vw), lambda bi, hq, i: (bi, i, hq)),
        out_shape=jax.ShapeDtypeStruct((b, lq, MLA_VW), BF16),
        scratch_shapes=[pltpu.VMEM((2, lc + (k_l.shape[1] if with_latent else 0), tq), F32)],
        compiler_params=_params("arbitrary", "arbitrary", "arbitrary"),
        name="mla_attention" if with_latent else "mla_attention_ctx",
    )(*args)


def _swa_kernel(*refs, with_window):
    if with_window:
        sink_ref, q_ref, kvc_ref, kvl_ref, o_ref = refs
    else:
        sink_ref, q_ref, kvc_ref, o_ref = refs
    bq = WINDOW
    rows = SWA_PAIRS * bq
    kc = kvc_ref[0, :, 0:SWA_KW]
    vc = kvc_ref[0, :, SWA_KW:SWA_KVW]
    lo = lax.broadcasted_iota(jnp.int32, (rows, LANES), 1) < SWA_HEAD_DIM
    seg = lax.broadcasted_iota(jnp.int32, (rows, 1), 0) // bq

    def block(sb, _):
        r0 = pl.multiple_of(sb * bq, bq)
        q = jnp.concatenate([q_ref[0, pl.ds(r0, bq), g * LANES:(g + 1) * LANES] for g in range(SWA_PAIRS)],
                            axis=0).astype(F32)
        if with_window:
            ll = kvl_ref.shape[1]
            span = bq + 2 * WINDOW
            q0 = pl.program_id(1) * (SWA_Q_BLOCKS_PER_STEP * bq) + r0
            w0 = pl.multiple_of(jnp.clip(q0 - WINDOW, 0, ll - span), LANES)
            kw = kvl_ref[0, pl.ds(w0, span), 0:SWA_KW]
            vw = kvl_ref[0, pl.ds(w0, span), SWA_KW:SWA_KVW]
            qi = q0 + (lax.broadcasted_iota(jnp.int32, (rows, span), 0) & (bq - 1))
            kj = w0 + lax.broadcasted_iota(jnp.int32, (rows, span), 1)
            band = jnp.abs(qi - kj) <= WINDOW
        halves = []
        for kvh in range(SWA_KV_HEADS):
            qh = (jnp.where(lo, 0.0, q) if kvh else jnp.where(lo, q, 0.0)).astype(BF16)
            sink = jnp.zeros((rows, 1), F32)
            for g in range(SWA_PAIRS):
                sink = jnp.where(seg == g, sink_ref[kvh * SWA_GROUP + g], sink)
            s_c = _dot_nt(qh, kc)
            m = jnp.maximum(jnp.max(s_c, axis=-1, keepdims=True), sink)
            if with_window:
                s_w = jnp.where(band, _dot_nt(qh, kw), NEG_INF)
                m = jnp.maximum(m, jnp.max(s_w, axis=-1, keepdims=True))
            p_c = jnp.exp(s_c - m)
            denom = jnp.sum(p_c, axis=-1, keepdims=True) + jnp.exp(sink - m)
            o = _dot(p_c.astype(BF16), vc)
            if with_window:
                p_w = jnp.exp(s_w - m)
                denom = denom + jnp.sum(p_w, axis=-1, keepdims=True)
                o = o + _dot(p_w.astype(BF16), vw)
            halves.append(o * (1.0 / denom))
        out = jnp.where(lo, halves[0], halves[1]).astype(o_ref.dtype)
        for g in range(SWA_PAIRS):
            o_ref[0, pl.ds(r0, bq), g * LANES:(g + 1) * LANES] = out[g * bq:(g + 1) * bq]
        return 0

    lax.fori_loop(0, q_ref.shape[1] // bq, block, 0)


def _swa_attention(q, kv_c, sink, kv_l=None):
    b, lq, _ = q.shape
    lc = kv_c.shape[1]
    with_window = kv_l is not None
    tq = min(lq, SWA_Q_BLOCKS_PER_STEP * WINDOW)
    in_specs = [pl.BlockSpec(memory_space=pltpu.SMEM),
                pl.BlockSpec((1, tq, SWA_QW), lambda bi, i: (bi, i, 0)),
                pl.BlockSpec((1, lc, SWA_KVW), lambda bi, i: (bi, 0, 0))]
    args = [sink, q, kv_c]
    if with_window:
        in_specs.append(pl.BlockSpec((1, kv_l.shape[1], SWA_KVW), lambda bi, i: (bi, 0, 0)))
        args.append(kv_l)
    return pl.pallas_call(
        functools.partial(_swa_kernel, with_window=with_window),
        grid=(b, lq // tq),
        in_specs=in_specs,
        out_specs=pl.BlockSpec((1, tq, SWA_QW), lambda bi, i: (bi, i, 0)),
        out_shape=jax.ShapeDtypeStruct((b, lq, SWA_QW), BF16),
        compiler_params=_params("arbitrary", "arbitrary"),
        name="swa_attention" if with_window else "swa_attention_ctx",
    )(*args)


def _postmix_kernel(x_ref, oa_ref, ob_ref, oc_ref, mod_ref, gg_ref, wout_ref, n2g_ref, w1_ref, w2_ref, o_ref):
    d = x_ref.shape[-1]
    gate1 = mod_ref[0, :, 2 * d:3 * d]
    shift2 = mod_ref[0, :, 3 * d:4 * d]
    scale2 = mod_ref[0, :, 4 * d:5 * d]
    gate2 = mod_ref[0, :, 5 * d:6 * d]
    y = None
    for gi, ref in enumerate((oa_ref, ob_ref, oc_ref)):
        sl = slice(gi * GROUP_WIDTH, (gi + 1) * GROUP_WIDTH)
        og = _rms(ref[0].astype(F32)) * gg_ref[:, sl]
        part = _dot(og.astype(BF16), wout_ref[sl, :])
        y = part if y is None else y + part
    x1 = x_ref[0] + gate1 * y
    h2 = (_rms(x1) * n2g_ref[...] * (1.0 + scale2) + shift2).astype(BF16)
    ff = None
    for j in range(w1_ref.shape[1] // FF_CHUNK):
        sl = slice(j * FF_CHUNK, (j + 1) * FF_CHUNK)
        hid = jnp.maximum(_dot(h2, w1_ref[:, sl]), 0.0)
        part = _dot((hid * hid).astype(BF16), w2_ref[sl, :])
        ff = part if ff is None else ff + part
    o_ref[0] = x1 + gate2 * ff


def _postmix(x, o_a, o_b, o_c, mod, wts):
    b, l, d = x.shape
    tm = min(l, 256)
    tok = lambda w: pl.BlockSpec((1, tm, w), lambda bi, i: (bi, i, 0))
    consts = [wts["gg"], wts["w_out"], wts["n2g"], wts["w_ff1"], wts["w_ff2"]]
    return pl.pallas_call(
        _postmix_kernel,
        grid=(b, l // tm),
        in_specs=[tok(d), tok(GROUP_WIDTH), tok(GROUP_WIDTH), tok(GROUP_WIDTH),
                  pl.BlockSpec((1, 1, N_MOD * d), lambda bi, i: (bi, 0, 0))]
        + [_const_spec(c.shape) for c in consts],
        out_specs=tok(d),
        out_shape=jax.ShapeDtypeStruct((b, l, d), F32),
        compiler_params=_params("arbitrary", "arbitrary"),
        name="postmix",
    )(x, o_a, o_b, o_c, mod, *consts)


def _rot_partner(dim):
    quarter = dim // 4
    idx = np.arange(dim)
    is_a = (idx // quarter) % 2 == 0
    return np.where(is_a, idx + quarter, idx - quarter), np.where(is_a, -1.0, 1.0).astype(np.float32)


def _rope_tables(rows, dim, lead, width):
    quarter = dim // 4
    n = rows * GRID_W
    row = jnp.repeat(jnp.arange(rows), GRID_W)
    col = jnp.tile(jnp.arange(GRID_W), rows)
    inv_freq = ROPE_THETA ** (-jnp.arange(quarter, dtype=F32) / quarter)
    ang = jnp.stack([row, col], axis=-1).astype(F32)[:, :, None] * inv_freq
    ang = jnp.broadcast_to(ang[:, :, None, :], (n, 2, 2, quarter)).reshape(n, dim)
    sign = _rot_partner(dim)[1]

    def place(t, fill):
        t = jnp.concatenate([jnp.full((n, lead), fill, F32), t], axis=1)
        t = jnp.tile(t, (1, width // (lead + dim)))
        return jnp.concatenate([t, jnp.full((n, width - t.shape[1]), fill, F32)], axis=1)

    return place(jnp.cos(ang), 1.0), place(jnp.sin(ang) * sign, 0.0)


def _identity_tables(n, width):
    return jnp.ones((n, width), F32), jnp.zeros((n, width), F32)


def _pad_heads(w, heads, width, lead=0):
    shape = w.shape[:-1]
    w = w.reshape(shape + (heads, width))
    w = jnp.pad(w, [(0, 0)] * len(shape) + [(0, 0), (lead, LANES - lead - width)])
    return w.reshape(shape + (heads * LANES,))


def _split_in(w):
    sizes = (MLA_Q_RANK, MLA_KV_RANK, MLA_ROPE, LRU_WIDTH, LRU_WIDTH, SWA_QW, SWA_KW, SWA_KW)
    parts, start = [], 0
    for s in sizes:
        parts.append(w[:, start:start + s])
        start += s
    return parts


_SWA_HEAD_ORDER = np.arange(SWA_HEADS).reshape(SWA_KV_HEADS, SWA_PAIRS).T.reshape(-1)
_SWA_LANE_ORDER = (_SWA_HEAD_ORDER[:, None] * SWA_HEAD_DIM + np.arange(SWA_HEAD_DIM)[None, :]).reshape(-1)


def _layer_weights(layer, norm1_g, w_in, q_a_g, w_uq, kv_a_g, w_ukv, mla_q_g, mla_k_g, conv_w, conv_b,
                   lru_gate_w, lru_gate_b, lru_lambda, swa_q_g, swa_k_g, swa_sink, group_g, w_out,
                   norm2_g, w_ff1, w_ff2):
    pm, _ = _rot_partner(MLA_ROPE)
    ps, _ = _rot_partner(SWA_HEAD_DIM)
    cq, ckv, kr, lx, lg, sq, sk, sv = _split_in(w_in[layer])
    sq = sq[:, _SWA_LANE_ORDER]
    per_head = lambda w, heads, perm: w.reshape(w.shape[0], heads, -1)[:, :, perm].reshape(w.shape[0], -1)
    w_in_p = jnp.concatenate(
        [cq, ckv, lx, lg, sq, per_head(sq, SWA_HEADS, ps), sk, per_head(sk, SWA_KV_HEADS, ps), sv,
         _pad_heads(kr, 1, MLA_ROPE, MLA_NOPE), _pad_heads(kr[:, pm], 1, MLA_ROPE, MLA_NOPE)], axis=1)
    uq = w_uq[layer].reshape(MLA_Q_RANK, MLA_HEADS, MLA_QK)
    uq_rot = uq[:, :, MLA_NOPE:][:, :, pm].reshape(MLA_Q_RANK, MLA_HEADS * MLA_ROPE)
    w_uq_p = jnp.concatenate([_pad_heads(w_uq[layer], MLA_HEADS, MLA_QK),
                              _pad_heads(uq_rot, MLA_HEADS, MLA_ROPE, MLA_NOPE)], axis=1)
    kv = w_ukv[layer].reshape(MLA_KV_RANK, MLA_HEADS, MLA_NOPE + MLA_V)
    w_k = _pad_heads(kv[:, :, :MLA_NOPE].reshape(MLA_KV_RANK, MLA_HEADS * MLA_NOPE), MLA_HEADS, MLA_NOPE)
    w_v = kv[:, :, MLA_NOPE:].reshape(MLA_KV_RANK, MLA_VW)
    gw = lru_gate_w[layer]
    nblk = gw.shape[2]
    per_group = LANES // LRU_BLOCK_DIM
    eye = jnp.eye(per_group, dtype=F32)
    gw = gw.reshape(2, 2, nblk // per_group, per_group, LRU_BLOCK_DIM, LRU_BLOCK_DIM)
    w_gate = jnp.einsum("zgpncm,nk->pnczgkm", gw, eye).reshape(nblk // per_group, LANES, 4 * LANES)
    b_gate = lru_gate_b[layer].reshape(2, 2, nblk // per_group, LANES).transpose(2, 0, 1, 3).reshape(
        nblk // per_group, 1, 4 * LANES)
    gq = mla_q_g[layer] * (MLA_QK ** -0.5 * LOG2_E)
    gk = mla_k_g[layer]
    gqs = swa_q_g[layer] * SWA_HEAD_DIM ** -0.5
    gks = swa_k_g[layer]
    rope_gain = lambda g: _pad_heads(g[MLA_NOPE:][pm][None], 1, MLA_ROPE, MLA_NOPE)
    c0 = 2 * GROUP_WIDTH
    gg = jnp.concatenate([group_g[layer][:c0], group_g[layer][c0:][_SWA_LANE_ORDER]])
    w_o = jnp.concatenate([w_out[layer][:c0], w_out[layer][c0:][_SWA_LANE_ORDER]], axis=0)
    return {
        "n1g": norm1_g[layer][None], "w_in": w_in_p.astype(BF16), "qag": q_a_g[layer][None],
        "w_uq": w_uq_p.astype(BF16), "kvag": kv_a_g[layer][None],
        "w_uk": w_k.astype(BF16), "w_vt": w_v.T.astype(BF16),
        "gq": _pad_heads(gq[None], 1, MLA_QK), "gq_rot": rope_gain(gq),
        "gk": _pad_heads(gk[None], 1, MLA_QK), "gk_rot": rope_gain(gk),
        "gqs": jnp.tile(gqs, 2)[None], "gqs_rot": jnp.tile(gqs[ps], 2)[None],
        "gks": jnp.tile(gks, 2)[None], "gks_rot": jnp.tile(gks[ps], 2)[None],
        "conv_w": conv_w[layer], "conv_b": conv_b[layer][None], "w_gate": w_gate.astype(BF16), "b_gate": b_gate,
        "lam": lru_lambda[layer], "sink": swa_sink[layer],
        "gg": gg[None], "w_out": w_o.astype(BF16), "n2g": norm2_g[layer][None],
        "w_ff1": w_ff1[layer].astype(BF16), "w_ff2": w_ff2[layer].astype(BF16),
    }


def kernel(x, c, ctx, c_ctx, w_mod, b_mod, norm1_g, w_in, q_a_g, w_uq, kv_a_g, w_ukv, mla_q_g, mla_k_g, conv_w, conv_b, lru_gate_w, lru_gate_b, lru_lambda, swa_q_g, swa_k_g, swa_sink, group_g, w_out, norm2_g, w_ff1, w_ff2):
    b, l, d = x.shape
    lc = ctx.shape[1]
    depth = w_mod.shape[0]
    rows = l // GRID_W

    n_rows = -(-(b + 1) // SUBLANES) * SUBLANES
    cc = jnp.concatenate([c, c_ctx[None], jnp.zeros((n_rows - b - 1, d), F32)], axis=0)
    mod_all = _modulation(cc, w_mod, b_mod)

    rope_m = _rope_tables(rows, MLA_ROPE, MLA_NOPE, LANES)
    rope_s = _rope_tables(rows, SWA_HEAD_DIM, 0, LANES)
    no_rope = _identity_tables(lc, LANES)

    xc = ctx
    for layer in range(depth):
        last = layer == depth - 1
        wts = _layer_weights(layer, norm1_g, w_in, q_a_g, w_uq, kv_a_g, w_ukv, mla_q_g, mla_k_g, conv_w, conv_b,
                             lru_gate_w, lru_gate_b, lru_lambda, swa_q_g, swa_k_g, swa_sink, group_g, w_out,
                             norm2_g, w_ff1, w_ff2)
        mod_l = mod_all[layer, :b][:, None, :]
        mod_c = jnp.broadcast_to(mod_all[layer, b][None, None, :], (b, 1, N_MOD * d))

        qm, km, vt, lx, lg, qs, kvs = _premix(x, mod_l, wts, rope_m, rope_s)
        qm_c, km_c, vt_c, lx_c, lg_c, qs_c, kvs_c = _premix(xc, mod_c, wts, no_rope, no_rope)

        o_b, o_b_c = _rglru(lx_c, lx, lg_c, lg, wts)
        o_a = _mla_attention(qm, km_c, vt_c, km, vt)
        o_c = _swa_attention(qs, kvs_c, wts["sink"], kvs)
        x = _postmix(x, o_a, o_b, o_c, mod_l, wts)
        if not last:
            o_a_c = _mla_attention(qm_c, km_c, vt_c)
            o_c_c = _swa_attention(qs_c, kvs_c, wts["sink"])
            xc = _postmix(xc, o_a_c, o_b_c, o_c_c, mod_c, wts)
    return x
```

```python
import functools

import numpy as np
import jax
import jax.numpy as jnp
from jax import lax
from jax.experimental import pallas as pl
from jax.experimental.pallas import tpu as pltpu

F32 = jnp.float32
BF16 = jnp.bfloat16

GRID_W = 64
WINDOW = 128
ROPE_THETA = 10000.0
EPS = 1e-6
NEG_INF = -1e30
N_MOD = 6
MLA_HEADS = 8
MLA_NOPE = 64
MLA_ROPE = 32
MLA_QK = MLA_NOPE + MLA_ROPE
MLA_V = 64
MLA_Q_RANK = 256
MLA_KV_RANK = 128
LRU_WIDTH = 512
LRU_BLOCK_DIM = 64
LRU_C = 8.0
CONV_W = 4
SWA_HEADS = 8
SWA_KV_HEADS = 2
SWA_GROUP = SWA_HEADS // SWA_KV_HEADS
SWA_HEAD_DIM = 64
GROUP_WIDTH = 512

LANES = 128
SUBLANES = 8
V7X_VMEM_BYTES = 64 * 1024 * 1024
VMEM_LIMIT_BYTES = V7X_VMEM_BYTES - 8 * 1024 * 1024

MLA_QW = MLA_HEADS * LANES
MLA_VW = MLA_HEADS * MLA_V
MLA_HEADS_PER_STEP = 8
MLA_KEY_CHUNK = 512
LOG2_E = 1.4426950408889634
SWA_QW = SWA_HEADS * SWA_HEAD_DIM
SWA_KW = SWA_KV_HEADS * SWA_HEAD_DIM
SWA_PAIRS = SWA_QW // LANES
SWA_Q_BLOCKS_PER_STEP = 4
FF_CHUNK = 1024

OFF_CQ = 0
OFF_CKV = OFF_CQ + MLA_Q_RANK
OFF_KR = OFF_CKV + MLA_KV_RANK
SECTION_2 = OFF_KR + LANES
OFF_KRR = SECTION_2
OFF_SQ = OFF_KRR + LANES
OFF_SQR = OFF_SQ + SWA_QW
OFF_SK = OFF_SQR + SWA_QW
OFF_SKR = OFF_SK + SWA_KW
OFF_SV = OFF_SKR + SWA_KW
OFF_LX = OFF_SV + SWA_KW
OFF_LG = OFF_LX + LRU_WIDTH
W_IN_PAD = OFF_LG + LRU_WIDTH


def _dot(a, b):
    return jnp.dot(a, b, preferred_element_type=F32)


def _dot_nt(a, b):
    return lax.dot_general(a, b, (((1,), (1,)), ((), ())), preferred_element_type=F32)


def _rms(x):
    return x * lax.rsqrt(jnp.mean(x * x, axis=-1, keepdims=True) + EPS)


def _params(*sem, flags=None):
    return pltpu.CompilerParams(dimension_semantics=sem, vmem_limit_bytes=VMEM_LIMIT_BYTES, flags=flags)


def _const_spec(shape):
    zeros = (0,) * len(shape)
    return pl.BlockSpec(shape, lambda *_: zeros)


def _mod_kernel(c_ref, w_ref, b_ref, o_ref):
    c = c_ref[...]
    a = c * jax.nn.sigmoid(c)
    w = w_ref[0]
    a_hi = a.astype(BF16)
    a_lo = (a - a_hi.astype(F32)).astype(BF16)
    w_hi = w.astype(BF16)
    w_lo = (w - w_hi.astype(F32)).astype(BF16)
    o_ref[0] = _dot(a_hi, w_hi) + _dot(a_hi, w_lo) + _dot(a_lo, w_hi) + b_ref[0]


def _modulation(cc, w_mod, b_mod):
    depth, d, n = w_mod.shape
    rows = cc.shape[0]
    tn = n // 4
    return pl.pallas_call(
        _mod_kernel,
        grid=(depth, n // tn),
        in_specs=[
            pl.BlockSpec((rows, d), lambda l, j: (0, 0)),
            pl.BlockSpec((1, d, tn), lambda l, j: (l, 0, j)),
            pl.BlockSpec((1, 1, tn), lambda l, j: (l, 0, j)),
        ],
        out_specs=pl.BlockSpec((1, rows, tn), lambda l, j: (l, 0, j)),
        out_shape=jax.ShapeDtypeStruct((depth, rows, n), F32),
        compiler_params=_params("arbitrary", "arbitrary"),
        name="modulation",
    )(cc, w_mod, b_mod.reshape(depth, 1, n))


def _premix_kernel(x_ref, mod_ref, n1g_ref, win_ref, qag_ref, wuq_ref, kvag_ref, wuk_ref, wvt_ref,
                   gq_ref, gqr_ref, gk_ref, gkr_ref, gqs_ref, gqsr_ref, gks_ref, gksr_ref,
                   cm_ref, sm_ref, cs_ref, ss_ref,
                   qm_ref, km_ref, vt_ref, lx_ref, lg_ref, qs_ref, ks_ref, vts_ref):
    d = x_ref.shape[-1]
    tm = x_ref.shape[1]
    x = x_ref[0]
    shift = mod_ref[0, :, 0:d]
    scale = mod_ref[0, :, d:2 * d]
    h = (_rms(x) * n1g_ref[...] * (1.0 + scale) + shift).astype(BF16)

    cm, sm = cm_ref[...], sm_ref[...]
    cs, ss = cs_ref[...], ss_ref[...]

    p1 = _dot(h, win_ref[:, 0:SECTION_2])
    qn = (_rms(p1[:, OFF_CQ:OFF_CQ + MLA_Q_RANK]) * qag_ref[...]).astype(BF16)
    kvn = (_rms(p1[:, OFF_CKV:OFF_CKV + MLA_KV_RANK]) * kvag_ref[...]).astype(BF16)
    kr = p1[:, OFF_KR:OFF_KR + LANES]
    ss_kr = jnp.sum(kr * kr, axis=-1, keepdims=True)

    p2 = _dot(h, win_ref[:, SECTION_2:OFF_LX])
    qu = _dot(qn, wuq_ref[...])
    kvu = _dot(kvn, wuk_ref[...])
    vt_ref[0] = _dot_nt(wvt_ref[...], kvn).astype(BF16)
    ga_k = gk_ref[...] * cm
    shared = kr * ga_k + p2[:, 0:LANES] * (gkr_ref[...] * sm)
    lo = lax.broadcasted_iota(jnp.int32, (tm, LANES), 1) < SWA_HEAD_DIM

    def pair_norm_rope(off, off_rot, ga, gb):
        x1 = p2[:, off - SECTION_2:off - SECTION_2 + LANES]
        x2 = p2[:, off_rot - SECTION_2:off_rot - SECTION_2 + LANES]
        sq = x1 * x1
        s_lo = jnp.sum(jnp.where(lo, sq, 0.0), axis=-1, keepdims=True)
        s_hi = jnp.sum(jnp.where(lo, 0.0, sq), axis=-1, keepdims=True)
        inv = 1.0 / SWA_HEAD_DIM
        r = jnp.where(lo, lax.rsqrt(s_lo * inv + EPS), lax.rsqrt(s_hi * inv + EPS))
        return (r * (x1 * ga + x2 * gb)).astype(BF16)

    ga_s, gb_s = gqs_ref[...] * cs, gqsr_ref[...] * ss
    for g in range(SWA_PAIRS):
        qs_ref[0, :, g * LANES:(g + 1) * LANES] = pair_norm_rope(OFF_SQ + g * LANES, OFF_SQR + g * LANES, ga_s, gb_s)
    ks_ref[0] = pair_norm_rope(OFF_SK, OFF_SKR, gks_ref[...] * cs, gksr_ref[...] * ss)
    vts_ref[0] = p2[:, OFF_SV - SECTION_2:OFF_SV - SECTION_2 + SWA_KW].T.astype(BF16)

    p3 = _dot(h, win_ref[:, OFF_LX:W_IN_PAD])
    ga_q, gb_q = gq_ref[...] * cm, gqr_ref[...] * sm
    for hd in range(MLA_HEADS):
        x1 = qu[:, hd * LANES:(hd + 1) * LANES]
        x2 = qu[:, MLA_QW + hd * LANES:MLA_QW + (hd + 1) * LANES]
        r = lax.rsqrt(jnp.sum(x1 * x1, axis=-1, keepdims=True) * (1.0 / MLA_QK) + EPS)
        qm_ref[0, :, hd * LANES:(hd + 1) * LANES] = (r * (x1 * ga_q + x2 * gb_q)).astype(BF16)
    for hd in range(MLA_HEADS):
        x1 = kvu[:, hd * LANES:(hd + 1) * LANES]
        r = lax.rsqrt((jnp.sum(x1 * x1, axis=-1, keepdims=True) + ss_kr) * (1.0 / MLA_QK) + EPS)
        km_ref[0, :, hd * LANES:(hd + 1) * LANES] = (r * (x1 * ga_k + shared)).astype(BF16)
    lx_ref[0] = p3[:, 0:LRU_WIDTH]
    lg_ref[0] = p3[:, LRU_WIDTH:2 * LRU_WIDTH]


def _premix(x, mod, wts, rope_m, rope_s):
    b, l, d = x.shape
    tm = min(l, 512)
    consts = [wts[k] for k in ("n1g", "w_in", "qag", "w_uq", "kvag", "w_uk", "w_vt",
                               "gq", "gq_rot", "gk", "gk_rot", "gqs", "gqs_rot", "gks", "gks_rot")]
    tables = list(rope_m) + list(rope_s)
    tok = lambda w, dt: (pl.BlockSpec((1, tm, w), lambda bi, i: (bi, i, 0)), jax.ShapeDtypeStruct((b, l, w), dt))
    tr = lambda w: (pl.BlockSpec((1, w, tm), lambda bi, i: (bi, 0, i)), jax.ShapeDtypeStruct((b, w, l), BF16))
    outs = [tok(MLA_QW, BF16), tok(MLA_QW, BF16), tr(MLA_VW), tok(LRU_WIDTH, F32), tok(LRU_WIDTH, F32),
            tok(SWA_QW, BF16), tok(SWA_KW, BF16), tr(SWA_KW)]
    return pl.pallas_call(
        _premix_kernel,
        grid=(b, l // tm),
        in_specs=[tok(d, F32)[0], pl.BlockSpec((1, 1, N_MOD * d), lambda bi, i: (bi, 0, 0))]
        + [_const_spec(c.shape) for c in consts]
        + [pl.BlockSpec((tm, LANES), lambda bi, i: (i, 0)) for _ in tables],
        out_specs=[o[0] for o in outs],
        out_shape=[o[1] for o in outs],
        compiler_params=_params("arbitrary", "arbitrary"),
        name="premix",
    )(x, mod, *consts, *tables)


def _gelu_tanh(x):
    return 0.5 * x * (1.0 + jnp.tanh(0.7978845608028654 * (x + 0.044715 * (x * x * x))))


def _lru_kernel(lxc_ref, lxl_ref, lgc_ref, lgl_ref, cw_ref, cb_ref, wg_ref, bg_ref, lam_ref,
                obl_ref, obc_ref, af, uf, ab, ub):
    lc = lxc_ref.shape[1]
    ll = lxl_ref.shape[1]
    cw = cw_ref[...]
    cb = cb_ref[...]
    nlam = -lam_ref[...]
    softplus = jnp.maximum(nlam, 0.0) + jnp.log1p(jnp.exp(-jnp.abs(nlam)))

    def coefficients(x):
        t = x.shape[0]
        row = lax.broadcasted_iota(jnp.int32, (t, LANES), 0)
        xm2 = jnp.where(row >= 2, pltpu.roll(x, 2, 0), 0.0)
        xm1 = jnp.where(row >= 1, pltpu.roll(x, 1, 0), 0.0)
        xp1 = jnp.where(row < t - 1, pltpu.roll(x, t - 1, 0), 0.0)
        xc = cw[0:1] * xm2 + cw[1:2] * xm1 + cw[2:3] * x + cw[3:4] * xp1 + cb
        g = _dot(xc.astype(BF16), wg_ref[0]) + bg_ref[0]
        gates = 0.5 * jnp.tanh(0.5 * g) + 0.5
        out = []
        for z in range(2):
            r = gates[:, 2 * z * LANES:(2 * z + 1) * LANES]
            i = gates[:, (2 * z + 1) * LANES:(2 * z + 2) * LANES]
            log_a = (-LRU_C) * r * softplus[z:z + 1]
            a = jnp.exp(log_a)
            y = 1.0 - a * a
            root = y * lax.rsqrt(jnp.maximum(y, jnp.finfo(F32).tiny))
            out.append((a, root * (i * xc)))
        return out

    for x_ref, f_off, b_off in ((lxc_ref, 0, ll), (lxl_ref, lc, 0)):
        t = x_ref.shape[1]
        (a_f, u_f), (a_b, u_b) = coefficients(x_ref[0])
        af[f_off:f_off + t, :] = a_f
        uf[f_off:f_off + t, :] = u_f
        ab[b_off:b_off + t, :] = a_b
        ub[b_off:b_off + t, :] = u_b
    pitch = af.shape[0] // SUBLANES
    tail = af.shape[0] - (lc + ll)
    assert tail > 0
    for a_ref, u_ref in ((af, uf), (ab, ub)):
        a_ref[lc + ll:, :] = jnp.ones((tail, LANES), F32)
        u_ref[lc + ll:, :] = jnp.zeros((tail, LANES), F32)

    def streams(i):
        return pl.ds(i, SUBLANES, stride=pitch)

    def local_scan(i, state):
        h_f, p_f, h_b, p_b = state
        a = af[streams(i), :]
        h_f = a * h_f + uf[streams(i), :]
        p_f = a * p_f
        uf[streams(i), :] = h_f
        af[streams(i), :] = p_f
        j = pitch - 1 - i
        a = ab[streams(j), :]
        h_b = a * h_b + ub[streams(j), :]
        p_b = a * p_b
        ub[streams(j), :] = h_b
        ab[streams(j), :] = p_b
        return h_f, p_f, h_b, p_b

    zeros8 = jnp.zeros((SUBLANES, LANES), F32)
    ones8 = jnp.ones((SUBLANES, LANES), F32)
    h_f, p_f, h_b, p_b = lax.fori_loop(0, pitch, local_scan, (zeros8, ones8, zeros8, ones8), unroll=4)

    sub = lax.broadcasted_iota(jnp.int32, (SUBLANES, LANES), 0)
    c_f = zeros8
    c_b = zeros8
    row_f = jnp.zeros((1, LANES), F32)
    row_b = jnp.zeros((1, LANES), F32)
    for s in range(1, SUBLANES):
        row_f = h_f[s - 1:s] + p_f[s - 1:s] * row_f
        c_f = jnp.where(sub == s, row_f, c_f)
        sb = SUBLANES - 1 - s
        row_b = h_b[sb + 1:sb + 2] + p_b[sb + 1:sb + 2] * row_b
        c_b = jnp.where(sub == sb, row_b, c_b)

    def add_carry(i, _):
        uf[streams(i), :] = uf[streams(i), :] + af[streams(i), :] * c_f
        ub[streams(i), :] = ub[streams(i), :] + ab[streams(i), :] * c_b
        return 0

    lax.fori_loop(0, pitch, add_carry, 0, unroll=4)

    obl_ref[0] = (uf[lc:lc + ll, :] + ub[0:ll, :]) * _gelu_tanh(lgl_ref[0])
    obc_ref[0] = (uf[0:lc, :] + ub[ll:ll + lc, :]) * _gelu_tanh(lgc_ref[0])


def _rglru(lx_c, lx_l, lg_c, lg_l, wts):
    b, lc, w = lx_c.shape
    ll = lx_l.shape[1]
    ng = w // LANES
    pitch = -(-(lc + ll) // SUBLANES)
    pitch += (SUBLANES // 2 - pitch) % SUBLANES
    seq = lambda t: pl.BlockSpec((1, t, LANES), lambda bi, g: (bi, 0, g))
    return pl.pallas_call(
        _lru_kernel,
        grid=(b, ng),
        in_specs=[seq(lc), seq(ll), seq(lc), seq(ll),
                  pl.BlockSpec((CONV_W, LANES), lambda bi, g: (0, g)),
                  pl.BlockSpec((1, LANES), lambda bi, g: (0, g)),
                  pl.BlockSpec((1, LANES, 4 * LANES), lambda bi, g: (g, 0, 0)),
                  pl.BlockSpec((1, 1, 4 * LANES), lambda bi, g: (g, 0, 0)),
                  pl.BlockSpec((2, LANES), lambda bi, g: (0, g))],
        out_specs=[seq(ll), seq(lc)],
        out_shape=[jax.ShapeDtypeStruct((b, ll, w), F32), jax.ShapeDtypeStruct((b, lc, w), F32)],
        scratch_shapes=[pltpu.VMEM((SUBLANES * pitch, LANES), F32) for _ in range(4)],
        compiler_params=_params("arbitrary", "arbitrary"),
        name="rglru",
    )(lx_c, lx_l, lg_c, lg_l, wts["conv_w"], wts["conv_b"], wts["w_gate"], wts["b_gate"], wts["lam"])


def _mla_kernel(*refs, with_latent):
    if with_latent:
        q_ref, kc_ref, vtc_ref, kl_ref, vtl_ref, o_ref, s_buf = refs
        sources = ((kc_ref, vtc_ref), (kl_ref, vtl_ref))
    else:
        q_ref, kc_ref, vtc_ref, o_ref, s_buf = refs
        sources = ((kc_ref, vtc_ref),)
    tq = q_ref.shape[1]
    chunks, row = [], 0
    for k_ref, vt_ref in sources:
        n = k_ref.shape[1]
        for k0 in range(0, n, MLA_KEY_CHUNK):
            kn = min(MLA_KEY_CHUNK, n - k0)
            chunks.append((k_ref, vt_ref, k0, kn, row))
            row += kn

    def add(acc, x, op=jnp.add):
        return x if acc is None else op(acc, x)

    def score_chunk(hd, chunk, m8):
        k_ref, _, k0, kn, r0 = chunk
        sl = slice(hd * LANES, (hd + 1) * LANES)
        s = _dot_nt(k_ref[0, k0:k0 + kn, sl], q_ref[0, :, sl])
        s_buf[hd % 2, r0:r0 + kn, :] = s
        return add(m8, jnp.max(s.reshape(kn // SUBLANES, SUBLANES, tq), axis=0), jnp.maximum)

    def attend_chunk(hd, chunk, m, l8, o_t):
        _, vt_ref, k0, kn, r0 = chunk
        p = jnp.exp2(s_buf[hd % 2, r0:r0 + kn, :] - m)
        l8 = add(l8, jnp.sum(p.reshape(kn // SUBLANES, SUBLANES, tq), axis=0))
        o_t = add(o_t, _dot(vt_ref[0, hd * MLA_V:(hd + 1) * MLA_V, k0:k0 + kn], p.astype(BF16)))
        return l8, o_t

    outs = []
    m8 = None
    for chunk in chunks:
        m8 = score_chunk(0, chunk, m8)
    for hd in range(MLA_HEADS_PER_STEP):
        m = jnp.max(m8, axis=0, keepdims=True)
        m8, l8, o_t = None, None, None
        if hd + 1 < MLA_HEADS_PER_STEP:
            for chunk in chunks:
                m8 = score_chunk(hd + 1, chunk, m8)
        for chunk in chunks:
            l8, o_t = attend_chunk(hd, chunk, m, l8, o_t)
        outs.append(o_t * (1.0 / jnp.sum(l8, axis=0, keepdims=True)))
    o_ref[0] = jnp.concatenate(outs, axis=0).T.astype(o_ref.dtype)


def _mla_attention(q, k_c, vt_c, k_l=None, vt_l=None):
    b, lq, _ = q.shape
    lc = k_c.shape[1]
    with_latent = k_l is not None
    tq = min(lq, 512)
    qw = MLA_HEADS_PER_STEP * LANES
    vw = MLA_HEADS_PER_STEP * MLA_V
    nq = MLA_HEADS // MLA_HEADS_PER_STEP
    keys = lambda t: pl.BlockSpec((1, t, qw), lambda bi, hq, i: (bi, 0, hq))
    vals = lambda t: pl.BlockSpec((1, vw, t), lambda bi, hq, i: (bi, hq, 0))
    in_specs = [pl.BlockSpec((1, tq, qw), lambda bi, hq, i: (bi, i, hq)), keys(lc), vals(lc)]
    args = [q, k_c, vt_c]
    if with_latent:
        in_specs += [keys(k_l.shape[1]), vals(k_l.shape[1])]
        args += [k_l, vt_l]
    return pl.pallas_call(
        functools.partial(_mla_kernel, with_latent=with_latent),
        grid=(b, nq, lq // tq),
        in_specs=in_specs,
        out_specs=pl.BlockSpec((1, tq, vw), lambda bi, hq, i: (bi, i, hq)),
        out_shape=jax.ShapeDtypeStruct((b, lq, MLA_VW), BF16),
        scratch_shapes=[pltpu.VMEM((2, lc + (k_l.shape[1] if with_latent else 0), tq), F32)],
        compiler_params=_params("arbitrary", "arbitrary", "arbitrary"),
        name="mla_attention" if with_latent else "mla_attention_ctx",
    )(*args)


def _swa_kernel(*refs, with_window):
    if with_window:
        sink_ref, q_ref, kc_ref, vtc_ref, kl_ref, vtl_ref, o_ref = refs
    else:
        sink_ref, q_ref, kc_ref, vtc_ref, o_ref = refs
    bq = WINDOW
    cols = SWA_PAIRS * bq
    kc = kc_ref[0]
    lo = lax.broadcasted_iota(jnp.int32, (cols, LANES), 1) < SWA_HEAD_DIM
    seg = lax.broadcasted_iota(jnp.int32, (1, cols), 1) // bq

    span = bq + 2 * WINDOW
    sinks = []
    for kvh in range(SWA_KV_HEADS):
        sink = jnp.zeros((1, cols), F32)
        for g in range(SWA_PAIRS):
            sink = jnp.where(seg == g, sink_ref[kvh * SWA_GROUP + g] * LOG2_E, sink)
        sinks.append(sink)

    def scores(sb):
        r0 = sb * bq
        q = jnp.concatenate([q_ref[0, r0:r0 + bq, g * LANES:(g + 1) * LANES] for g in range(SWA_PAIRS)],
                            axis=0).astype(F32)
        w0 = None
        if with_window:
            q0 = pl.program_id(1) * q_ref.shape[1] + r0
            w0 = pl.multiple_of(jnp.clip(q0 - WINDOW, 0, kl_ref.shape[1] - span), LANES)
            kw = kl_ref[0, pl.ds(w0, span), :]
            kj = w0 + lax.broadcasted_iota(jnp.int32, (span, cols), 0)
            qi = q0 + (lax.broadcasted_iota(jnp.int32, (span, cols), 1) & (bq - 1))
            band = jnp.abs(qi - kj) <= WINDOW
        out = []
        for kvh in range(SWA_KV_HEADS):
            qh = (jnp.where(lo, 0.0, q) if kvh else jnp.where(lo, q, 0.0)).astype(BF16)
            s = [_dot_nt(kc, qh)]
            if with_window:
                s.append(jnp.where(band, _dot_nt(kw, qh), NEG_INF))
            out.append(s)
        return w0, out

    def attend(sb, w0, blk_scores):
        o_t = []
        for kvh, s in enumerate(blk_scores):
            sink = sinks[kvh]
            m = functools.reduce(jnp.maximum, [jnp.max(x, axis=0, keepdims=True) for x in s] + [sink])
            p = [jnp.exp2(x - m) for x in s]
            denom = sum(jnp.sum(x, axis=0, keepdims=True) for x in p) + jnp.exp2(sink - m)
            vs = slice(kvh * SWA_HEAD_DIM, (kvh + 1) * SWA_HEAD_DIM)
            o = _dot(vtc_ref[0, vs, :], p[0].astype(BF16))
            if with_window:
                o = o + _dot(vtl_ref[0, vs, pl.ds(w0, span)], p[1].astype(BF16))
            o_t.append(o * (1.0 / denom))
        out = jnp.concatenate(o_t, axis=0).T.astype(o_ref.dtype)
        for g in range(SWA_PAIRS):
            o_ref[0, sb * bq:(sb + 1) * bq, g * LANES:(g + 1) * LANES] = out[g * bq:(g + 1) * bq]

    n_blocks = q_ref.shape[1] // bq
    nxt = scores(0)
    for sb in range(n_blocks):
        cur = nxt
        if sb + 1 < n_blocks:
            nxt = scores(sb + 1)
        attend(sb, *cur)


def _swa_attention(q, k_c, vt_c, sink, k_l=None, vt_l=None):
    b, lq, _ = q.shape
    lc = k_c.shape[1]
    with_window = k_l is not None
    tq = min(lq, SWA_Q_BLOCKS_PER_STEP * WINDOW)
    keys = lambda t: pl.BlockSpec((1, t, SWA_KW), lambda bi, i: (bi, 0, 0))
    vals = lambda t: pl.BlockSpec((1, SWA_KW, t), lambda bi, i: (bi, 0, 0))
    in_specs = [pl.BlockSpec(memory_space=pltpu.SMEM),
                pl.BlockSpec((1, tq, SWA_QW), lambda bi, i: (bi, i, 0)), keys(lc), vals(lc)]
    args = [sink, q, k_c, vt_c]
    if with_window:
        in_specs += [keys(k_l.shape[1]), vals(k_l.shape[1])]
        args += [k_l, vt_l]
    return pl.pallas_call(
        functools.partial(_swa_kernel, with_window=with_window),
        grid=(b, lq // tq),
        in_specs=in_specs,
        out_specs=pl.BlockSpec((1, tq, SWA_QW), lambda bi, i: (bi, i, 0)),
        out_shape=jax.ShapeDtypeStruct((b, lq, SWA_QW), BF16),
        compiler_params=_params("arbitrary", "arbitrary"),
        name="swa_attention" if with_window else "swa_attention_ctx",
    )(*args)


def _postmix_kernel(x_ref, oa_ref, ob_ref, oc_ref, mod_ref, gg_ref, wout_ref, n2g_ref, w1_ref, w2_ref, o_ref):
    d = x_ref.shape[-1]
    gate1 = mod_ref[0, :, 2 * d:3 * d]
    shift2 = mod_ref[0, :, 3 * d:4 * d]
    scale2 = mod_ref[0, :, 4 * d:5 * d]
    gate2 = mod_ref[0, :, 5 * d:6 * d]
    y = None
    for gi, ref in enumerate((oa_ref, ob_ref, oc_ref)):
        sl = slice(gi * GROUP_WIDTH, (gi + 1) * GROUP_WIDTH)
        og = _rms(ref[0].astype(F32)) * gg_ref[:, sl]
        part = _dot(og.astype(BF16), wout_ref[sl, :])
        y = part if y is None else y + part
    x1 = x_ref[0] + gate1 * y
    h2 = (_rms(x1) * n2g_ref[...] * (1.0 + scale2) + shift2).astype(BF16)
    ff = None
    for j in range(w1_ref.shape[1] // FF_CHUNK):
        sl = slice(j * FF_CHUNK, (j + 1) * FF_CHUNK)
        hid = jnp.maximum(_dot(h2, w1_ref[:, sl]), 0.0)
        part = _dot((hid * hid).astype(BF16), w2_ref[sl, :])
        ff = part if ff is None else ff + part
    o_ref[0] = x1 + gate2 * ff


def _postmix(x, o_a, o_b, o_c, mod, wts):
    b, l, d = x.shape
    tm = min(l, 256)
    tok = lambda w: pl.BlockSpec((1, tm, w), lambda bi, i: (bi, i, 0))
    consts = [wts["gg"], wts["w_out"], wts["n2g"], wts["w_ff1"], wts["w_ff2"]]
    return pl.pallas_call(
        _postmix_kernel,
        grid=(b, l // tm),
        in_specs=[tok(d), tok(GROUP_WIDTH), tok(GROUP_WIDTH), tok(GROUP_WIDTH),
                  pl.BlockSpec((1, 1, N_MOD * d), lambda bi, i: (bi, 0, 0))]
        + [_const_spec(c.shape) for c in consts],
        out_specs=tok(d),
        out_shape=jax.ShapeDtypeStruct((b, l, d), F32),
        compiler_params=_params("arbitrary", "arbitrary"),
        name="postmix",
    )(x, o_a, o_b, o_c, mod, *consts)


def _rot_partner(dim):
    quarter = dim // 4
    idx = np.arange(dim)
    is_a = (idx // quarter) % 2 == 0
    return np.where(is_a, idx + quarter, idx - quarter), np.where(is_a, -1.0, 1.0).astype(np.float32)


def _rope_tables(rows, dim, lead, width):
    quarter = dim // 4
    n = rows * GRID_W
    row = jnp.repeat(jnp.arange(rows), GRID_W)
    col = jnp.tile(jnp.arange(GRID_W), rows)
    inv_freq = ROPE_THETA ** (-jnp.arange(quarter, dtype=F32) / quarter)
    ang = jnp.stack([row, col], axis=-1).astype(F32)[:, :, None] * inv_freq
    ang = jnp.broadcast_to(ang[:, :, None, :], (n, 2, 2, quarter)).reshape(n, dim)
    sign = _rot_partner(dim)[1]

    def place(t, fill):
        t = jnp.concatenate([jnp.full((n, lead), fill, F32), t], axis=1)
        t = jnp.tile(t, (1, width // (lead + dim)))
        return jnp.concatenate([t, jnp.full((n, width - t.shape[1]), fill, F32)], axis=1)

    return place(jnp.cos(ang), 1.0), place(jnp.sin(ang) * sign, 0.0)


def _identity_tables(n, width):
    return jnp.ones((n, width), F32), jnp.zeros((n, width), F32)


def _pad_heads(w, heads, width, lead=0):
    shape = w.shape[:-1]
    w = w.reshape(shape + (heads, width))
    w = jnp.pad(w, [(0, 0)] * len(shape) + [(0, 0), (lead, LANES - lead - width)])
    return w.reshape(shape + (heads * LANES,))


def _split_in(w):
    sizes = (MLA_Q_RANK, MLA_KV_RANK, MLA_ROPE, LRU_WIDTH, LRU_WIDTH, SWA_QW, SWA_KW, SWA_KW)
    parts, start = [], 0
    for s in sizes:
        parts.append(w[:, start:start + s])
        start += s
    return parts


_SWA_HEAD_ORDER = np.arange(SWA_HEADS).reshape(SWA_KV_HEADS, SWA_PAIRS).T.reshape(-1)
_SWA_LANE_ORDER = (_SWA_HEAD_ORDER[:, None] * SWA_HEAD_DIM + np.arange(SWA_HEAD_DIM)[None, :]).reshape(-1)


def _layer_weights(layer, norm1_g, w_in, q_a_g, w_uq, kv_a_g, w_ukv, mla_q_g, mla_k_g, conv_w, conv_b,
                   lru_gate_w, lru_gate_b, lru_lambda, swa_q_g, swa_k_g, swa_sink, group_g, w_out,
                   norm2_g, w_ff1, w_ff2):
    pm, _ = _rot_partner(MLA_ROPE)
    ps, _ = _rot_partner(SWA_HEAD_DIM)
    cq, ckv, kr, lx, lg, sq, sk, sv = _split_in(w_in[layer])
    sq = sq[:, _SWA_LANE_ORDER]
    per_head = lambda w, heads, perm: w.reshape(w.shape[0], heads, -1)[:, :, perm].reshape(w.shape[0], -1)
    w_in_p = jnp.concatenate(
        [cq, ckv, _pad_heads(kr, 1, MLA_ROPE, MLA_NOPE), _pad_heads(kr[:, pm], 1, MLA_ROPE, MLA_NOPE),
         sq, per_head(sq, SWA_HEADS, ps), sk, per_head(sk, SWA_KV_HEADS, ps), sv, lx, lg], axis=1)
    uq = w_uq[layer].reshape(MLA_Q_RANK, MLA_HEADS, MLA_QK)
    uq_rot = uq[:, :, MLA_NOPE:][:, :, pm].reshape(MLA_Q_RANK, MLA_HEADS * MLA_ROPE)
    w_uq_p = jnp.concatenate([_pad_heads(w_uq[layer], MLA_HEADS, MLA_QK),
                              _pad_heads(uq_rot, MLA_HEADS, MLA_ROPE, MLA_NOPE)], axis=1)
    kv = w_ukv[layer].reshape(MLA_KV_RANK, MLA_HEADS, MLA_NOPE + MLA_V)
    w_k = _pad_heads(kv[:, :, :MLA_NOPE].reshape(MLA_KV_RANK, MLA_HEADS * MLA_NOPE), MLA_HEADS, MLA_NOPE)
    w_v = kv[:, :, MLA_NOPE:].reshape(MLA_KV_RANK, MLA_VW)
    gw = lru_gate_w[layer]
    nblk = gw.shape[2]
    per_group = LANES // LRU_BLOCK_DIM
    eye = jnp.eye(per_group, dtype=F32)
    gw = gw.reshape(2, 2, nblk // per_group, per_group, LRU_BLOCK_DIM, LRU_BLOCK_DIM)
    w_gate = jnp.einsum("zgpncm,nk->pnczgkm", gw, eye).reshape(nblk // per_group, LANES, 4 * LANES)
    b_gate = lru_gate_b[layer].reshape(2, 2, nblk // per_group, LANES).transpose(2, 0, 1, 3).reshape(
        nblk // per_group, 1, 4 * LANES)
    gq = mla_q_g[layer] * (MLA_QK ** -0.5 * LOG2_E)
    gk = mla_k_g[layer]
    gqs = swa_q_g[layer] * (SWA_HEAD_DIM ** -0.5 * LOG2_E)
    gks = swa_k_g[layer]
    rope_gain = lambda g: _pad_heads(g[MLA_NOPE:][pm][None], 1, MLA_ROPE, MLA_NOPE)
    c0 = 2 * GROUP_WIDTH
    gg = jnp.concatenate([group_g[layer][:c0], group_g[layer][c0:][_SWA_LANE_ORDER]])
    w_o = jnp.concatenate([w_out[layer][:c0], w_out[layer][c0:][_SWA_LANE_ORDER]], axis=0)
    return {
        "n1g": norm1_g[layer][None], "w_in": w_in_p.astype(BF16), "qag": q_a_g[layer][None],
        "w_uq": w_uq_p.astype(BF16), "kvag": kv_a_g[layer][None],
        "w_uk": w_k.astype(BF16), "w_vt": w_v.T.astype(BF16),
        "gq": _pad_heads(gq[None], 1, MLA_QK), "gq_rot": rope_gain(gq),
        "gk": _pad_heads(gk[None], 1, MLA_QK), "gk_rot": rope_gain(gk),
        "gqs": jnp.tile(gqs, 2)[None], "gqs_rot": jnp.tile(gqs[ps], 2)[None],
        "gks": jnp.tile(gks, 2)[None], "gks_rot": jnp.tile(gks[ps], 2)[None],
        "conv_w": conv_w[layer], "conv_b": conv_b[layer][None], "w_gate": w_gate.astype(BF16), "b_gate": b_gate,
        "lam": lru_lambda[layer], "sink": swa_sink[layer],
        "gg": gg[None], "w_out": w_o.astype(BF16), "n2g": norm2_g[layer][None],
        "w_ff1": w_ff1[layer].astype(BF16), "w_ff2": w_ff2[layer].astype(BF16),
    }


def kernel(x, c, ctx, c_ctx, w_mod, b_mod, norm1_g, w_in, q_a_g, w_uq, kv_a_g, w_ukv, mla_q_g, mla_k_g, conv_w, conv_b, lru_gate_w, lru_gate_b, lru_lambda, swa_q_g, swa_k_g, swa_sink, group_g, w_out, norm2_g, w_ff1, w_ff2):
    b, l, d = x.shape
    lc = ctx.shape[1]
    depth = w_mod.shape[0]
    rows = l // GRID_W

    n_rows = -(-(b + 1) // SUBLANES) * SUBLANES
    cc = jnp.concatenate([c, c_ctx[None], jnp.zeros((n_rows - b - 1, d), F32)], axis=0)
    mod_all = _modulation(cc, w_mod, b_mod)

    rope_m = _rope_tables(rows, MLA_ROPE, MLA_NOPE, LANES)
    rope_s = _rope_tables(rows, SWA_HEAD_DIM, 0, LANES)
    no_rope = _identity_tables(lc, LANES)

    xc = ctx
    for layer in range(depth):
        last = layer == depth - 1
        wts = _layer_weights(layer, norm1_g, w_in, q_a_g, w_uq, kv_a_g, w_ukv, mla_q_g, mla_k_g, conv_w, conv_b,
                             lru_gate_w, lru_gate_b, lru_lambda, swa_q_g, swa_k_g, swa_sink, group_g, w_out,
                             norm2_g, w_ff1, w_ff2)
        mod_l = mod_all[layer, :b][:, None, :]
        mod_c = jnp.broadcast_to(mod_all[layer, b][None, None, :], (b, 1, N_MOD * d))

        qm, km, vt, lx, lg, qs, ks, vts = _premix(x, mod_l, wts, rope_m, rope_s)
        qm_c, km_c, vt_c, lx_c, lg_c, qs_c, ks_c, vts_c = _premix(xc, mod_c, wts, no_rope, no_rope)

        o_b, o_b_c = _rglru(lx_c, lx, lg_c, lg, wts)
        o_a = _mla_attention(qm, km_c, vt_c, km, vt)
        o_c = _swa_attention(qs, ks_c, vts_c, wts["sink"], ks, vts)
        x = _postmix(x, o_a, o_b, o_c, mod_l, wts)
        if not last:
            o_a_c = _mla_attention(qm_c, km_c, vt_c)
            o_c_c = _swa_attention(qs_c, ks_c, vts_c, wts["sink"])
            xc = _postmix(xc, o_a_c, o_b_c, o_c_c, mod_c, wts)
    return x
```

```python
import functools

import numpy as np
import jax
import jax.numpy as jnp
from jax import lax
from jax.experimental import pallas as pl
from jax.experimental.pallas import tpu as pltpu

F32 = jnp.float32
BF16 = jnp.bfloat16

GRID_W = 64
WINDOW = 128
ROPE_THETA = 10000.0
EPS = 1e-6
NEG_INF = -1e30
N_MOD = 6
MLA_HEADS = 8
MLA_NOPE = 64
MLA_ROPE = 32
MLA_QK = MLA_NOPE + MLA_ROPE
MLA_V = 64
MLA_Q_RANK = 256
MLA_KV_RANK = 128
LRU_WIDTH = 512
LRU_BLOCK_DIM = 64
LRU_C = 8.0
CONV_W = 4
SWA_HEADS = 8
SWA_KV_HEADS = 2
SWA_GROUP = SWA_HEADS // SWA_KV_HEADS
SWA_HEAD_DIM = 64
GROUP_WIDTH = 512

LANES = 128
SUBLANES = 8
V7X_VMEM_BYTES = 64 * 1024 * 1024
VMEM_LIMIT_BYTES = V7X_VMEM_BYTES - 8 * 1024 * 1024

MLA_QW = MLA_HEADS * LANES
MLA_VW = MLA_HEADS * MLA_V
MLA_HEADS_PER_STEP = 8
MLA_KEY_CHUNK = 512
LOG2_E = 1.4426950408889634
SWA_QW = SWA_HEADS * SWA_HEAD_DIM
SWA_KW = SWA_KV_HEADS * SWA_HEAD_DIM
SWA_PAIRS = SWA_QW // LANES
SWA_Q_BLOCKS_PER_STEP = 4
FF_CHUNK = 1024

OFF_CQ = 0
OFF_CKV = OFF_CQ + MLA_Q_RANK
OFF_KR = OFF_CKV + MLA_KV_RANK
SECTION_2 = OFF_KR + LANES
OFF_KRR = SECTION_2
OFF_SQ = OFF_KRR + LANES
OFF_SQR = OFF_SQ + SWA_QW
OFF_SK = OFF_SQR + SWA_QW
OFF_SKR = OFF_SK + SWA_KW
OFF_SV = OFF_SKR + SWA_KW
OFF_LX = OFF_SV + SWA_KW
OFF_LG = OFF_LX + LRU_WIDTH
W_IN_PAD = OFF_LG + LRU_WIDTH


def _dot(a, b):
    return jnp.dot(a, b, preferred_element_type=F32)


def _dot_nt(a, b):
    return lax.dot_general(a, b, (((1,), (1,)), ((), ())), preferred_element_type=F32)


def _rms(x):
    return x * lax.rsqrt(jnp.mean(x * x, axis=-1, keepdims=True) + EPS)


def _params(*sem, flags=None):
    return pltpu.CompilerParams(dimension_semantics=sem, vmem_limit_bytes=VMEM_LIMIT_BYTES, flags=flags)


def _const_spec(shape):
    zeros = (0,) * len(shape)
    return pl.BlockSpec(shape, lambda *_: zeros)


def _mod_kernel(c_ref, w_ref, b_ref, o_ref):
    c = c_ref[...]
    a = c * jax.nn.sigmoid(c)
    w = w_ref[0]
    a_hi = a.astype(BF16)
    a_lo = (a - a_hi.astype(F32)).astype(BF16)
    w_hi = w.astype(BF16)
    w_lo = (w - w_hi.astype(F32)).astype(BF16)
    o_ref[0] = _dot(a_hi, w_hi) + _dot(a_hi, w_lo) + _dot(a_lo, w_hi) + b_ref[0]


def _modulation(cc, w_mod, b_mod):
    depth, d, n = w_mod.shape
    rows = cc.shape[0]
    tn = n // 4
    return pl.pallas_call(
        _mod_kernel,
        grid=(depth, n // tn),
        in_specs=[
            pl.BlockSpec((rows, d), lambda l, j: (0, 0)),
            pl.BlockSpec((1, d, tn), lambda l, j: (l, 0, j)),
            pl.BlockSpec((1, 1, tn), lambda l, j: (l, 0, j)),
        ],
        out_specs=pl.BlockSpec((1, rows, tn), lambda l, j: (l, 0, j)),
        out_shape=jax.ShapeDtypeStruct((depth, rows, n), F32),
        compiler_params=_params("arbitrary", "arbitrary"),
        name="modulation",
    )(cc, w_mod, b_mod.reshape(depth, 1, n))


def _premix_kernel(x_ref, mod_ref, n1g_ref, win_ref, qag_ref, wuq_ref, kvag_ref, wuk_ref, wvt_ref,
                   gq_ref, gqr_ref, gk_ref, gkr_ref, gqs_ref, gqsr_ref, gks_ref, gksr_ref,
                   cm_ref, sm_ref, cs_ref, ss_ref,
                   qm_ref, km_ref, vt_ref, lx_ref, lg_ref, qs_ref, ks_ref, vts_ref):
    d = x_ref.shape[-1]
    tm = x_ref.shape[1]
    x = x_ref[0]
    shift = mod_ref[0, :, 0:d]
    scale = mod_ref[0, :, d:2 * d]
    h = (_rms(x) * n1g_ref[...] * (1.0 + scale) + shift).astype(BF16)

    cm, sm = cm_ref[...], sm_ref[...]
    cs, ss = cs_ref[...], ss_ref[...]

    p1 = _dot(h, win_ref[:, 0:SECTION_2])
    qn = (_rms(p1[:, OFF_CQ:OFF_CQ + MLA_Q_RANK]) * qag_ref[...]).astype(BF16)
    kvn = (_rms(p1[:, OFF_CKV:OFF_CKV + MLA_KV_RANK]) * kvag_ref[...]).astype(BF16)
    kr = p1[:, OFF_KR:OFF_KR + LANES]
    ss_kr = jnp.sum(kr * kr, axis=-1, keepdims=True)

    p2 = _dot(h, win_ref[:, SECTION_2:OFF_LX])
    qu = _dot(qn, wuq_ref[...])
    kvu = _dot(kvn, wuk_ref[...])
    vt_ref[0] = _dot_nt(wvt_ref[...], kvn).astype(BF16)
    ga_k = gk_ref[...] * cm
    shared = kr * ga_k + p2[:, 0:LANES] * (gkr_ref[...] * sm)
    lo = lax.broadcasted_iota(jnp.int32, (tm, LANES), 1) < SWA_HEAD_DIM

    def pair_norm_rope(off, off_rot, ga, gb):
        x1 = p2[:, off - SECTION_2:off - SECTION_2 + LANES]
        x2 = p2[:, off_rot - SECTION_2:off_rot - SECTION_2 + LANES]
        sq = x1 * x1
        s_lo = jnp.sum(jnp.where(lo, sq, 0.0), axis=-1, keepdims=True)
        s_hi = jnp.sum(jnp.where(lo, 0.0, sq), axis=-1, keepdims=True)
        inv = 1.0 / SWA_HEAD_DIM
        r = jnp.where(lo, lax.rsqrt(s_lo * inv + EPS), lax.rsqrt(s_hi * inv + EPS))
        return (r * (x1 * ga + x2 * gb)).astype(BF16)

    ga_s, gb_s = gqs_ref[...] * cs, gqsr_ref[...] * ss
    for g in range(SWA_PAIRS):
        qs_ref[0, :, g * LANES:(g + 1) * LANES] = pair_norm_rope(OFF_SQ + g * LANES, OFF_SQR + g * LANES, ga_s, gb_s)
    ks_ref[0] = pair_norm_rope(OFF_SK, OFF_SKR, gks_ref[...] * cs, gksr_ref[...] * ss)
    vts_ref[0] = p2[:, OFF_SV - SECTION_2:OFF_SV - SECTION_2 + SWA_KW].T.astype(BF16)

    p3 = _dot(h, win_ref[:, OFF_LX:W_IN_PAD])
    ga_q, gb_q = gq_ref[...] * cm, gqr_ref[...] * sm
    for hd in range(MLA_HEADS):
        x1 = qu[:, hd * LANES:(hd + 1) * LANES]
        x2 = qu[:, MLA_QW + hd * LANES:MLA_QW + (hd + 1) * LANES]
        r = lax.rsqrt(jnp.sum(x1 * x1, axis=-1, keepdims=True) * (1.0 / MLA_QK) + EPS)
        qm_ref[0, :, hd * LANES:(hd + 1) * LANES] = (r * (x1 * ga_q + x2 * gb_q)).astype(BF16)
    for hd in range(MLA_HEADS):
        x1 = kvu[:, hd * LANES:(hd + 1) * LANES]
        r = lax.rsqrt((jnp.sum(x1 * x1, axis=-1, keepdims=True) + ss_kr) * (1.0 / MLA_QK) + EPS)
        km_ref[0, :, hd * LANES:(hd + 1) * LANES] = (r * (x1 * ga_k + shared)).astype(BF16)
    lx_ref[0] = p3[:, 0:LRU_WIDTH]
    lg_ref[0] = p3[:, LRU_WIDTH:2 * LRU_WIDTH]


def _premix(x, mod, wts, rope_m, rope_s):
    b, l, d = x.shape
    tm = min(l, 512)
    consts = [wts[k] for k in ("n1g", "w_in", "qag", "w_uq", "kvag", "w_uk", "w_vt",
                               "gq", "gq_rot", "gk", "gk_rot", "gqs", "gqs_rot", "gks", "gks_rot")]
    tables = list(rope_m) + list(rope_s)
    tok = lambda w, dt: (pl.BlockSpec((1, tm, w), lambda bi, i: (bi, i, 0)), jax.ShapeDtypeStruct((b, l, w), dt))
    tr = lambda w: (pl.BlockSpec((1, w, tm), lambda bi, i: (bi, 0, i)), jax.ShapeDtypeStruct((b, w, l), BF16))
    outs = [tok(MLA_QW, BF16), tok(MLA_QW, BF16), tr(MLA_VW), tok(LRU_WIDTH, F32), tok(LRU_WIDTH, F32),
            tok(SWA_QW, BF16), tok(SWA_KW, BF16), tr(SWA_KW)]
    return pl.pallas_call(
        _premix_kernel,
        grid=(b, l // tm),
        in_specs=[tok(d, F32)[0], pl.BlockSpec((1, 1, N_MOD * d), lambda bi, i: (bi, 0, 0))]
        + [_const_spec(c.shape) for c in consts]
        + [pl.BlockSpec((tm, LANES), lambda bi, i: (i, 0)) for _ in tables],
        out_specs=[o[0] for o in outs],
        out_shape=[o[1] for o in outs],
        compiler_params=_params("arbitrary", "arbitrary"),
        name="premix",
    )(x, mod, *consts, *tables)


def _gelu_tanh(x):
    c = 0.7978845608028654
    half_x = 0.5 * x
    return half_x * jnp.tanh(x * (c + (c * 0.044715) * (x * x))) + half_x


def _lru_kernel(lxc_ref, lxl_ref, lgc_ref, lgl_ref, cw_ref, cb_ref, wg_ref, bg_ref, lam_ref,
                obl_ref, obc_ref, af, uf, ab, ub):
    lc = lxc_ref.shape[1]
    ll = lxl_ref.shape[1]
    cw = cw_ref[...]
    cb = cb_ref[...]
    nlam = -lam_ref[...]
    softplus = jnp.maximum(nlam, 0.0) + jnp.log1p(jnp.exp(-jnp.abs(nlam)))
    decay = (-0.5 * LRU_C * LOG2_E) * softplus

    def coefficients(x):
        t = x.shape[0]
        row = lax.broadcasted_iota(jnp.int32, (t, LANES), 0)
        xm2 = jnp.where(row >= 2, pltpu.roll(x, 2, 0), 0.0)
        xm1 = jnp.where(row >= 1, pltpu.roll(x, 1, 0), 0.0)
        xp1 = jnp.where(row < t - 1, pltpu.roll(x, t - 1, 0), 0.0)
        xc = cw[0:1] * xm2 + cw[1:2] * xm1 + cw[2:3] * x + cw[3:4] * xp1 + cb
        th = jnp.tanh(_dot(xc.astype(BF16), wg_ref[0]) + bg_ref[0])
        half_x = 0.5 * xc
        out = []
        for z in range(2):
            t_r = th[:, 2 * z * LANES:(2 * z + 1) * LANES]
            t_i = th[:, (2 * z + 1) * LANES:(2 * z + 2) * LANES]
            a = jnp.exp2(decay[z:z + 1] * t_r + decay[z:z + 1])
            y = 1.0 - a * a
            root = y * lax.rsqrt(jnp.maximum(y, jnp.finfo(F32).tiny))
            out.append((a, root * (half_x * t_i + half_x)))
        return out

    for x_ref, f_off, b_off in ((lxc_ref, 0, ll), (lxl_ref, lc, 0)):
        t = x_ref.shape[1]
        (a_f, u_f), (a_b, u_b) = coefficients(x_ref[0])
        af[f_off:f_off + t, :] = a_f
        uf[f_off:f_off + t, :] = u_f
        ab[b_off:b_off + t, :] = a_b
        ub[b_off:b_off + t, :] = u_b
    pitch = af.shape[0] // SUBLANES
    tail = af.shape[0] - (lc + ll)
    assert tail > 0
    for a_ref, u_ref in ((af, uf), (ab, ub)):
        a_ref[lc + ll:, :] = jnp.ones((tail, LANES), F32)
        u_ref[lc + ll:, :] = jnp.zeros((tail, LANES), F32)

    def streams(i):
        return pl.ds(i, SUBLANES, stride=pitch)

    def two_steps(a_ref, u_ref, i0, i1, h, p):
        a0, u0 = a_ref[streams(i0), :], u_ref[streams(i0), :]
        a1, u1 = a_ref[streams(i1), :], u_ref[streams(i1), :]
        a01 = a1 * a0
        u_ref[streams(i0), :] = a0 * h + u0
        a_ref[streams(i0), :] = a0 * p
        h = a01 * h + (a1 * u0 + u1)
        p = a01 * p
        u_ref[streams(i1), :] = h
        a_ref[streams(i1), :] = p
        return h, p

    def local_scan(i, state):
        h_f, p_f, h_b, p_b = state
        h_f, p_f = two_steps(af, uf, 2 * i, 2 * i + 1, h_f, p_f)
        h_b, p_b = two_steps(ab, ub, pitch - 1 - 2 * i, pitch - 2 - 2 * i, h_b, p_b)
        return h_f, p_f, h_b, p_b

    assert pitch % 2 == 0
    zeros8 = jnp.zeros((SUBLANES, LANES), F32)
    ones8 = jnp.ones((SUBLANES, LANES), F32)
    h_f, p_f, h_b, p_b = lax.fori_loop(0, pitch // 2, local_scan, (zeros8, ones8, zeros8, ones8), unroll=2)

    sub = lax.broadcasted_iota(jnp.int32, (SUBLANES, LANES), 0)
    c_f = zeros8
    c_b = zeros8
    row_f = jnp.zeros((1, LANES), F32)
    row_b = jnp.zeros((1, LANES), F32)
    for s in range(1, SUBLANES):
        row_f = h_f[s - 1:s] + p_f[s - 1:s] * row_f
        c_f = jnp.where(sub == s, row_f, c_f)
        sb = SUBLANES - 1 - s
        row_b = h_b[sb + 1:sb + 2] + p_b[sb + 1:sb + 2] * row_b
        c_b = jnp.where(sub == sb, row_b, c_b)

    def add_carry(i, _):
        uf[streams(i), :] = uf[streams(i), :] + af[streams(i), :] * c_f
        ub[streams(i), :] = ub[streams(i), :] + ab[streams(i), :] * c_b
        return 0

    lax.fori_loop(0, pitch, add_carry, 0, unroll=4)

    obl_ref[0] = (uf[lc:lc + ll, :] + ub[0:ll, :]) * _gelu_tanh(lgl_ref[0])
    obc_ref[0] = (uf[0:lc, :] + ub[ll:ll + lc, :]) * _gelu_tanh(lgc_ref[0])


def _rglru(lx_c, lx_l, lg_c, lg_l, wts):
    b, lc, w = lx_c.shape
    ll = lx_l.shape[1]
    ng = w // LANES
    pitch = -(-(lc + ll) // SUBLANES)
    pitch += (SUBLANES // 2 - pitch) % SUBLANES
    seq = lambda t: pl.BlockSpec((1, t, LANES), lambda bi, g: (bi, 0, g))
    return pl.pallas_call(
        _lru_kernel,
        grid=(b, ng),
        in_specs=[seq(lc), seq(ll), seq(lc), seq(ll),
                  pl.BlockSpec((CONV_W, LANES), lambda bi, g: (0, g)),
                  pl.BlockSpec((1, LANES), lambda bi, g: (0, g)),
                  pl.BlockSpec((1, LANES, 4 * LANES), lambda bi, g: (g, 0, 0)),
                  pl.BlockSpec((1, 1, 4 * LANES), lambda bi, g: (g, 0, 0)),
                  pl.BlockSpec((2, LANES), lambda bi, g: (0, g))],
        out_specs=[seq(ll), seq(lc)],
        out_shape=[jax.ShapeDtypeStruct((b, ll, w), F32), jax.ShapeDtypeStruct((b, lc, w), F32)],
        scratch_shapes=[pltpu.VMEM((SUBLANES * pitch, LANES), F32) for _ in range(4)],
        compiler_params=_params("arbitrary", "arbitrary"),
        name="rglru",
    )(lx_c, lx_l, lg_c, lg_l, wts["conv_w"], wts["conv_b"], wts["w_gate"], wts["b_gate"], wts["lam"])


def _mla_kernel(*refs, with_latent):
    if with_latent:
        q_ref, kc_ref, vtc_ref, kl_ref, vtl_ref, o_ref, s_buf = refs
        sources = ((kc_ref, vtc_ref), (kl_ref, vtl_ref))
    else:
        q_ref, kc_ref, vtc_ref, o_ref, s_buf = refs
        sources = ((kc_ref, vtc_ref),)
    tq = q_ref.shape[1]
    chunks, row = [], 0
    for k_ref, vt_ref in sources:
        n = k_ref.shape[1]
        for k0 in range(0, n, MLA_KEY_CHUNK):
            kn = min(MLA_KEY_CHUNK, n - k0)
            chunks.append((k_ref, vt_ref, k0, kn, row))
            row += kn

    def add(acc, x, op=jnp.add):
        return x if acc is None else op(acc, x)

    def score_chunk(hd, chunk, m8):
        k_ref, _, k0, kn, r0 = chunk
        sl = slice(hd * LANES, (hd + 1) * LANES)
        s = _dot_nt(k_ref[0, k0:k0 + kn, sl], q_ref[0, :, sl])
        s_buf[hd % 2, r0:r0 + kn, :] = s
        return add(m8, jnp.max(s.reshape(kn // SUBLANES, SUBLANES, tq), axis=0), jnp.maximum)

    def attend_chunk(hd, chunk, m, l8, o_t):
        _, vt_ref, k0, kn, r0 = chunk
        p = jnp.exp2(s_buf[hd % 2, r0:r0 + kn, :] - m)
        l8 = add(l8, jnp.sum(p.reshape(kn // SUBLANES, SUBLANES, tq), axis=0))
        o_t = add(o_t, _dot(vt_ref[0, hd * MLA_V:(hd + 1) * MLA_V, k0:k0 + kn], p.astype(BF16)))
        return l8, o_t

    outs = []
    m8 = None
    for chunk in chunks:
        m8 = score_chunk(0, chunk, m8)
    for hd in range(MLA_HEADS_PER_STEP):
        m = jnp.max(m8, axis=0, keepdims=True)
        m8, l8, o_t = None, None, None
        if hd + 1 < MLA_HEADS_PER_STEP:
            for chunk in chunks:
                m8 = score_chunk(hd + 1, chunk, m8)
        for chunk in chunks:
            l8, o_t = attend_chunk(hd, chunk, m, l8, o_t)
        outs.append(o_t * (1.0 / jnp.sum(l8, axis=0, keepdims=True)))
    o_ref[0] = jnp.concatenate(outs, axis=0).T.astype(o_ref.dtype)


def _mla_attention(q, k_c, vt_c, k_l=None, vt_l=None):
    b, lq, _ = q.shape
    lc = k_c.shape[1]
    with_latent = k_l is not None
    tq = min(lq, 512)
    qw = MLA_HEADS_PER_STEP * LANES
    vw = MLA_HEADS_PER_STEP * MLA_V
    nq = MLA_HEADS // MLA_HEADS_PER_STEP
    keys = lambda t: pl.BlockSpec((1, t, qw), lambda bi, hq, i: (bi, 0, hq))
    vals = lambda t: pl.BlockSpec((1, vw, t), lambda bi, hq, i: (bi, hq, 0))
    in_specs = [pl.BlockSpec((1, tq, qw), lambda bi, hq, i: (bi, i, hq)), keys(lc), vals(lc)]
    args = [q, k_c, vt_c]
    if with_latent:
        in_specs += [keys(k_l.shape[1]), vals(k_l.shape[1])]
        args += [k_l, vt_l]
    return pl.pallas_call(
        functools.partial(_mla_kernel, with_latent=with_latent),
        grid=(b, nq, lq // tq),
        in_specs=in_specs,
        out_specs=pl.BlockSpec((1, tq, vw), lambda bi, hq, i: (bi, i, hq)),
        out_shape=jax.ShapeDtypeStruct((b, lq, MLA_VW), BF16),
        scratch_shapes=[pltpu.VMEM((2, lc + (k_l.shape[1] if with_latent else 0), tq), F32)],
        compiler_params=_params("arbitrary", "arbitrary", "arbitrary"),
        name="mla_attention" if with_latent else "mla_attention_ctx",
    )(*args)


def _swa_kernel(*refs, with_window):
    if with_window:
        sink_ref, q_ref, kc_ref, vtc_ref, kl_ref, vtl_ref, o_ref = refs
    else:
        sink_ref, q_ref, kc_ref, vtc_ref, o_ref = refs
    bq = WINDOW
    cols = SWA_PAIRS * bq
    kc = kc_ref[0]
    lo = lax.broadcasted_iota(jnp.int32, (cols, LANES), 1) < SWA_HEAD_DIM
    seg = lax.broadcasted_iota(jnp.int32, (1, cols), 1) // bq

    span = bq + 2 * WINDOW
    sinks = []
    for kvh in range(SWA_KV_HEADS):
        sink = jnp.zeros((1, cols), F32)
        for g in range(SWA_PAIRS):
            sink = jnp.where(seg == g, sink_ref[kvh * SWA_GROUP + g] * LOG2_E, sink)
        sinks.append(sink)

    def scores(sb):
        r0 = sb * bq
        q = jnp.concatenate([q_ref[0, r0:r0 + bq, g * LANES:(g + 1) * LANES] for g in range(SWA_PAIRS)],
                            axis=0).astype(F32)
        w0 = None
        if with_window:
            q0 = pl.program_id(1) * q_ref.shape[1] + r0
            w0 = pl.multiple_of(jnp.clip(q0 - WINDOW, 0, kl_ref.shape[1] - span), LANES)
            kw = kl_ref[0, pl.ds(w0, span), :]
            kj = w0 + lax.broadcasted_iota(jnp.int32, (span, cols), 0)
            qi = q0 + (lax.broadcasted_iota(jnp.int32, (span, cols), 1) & (bq - 1))
            band = jnp.abs(qi - kj) <= WINDOW
        out = []
        for kvh in range(SWA_KV_HEADS):
            qh = (jnp.where(lo, 0.0, q) if kvh else jnp.where(lo, q, 0.0)).astype(BF16)
            s = [_dot_nt(kc, qh)]
            if with_window:
                s.append(jnp.where(band, _dot_nt(kw, qh), NEG_INF))
            out.append(s)
        return w0, out

    def attend(sb, w0, blk_scores):
        o_t = []
        for kvh, s in enumerate(blk_scores):
            sink = sinks[kvh]
            m = functools.reduce(jnp.maximum, [jnp.max(x, axis=0, keepdims=True) for x in s] + [sink])
            p = [jnp.exp2(x - m) for x in s]
            denom = sum(jnp.sum(x, axis=0, keepdims=True) for x in p) + jnp.exp2(sink - m)
            vs = slice(kvh * SWA_HEAD_DIM, (kvh + 1) * SWA_HEAD_DIM)
            o = _dot(vtc_ref[0, vs, :], p[0].astype(BF16))
            if with_window:
                o = o + _dot(vtl_ref[0, vs, pl.ds(w0, span)], p[1].astype(BF16))
            o_t.append(o * (1.0 / denom))
        out = jnp.concatenate(o_t, axis=0).T.astype(o_ref.dtype)
        for g in range(SWA_PAIRS):
            o_ref[0, sb * bq:(sb + 1) * bq, g * LANES:(g + 1) * LANES] = out[g * bq:(g + 1) * bq]

    n_blocks = q_ref.shape[1] // bq
    nxt = scores(0)
    for sb in range(n_blocks):
        cur = nxt
        if sb + 1 < n_blocks:
            nxt = scores(sb + 1)
        attend(sb, *cur)


def _swa_attention(q, k_c, vt_c, sink, k_l=None, vt_l=None):
    b, lq, _ = q.shape
    lc = k_c.shape[1]
    with_window = k_l is not None
    tq = min(lq, SWA_Q_BLOCKS_PER_STEP * WINDOW)
    keys = lambda t: pl.BlockSpec((1, t, SWA_KW), lambda bi, i: (bi, 0, 0))
    vals = lambda t: pl.BlockSpec((1, SWA_KW, t), lambda bi, i: (bi, 0, 0))
    in_specs = [pl.BlockSpec(memory_space=pltpu.SMEM),
                pl.BlockSpec((1, tq, SWA_QW), lambda bi, i: (bi, i, 0)), keys(lc), vals(lc)]
    args = [sink, q, k_c, vt_c]
    if with_window:
        in_specs += [keys(k_l.shape[1]), vals(k_l.shape[1])]
        args += [k_l, vt_l]
    return pl.pallas_call(
        functools.partial(_swa_kernel, with_window=with_window),
        grid=(b, lq // tq),
        in_specs=in_specs,
        out_specs=pl.BlockSpec((1, tq, SWA_QW), lambda bi, i: (bi, i, 0)),
        out_shape=jax.ShapeDtypeStruct((b, lq, SWA_QW), BF16),
        compiler_params=_params("arbitrary", "arbitrary"),
        name="swa_attention" if with_window else "swa_attention_ctx",
    )(*args)


def _postmix_kernel(x_ref, oa_ref, ob_ref, oc_ref, mod_ref, gg_ref, wout_ref, n2g_ref, w1_ref, w2_ref, o_ref):
    d = x_ref.shape[-1]
    gate1 = mod_ref[0, :, 2 * d:3 * d]
    shift2 = mod_ref[0, :, 3 * d:4 * d]
    scale2 = mod_ref[0, :, 4 * d:5 * d]
    gate2 = mod_ref[0, :, 5 * d:6 * d]
    tm = x_ref.shape[1]
    halves = [slice(r, r + tm // 2) for r in (0, tm // 2)]

    def mix(rows):
        y = None
        for gi, ref in enumerate((oa_ref, ob_ref, oc_ref)):
            sl = slice(gi * GROUP_WIDTH, (gi + 1) * GROUP_WIDTH)
            og = _rms(ref[0, rows, :].astype(F32)) * gg_ref[:, sl]
            part = _dot(og.astype(BF16), wout_ref[sl, :])
            y = part if y is None else y + part
        return y

    def residual_norm(rows, y):
        x1 = x_ref[0, rows, :] + gate1 * y
        return x1, (_rms(x1) * n2g_ref[...] * (1.0 + scale2) + shift2).astype(BF16)

    def mlp(h2):
        ff = None
        for j in range(w1_ref.shape[1] // FF_CHUNK):
            sl = slice(j * FF_CHUNK, (j + 1) * FF_CHUNK)
            hid = jnp.maximum(_dot(h2, w1_ref[:, sl]), 0.0)
            part = _dot((hid * hid).astype(BF16), w2_ref[sl, :])
            ff = part if ff is None else ff + part
        return ff

    y = [mix(rows) for rows in halves]
    for rows, y_half in zip(halves, y):
        x1, h2 = residual_norm(rows, y_half)
        o_ref[0, rows, :] = x1 + gate2 * mlp(h2)


def _postmix(x, o_a, o_b, o_c, mod, wts):
    b, l, d = x.shape
    tm = min(l, 512)
    tok = lambda w: pl.BlockSpec((1, tm, w), lambda bi, i: (bi, i, 0))
    consts = [wts["gg"], wts["w_out"], wts["n2g"], wts["w_ff1"], wts["w_ff2"]]
    return pl.pallas_call(
        _postmix_kernel,
        grid=(b, l // tm),
        in_specs=[tok(d), tok(GROUP_WIDTH), tok(GROUP_WIDTH), tok(GROUP_WIDTH),
                  pl.BlockSpec((1, 1, N_MOD * d), lambda bi, i: (bi, 0, 0))]
        + [_const_spec(c.shape) for c in consts],
        out_specs=tok(d),
        out_shape=jax.ShapeDtypeStruct((b, l, d), F32),
        compiler_params=_params("arbitrary", "arbitrary"),
        name="postmix",
    )(x, o_a, o_b, o_c, mod, *consts)


def _rot_partner(dim):
    quarter = dim // 4
    idx = np.arange(dim)
    is_a = (idx // quarter) % 2 == 0
    return np.where(is_a, idx + quarter, idx - quarter), np.where(is_a, -1.0, 1.0).astype(np.float32)


def _rope_tables(rows, dim, lead, width):
    quarter = dim // 4
    n = rows * GRID_W
    row = jnp.repeat(jnp.arange(rows), GRID_W)
    col = jnp.tile(jnp.arange(GRID_W), rows)
    inv_freq = ROPE_THETA ** (-jnp.arange(quarter, dtype=F32) / quarter)
    ang = jnp.stack([row, col], axis=-1).astype(F32)[:, :, None] * inv_freq
    ang = jnp.broadcast_to(ang[:, :, None, :], (n, 2, 2, quarter)).reshape(n, dim)
    sign = _rot_partner(dim)[1]

    def place(t, fill):
        t = jnp.concatenate([jnp.full((n, lead), fill, F32), t], axis=1)
        t = jnp.tile(t, (1, width // (lead + dim)))
        return jnp.concatenate([t, jnp.full((n, width - t.shape[1]), fill, F32)], axis=1)

    return place(jnp.cos(ang), 1.0), place(jnp.sin(ang) * sign, 0.0)


def _identity_tables(n, width):
    return jnp.ones((n, width), F32), jnp.zeros((n, width), F32)


def _pad_heads(w, heads, width, lead=0):
    shape = w.shape[:-1]
    w = w.reshape(shape + (heads, width))
    w = jnp.pad(w, [(0, 0)] * len(shape) + [(0, 0), (lead, LANES - lead - width)])
    return w.reshape(shape + (heads * LANES,))


def _split_in(w):
    sizes = (MLA_Q_RANK, MLA_KV_RANK, MLA_ROPE, LRU_WIDTH, LRU_WIDTH, SWA_QW, SWA_KW, SWA_KW)
    parts, start = [], 0
    for s in sizes:
        parts.append(w[:, start:start + s])
        start += s
    return parts


_SWA_HEAD_ORDER = np.arange(SWA_HEADS).reshape(SWA_KV_HEADS, SWA_PAIRS).T.reshape(-1)
_SWA_LANE_ORDER = (_SWA_HEAD_ORDER[:, None] * SWA_HEAD_DIM + np.arange(SWA_HEAD_DIM)[None, :]).reshape(-1)


def _layer_weights(layer, norm1_g, w_in, q_a_g, w_uq, kv_a_g, w_ukv, mla_q_g, mla_k_g, conv_w, conv_b,
                   lru_gate_w, lru_gate_b, lru_lambda, swa_q_g, swa_k_g, swa_sink, group_g, w_out,
                   norm2_g, w_ff1, w_ff2):
    pm, _ = _rot_partner(MLA_ROPE)
    ps, _ = _rot_partner(SWA_HEAD_DIM)
    cq, ckv, kr, lx, lg, sq, sk, sv = _split_in(w_in[layer])
    sq = sq[:, _SWA_LANE_ORDER]
    per_head = lambda w, heads, perm: w.reshape(w.shape[0], heads, -1)[:, :, perm].reshape(w.shape[0], -1)
    w_in_p = jnp.concatenate(
        [cq, ckv, _pad_heads(kr, 1, MLA_ROPE, MLA_NOPE), _pad_heads(kr[:, pm], 1, MLA_ROPE, MLA_NOPE),
         sq, per_head(sq, SWA_HEADS, ps), sk, per_head(sk, SWA_KV_HEADS, ps), sv, lx, lg], axis=1)
    uq = w_uq[layer].reshape(MLA_Q_RANK, MLA_HEADS, MLA_QK)
    uq_rot = uq[:, :, MLA_NOPE:][:, :, pm].reshape(MLA_Q_RANK, MLA_HEADS * MLA_ROPE)
    w_uq_p = jnp.concatenate([_pad_heads(w_uq[layer], MLA_HEADS, MLA_QK),
                              _pad_heads(uq_rot, MLA_HEADS, MLA_ROPE, MLA_NOPE)], axis=1)
    kv = w_ukv[layer].reshape(MLA_KV_RANK, MLA_HEADS, MLA_NOPE + MLA_V)
    w_k = _pad_heads(kv[:, :, :MLA_NOPE].reshape(MLA_KV_RANK, MLA_HEADS * MLA_NOPE), MLA_HEADS, MLA_NOPE)
    w_v = kv[:, :, MLA_NOPE:].reshape(MLA_KV_RANK, MLA_VW)
    gw = lru_gate_w[layer]
    nblk = gw.shape[2]
    per_group = LANES // LRU_BLOCK_DIM
    eye = jnp.eye(per_group, dtype=F32)
    gw = gw.reshape(2, 2, nblk // per_group, per_group, LRU_BLOCK_DIM, LRU_BLOCK_DIM)
    w_gate = 0.5 * jnp.einsum("zgpncm,nk->pnczgkm", gw, eye).reshape(nblk // per_group, LANES, 4 * LANES)
    b_gate = 0.5 * lru_gate_b[layer].reshape(2, 2, nblk // per_group, LANES).transpose(2, 0, 1, 3).reshape(
        nblk // per_group, 1, 4 * LANES)
    gq = mla_q_g[layer] * (MLA_QK ** -0.5 * LOG2_E)
    gk = mla_k_g[layer]
    gqs = swa_q_g[layer] * (SWA_HEAD_DIM ** -0.5 * LOG2_E)
    gks = swa_k_g[layer]
    rope_gain = lambda g: _pad_heads(g[MLA_NOPE:][pm][None], 1, MLA_ROPE, MLA_NOPE)
    c0 = 2 * GROUP_WIDTH
    gg = jnp.concatenate([group_g[layer][:c0], group_g[layer][c0:][_SWA_LANE_ORDER]])
    w_o = jnp.concatenate([w_out[layer][:c0], w_out[layer][c0:][_SWA_LANE_ORDER]], axis=0)
    return {
        "n1g": norm1_g[layer][None], "w_in": w_in_p.astype(BF16), "qag": q_a_g[layer][None],
        "w_uq": w_uq_p.astype(BF16), "kvag": kv_a_g[layer][None],
        "w_uk": w_k.astype(BF16), "w_vt": w_v.T.astype(BF16),
        "gq": _pad_heads(gq[None], 1, MLA_QK), "gq_rot": rope_gain(gq),
        "gk": _pad_heads(gk[None], 1, MLA_QK), "gk_rot": rope_gain(gk),
        "gqs": jnp.tile(gqs, 2)[None], "gqs_rot": jnp.tile(gqs[ps], 2)[None],
        "gks": jnp.tile(gks, 2)[None], "gks_rot": jnp.tile(gks[ps], 2)[None],
        "conv_w": conv_w[layer], "conv_b": conv_b[layer][None], "w_gate": w_gate.astype(BF16), "b_gate": b_gate,
        "lam": lru_lambda[layer], "sink": swa_sink[layer],
        "gg": gg[None], "w_out": w_o.astype(BF16), "n2g": norm2_g[layer][None],
        "w_ff1": w_ff1[layer].astype(BF16), "w_ff2": w_ff2[layer].astype(BF16),
    }


def kernel(x, c, ctx, c_ctx, w_mod, b_mod, norm1_g, w_in, q_a_g, w_uq, kv_a_g, w_ukv, mla_q_g, mla_k_g, conv_w, conv_b, lru_gate_w, lru_gate_b, lru_lambda, swa_q_g, swa_k_g, swa_sink, group_g, w_out, norm2_g, w_ff1, w_ff2):
    b, l, d = x.shape
    lc = ctx.shape[1]
    depth = w_mod.shape[0]
    rows = l // GRID_W

    n_rows = -(-(b + 1) // SUBLANES) * SUBLANES
    cc = jnp.concatenate([c, c_ctx[None], jnp.zeros((n_rows - b - 1, d), F32)], axis=0)
    mod_all = _modulation(cc, w_mod, b_mod)

    rope_m = _rope_tables(rows, MLA_ROPE, MLA_NOPE, LANES)
    rope_s = _rope_tables(rows, SWA_HEAD_DIM, 0, LANES)
    no_rope = _identity_tables(lc, LANES)

    xc = ctx
    for layer in range(depth):
        last = layer == depth - 1
        wts = _layer_weights(layer, norm1_g, w_in, q_a_g, w_uq, kv_a_g, w_ukv, mla_q_g, mla_k_g, conv_w, conv_b,
                             lru_gate_w, lru_gate_b, lru_lambda, swa_q_g, swa_k_g, swa_sink, group_g, w_out,
                             norm2_g, w_ff1, w_ff2)
        mod_l = mod_all[layer, :b][:, None, :]
        mod_c = jnp.broadcast_to(mod_all[layer, b][None, None, :], (b, 1, N_MOD * d))

        qm, km, vt, lx, lg, qs, ks, vts = _premix(x, mod_l, wts, rope_m, rope_s)
        qm_c, km_c, vt_c, lx_c, lg_c, qs_c, ks_c, vts_c = _premix(xc, mod_c, wts, no_rope, no_rope)

        o_b, o_b_c = _rglru(lx_c, lx, lg_c, lg, wts)
        o_a = _mla_attention(qm, km_c, vt_c, km, vt)
        o_c = _swa_attention(qs, ks_c, vts_c, wts["sink"], ks, vts)
        x = _postmix(x, o_a, o_b, o_c, mod_l, wts)
        if not last:
            o_a_c = _mla_attention(qm_c, km_c, vt_c)
            o_c_c = _swa_attention(qs_c, ks_c, vts_c, wts["sink"])
            xc = _postmix(xc, o_a_c, o_b_c, o_c_c, mod_c, wts)
    return x
```

```python
import functools

import numpy as np
import jax
import jax.numpy as jnp
from jax import lax
from jax.experimental import pallas as pl
from jax.experimental.pallas import tpu as pltpu

F32 = jnp.float32
BF16 = jnp.bfloat16

GRID_W = 64
WINDOW = 128
ROPE_THETA = 10000.0
EPS = 1e-6
NEG_INF = -1e30
N_MOD = 6
MLA_HEADS = 8
MLA_NOPE = 64
MLA_ROPE = 32
MLA_QK = MLA_NOPE + MLA_ROPE
MLA_V = 64
MLA_Q_RANK = 256
MLA_KV_RANK = 128
LRU_WIDTH = 512
LRU_BLOCK_DIM = 64
LRU_C = 8.0
CONV_W = 4
SWA_HEADS = 8
SWA_KV_HEADS = 2
SWA_GROUP = SWA_HEADS // SWA_KV_HEADS
SWA_HEAD_DIM = 64
GROUP_WIDTH = 512

LANES = 128
SUBLANES = 8
V7X_VMEM_BYTES = 64 * 1024 * 1024
VMEM_LIMIT_BYTES = V7X_VMEM_BYTES - 8 * 1024 * 1024

MLA_QW = MLA_HEADS * LANES
MLA_VW = MLA_HEADS * MLA_V
MLA_HEADS_PER_STEP = 8
MLA_KEY_CHUNK = 512
LOG2_E = 1.4426950408889634
SWA_QW = SWA_HEADS * SWA_HEAD_DIM
SWA_KW = SWA_KV_HEADS * SWA_HEAD_DIM
SWA_PAIRS = SWA_QW // LANES
SWA_Q_BLOCKS_PER_STEP = 8
FF_CHUNK = 1024

OFF_CQ = 0
OFF_CKV = OFF_CQ + MLA_Q_RANK
OFF_KR = OFF_CKV + MLA_KV_RANK
SECTION_2 = OFF_KR + LANES
OFF_KRR = SECTION_2
OFF_SQ = OFF_KRR + LANES
OFF_SQR = OFF_SQ + SWA_QW
OFF_SK = OFF_SQR + SWA_QW
OFF_SKR = OFF_SK + SWA_KW
OFF_SV = OFF_SKR + SWA_KW
OFF_LX = OFF_SV + SWA_KW
OFF_LG = OFF_LX + LRU_WIDTH
W_IN_PAD = OFF_LG + LRU_WIDTH


def _dot(a, b):
    return jnp.dot(a, b, preferred_element_type=F32)


def _dot_nt(a, b):
    return lax.dot_general(a, b, (((1,), (1,)), ((), ())), preferred_element_type=F32)


def _rms(x):
    return x * lax.rsqrt(jnp.mean(x * x, axis=-1, keepdims=True) + EPS)


def _params(*sem, flags=None):
    return pltpu.CompilerParams(dimension_semantics=sem, vmem_limit_bytes=VMEM_LIMIT_BYTES, flags=flags)


def _layer_spec(arr, layer):
    index = (layer,) + (0,) * (arr.ndim - 1)
    return pl.BlockSpec((None,) + arr.shape[1:], lambda *_: index)


def _mod_spec(mod, layer, shared_row):
    if shared_row is None:
        return pl.BlockSpec((None, 1, 1, mod.shape[-1]), lambda bi, i: (layer, bi, 0, 0))
    return pl.BlockSpec((None, 1, 1, mod.shape[-1]), lambda bi, i: (layer, shared_row, 0, 0))


def _mod_kernel(c_ref, w_ref, b_ref, o_ref):
    c = c_ref[...]
    a = c * jax.nn.sigmoid(c)
    w = w_ref[0]
    a_hi = a.astype(BF16)
    a_lo = (a - a_hi.astype(F32)).astype(BF16)
    w_hi = w.astype(BF16)
    w_lo = (w - w_hi.astype(F32)).astype(BF16)
    o_ref[0] = _dot(a_hi, w_hi) + _dot(a_hi, w_lo) + _dot(a_lo, w_hi) + b_ref[0]


def _modulation(cc, w_mod, b_mod):
    depth, d, n = w_mod.shape
    rows = cc.shape[0]
    tn = n // 4
    return pl.pallas_call(
        _mod_kernel,
        grid=(depth, n // tn),
        in_specs=[
            pl.BlockSpec((rows, d), lambda l, j: (0, 0)),
            pl.BlockSpec((1, d, tn), lambda l, j: (l, 0, j)),
            pl.BlockSpec((1, 1, tn), lambda l, j: (l, 0, j)),
        ],
        out_specs=pl.BlockSpec((1, rows, tn), lambda l, j: (l, 0, j)),
        out_shape=jax.ShapeDtypeStruct((depth, rows, n), F32),
        compiler_params=_params("arbitrary", "arbitrary"),
        name="modulation",
    )(cc, w_mod, b_mod.reshape(depth, 1, n))


def _premix_kernel(x_ref, mod_ref, n1g_ref, win_ref, qag_ref, wuq_ref, kvag_ref, wuk_ref, wvt_ref,
                   gq_ref, gqr_ref, gk_ref, gkr_ref, gqs_ref, gqsr_ref, gks_ref, gksr_ref,
                   cm_ref, sm_ref, cs_ref, ss_ref,
                   qm_ref, km_ref, vt_ref, lx_ref, lg_ref, qs_ref, ks_ref, vts_ref):
    d = x_ref.shape[-1]
    tm = x_ref.shape[1]
    x = x_ref[0]
    shift = mod_ref[0, :, 0:d]
    scale = mod_ref[0, :, d:2 * d]
    h = (_rms(x) * n1g_ref[...] * (1.0 + scale) + shift).astype(BF16)

    cm, sm = cm_ref[...], sm_ref[...]
    cs, ss = cs_ref[...], ss_ref[...]

    p1 = _dot(h, win_ref[:, 0:SECTION_2])
    qn = (_rms(p1[:, OFF_CQ:OFF_CQ + MLA_Q_RANK]) * qag_ref[...]).astype(BF16)
    kvn = (_rms(p1[:, OFF_CKV:OFF_CKV + MLA_KV_RANK]) * kvag_ref[...]).astype(BF16)
    kr = p1[:, OFF_KR:OFF_KR + LANES]
    ss_kr = jnp.sum(kr * kr, axis=-1, keepdims=True)

    p2 = _dot(h, win_ref[:, SECTION_2:OFF_LX])
    qu = _dot(qn, wuq_ref[...])
    kvu = _dot(kvn, wuk_ref[...])
    vt_ref[0] = _dot_nt(wvt_ref[...], kvn).astype(BF16)
    ga_k = gk_ref[...] * cm
    shared = kr * ga_k + p2[:, 0:LANES] * (gkr_ref[...] * sm)
    lo = lax.broadcasted_iota(jnp.int32, (tm, LANES), 1) < SWA_HEAD_DIM

    def pair_norm_rope(off, off_rot, ga, gb):
        x1 = p2[:, off - SECTION_2:off - SECTION_2 + LANES]
        x2 = p2[:, off_rot - SECTION_2:off_rot - SECTION_2 + LANES]
        sq = x1 * x1
        s_lo = jnp.sum(jnp.where(lo, sq, 0.0), axis=-1, keepdims=True)
        s_hi = jnp.sum(jnp.where(lo, 0.0, sq), axis=-1, keepdims=True)
        inv = 1.0 / SWA_HEAD_DIM
        r = jnp.where(lo, lax.rsqrt(s_lo * inv + EPS), lax.rsqrt(s_hi * inv + EPS))
        return (r * (x1 * ga + x2 * gb)).astype(BF16)

    ga_s, gb_s = gqs_ref[...] * cs, gqsr_ref[...] * ss
    for g in range(SWA_PAIRS):
        qs_ref[0, :, g * LANES:(g + 1) * LANES] = pair_norm_rope(OFF_SQ + g * LANES, OFF_SQR + g * LANES, ga_s, gb_s)
    ks_ref[0] = pair_norm_rope(OFF_SK, OFF_SKR, gks_ref[...] * cs, gksr_ref[...] * ss)
    vts_ref[0] = p2[:, OFF_SV - SECTION_2:OFF_SV - SECTION_2 + SWA_KW].T.astype(BF16)

    p3 = _dot(h, win_ref[:, OFF_LX:W_IN_PAD])
    ga_q, gb_q = gq_ref[...] * cm, gqr_ref[...] * sm
    for hd in range(MLA_HEADS):
        x1 = qu[:, hd * LANES:(hd + 1) * LANES]
        x2 = qu[:, MLA_QW + hd * LANES:MLA_QW + (hd + 1) * LANES]
        r = lax.rsqrt(jnp.sum(x1 * x1, axis=-1, keepdims=True) * (1.0 / MLA_QK) + EPS)
        qm_ref[0, :, hd * LANES:(hd + 1) * LANES] = (r * (x1 * ga_q + x2 * gb_q)).astype(BF16)
    for hd in range(MLA_HEADS):
        x1 = kvu[:, hd * LANES:(hd + 1) * LANES]
        r = lax.rsqrt((jnp.sum(x1 * x1, axis=-1, keepdims=True) + ss_kr) * (1.0 / MLA_QK) + EPS)
        km_ref[0, :, hd * LANES:(hd + 1) * LANES] = (r * (x1 * ga_k + shared)).astype(BF16)
    lx_ref[0] = p3[:, 0:LRU_WIDTH]
    lg_ref[0] = p3[:, LRU_WIDTH:2 * LRU_WIDTH]


def _premix(x, mod, shared_row, layer, wts, rope_m, rope_s):
    b, l, d = x.shape
    tm = min(l, 512)
    consts = [wts[k] for k in ("n1g", "w_in", "qag", "w_uq", "kvag", "w_uk", "w_vt",
                               "gq", "gq_rot", "gk", "gk_rot", "gqs", "gqs_rot", "gks", "gks_rot")]
    tables = list(rope_m) + list(rope_s)
    tok = lambda w, dt: (pl.BlockSpec((1, tm, w), lambda bi, i: (bi, i, 0)), jax.ShapeDtypeStruct((b, l, w), dt))
    tr = lambda w: (pl.BlockSpec((1, w, tm), lambda bi, i: (bi, 0, i)), jax.ShapeDtypeStruct((b, w, l), BF16))
    outs = [tok(MLA_QW, BF16), tok(MLA_QW, BF16), tr(MLA_VW), tok(LRU_WIDTH, F32), tok(LRU_WIDTH, F32),
            tok(SWA_QW, BF16), tok(SWA_KW, BF16), tr(SWA_KW)]
    return pl.pallas_call(
        _premix_kernel,
        grid=(b, l // tm),
        in_specs=[tok(d, F32)[0], _mod_spec(mod, layer, shared_row)]
        + [_layer_spec(c, layer) for c in consts]
        + [pl.BlockSpec((tm, LANES), lambda bi, i: (i, 0)) for _ in tables],
        out_specs=[o[0] for o in outs],
        out_shape=[o[1] for o in outs],
        compiler_params=_params("arbitrary", "arbitrary"),
        name="premix",
    )(x, mod, *consts, *tables)


def _gelu_tanh(x):
    c = 0.7978845608028654
    half_x = 0.5 * x
    return half_x * jnp.tanh(x * (c + (c * 0.044715) * (x * x))) + half_x


def _lru_kernel(lxc_ref, lxl_ref, lgc_ref, lgl_ref, cw_ref, cb_ref, wg_ref, bg_ref, lam_ref,
                obl_ref, obc_ref, af, uf, ab, ub):
    lc = lxc_ref.shape[1]
    ll = lxl_ref.shape[1]
    cw = cw_ref[...]
    cb = cb_ref[...]
    nlam = -lam_ref[...]
    softplus = jnp.maximum(nlam, 0.0) + jnp.log1p(jnp.exp(-jnp.abs(nlam)))
    decay = (-0.5 * LRU_C * LOG2_E) * softplus

    def coefficients(x):
        t = x.shape[0]
        row = lax.broadcasted_iota(jnp.int32, (t, LANES), 0)
        xm2 = jnp.where(row >= 2, pltpu.roll(x, 2, 0), 0.0)
        xm1 = jnp.where(row >= 1, pltpu.roll(x, 1, 0), 0.0)
        xp1 = jnp.where(row < t - 1, pltpu.roll(x, t - 1, 0), 0.0)
        xc = cw[0:1] * xm2 + cw[1:2] * xm1 + cw[2:3] * x + cw[3:4] * xp1 + cb
        th = jnp.tanh(_dot(xc.astype(BF16), wg_ref[0]) + bg_ref[0])
        half_x = 0.5 * xc
        out = []
        for z in range(2):
            t_r = th[:, 2 * z * LANES:(2 * z + 1) * LANES]
            t_i = th[:, (2 * z + 1) * LANES:(2 * z + 2) * LANES]
            a = jnp.exp2(decay[z:z + 1] * t_r + decay[z:z + 1])
            y = 1.0 - a * a
            root = y * lax.rsqrt(jnp.maximum(y, jnp.finfo(F32).tiny))
            out.append((a, root * (half_x * t_i + half_x)))
        return out

    for x_ref, f_off, b_off in ((lxc_ref, 0, ll), (lxl_ref, lc, 0)):
        t = x_ref.shape[1]
        (a_f, u_f), (a_b, u_b) = coefficients(x_ref[0])
        af[f_off:f_off + t, :] = a_f
        uf[f_off:f_off + t, :] = u_f
        ab[b_off:b_off + t, :] = a_b
        ub[b_off:b_off + t, :] = u_b
    pitch = af.shape[0] // SUBLANES
    tail = af.shape[0] - (lc + ll)
    assert tail > 0
    for a_ref, u_ref in ((af, uf), (ab, ub)):
        a_ref[lc + ll:, :] = jnp.ones((tail, LANES), F32)
        u_ref[lc + ll:, :] = jnp.zeros((tail, LANES), F32)

    def streams(i):
        return pl.ds(i, SUBLANES, stride=pitch)

    def two_steps(a_ref, u_ref, i0, i1, h, p):
        a0, u0 = a_ref[streams(i0), :], u_ref[streams(i0), :]
        a1, u1 = a_ref[streams(i1), :], u_ref[streams(i1), :]
        a01 = a1 * a0
        u_ref[streams(i0), :] = a0 * h + u0
        a_ref[streams(i0), :] = a0 * p
        h = a01 * h + (a1 * u0 + u1)
        p = a01 * p
        u_ref[streams(i1), :] = h
        a_ref[streams(i1), :] = p
        return h, p

    def local_scan(i, state):
        h_f, p_f, h_b, p_b = state
        h_f, p_f = two_steps(af, uf, 2 * i, 2 * i + 1, h_f, p_f)
        h_b, p_b = two_steps(ab, ub, pitch - 1 - 2 * i, pitch - 2 - 2 * i, h_b, p_b)
        return h_f, p_f, h_b, p_b

    assert pitch % 2 == 0
    zeros8 = jnp.zeros((SUBLANES, LANES), F32)
    ones8 = jnp.ones((SUBLANES, LANES), F32)
    h_f, p_f, h_b, p_b = lax.fori_loop(0, pitch // 2, local_scan, (zeros8, ones8, zeros8, ones8), unroll=2)

    sub = lax.broadcasted_iota(jnp.int32, (SUBLANES, LANES), 0)
    c_f = zeros8
    c_b = zeros8
    row_f = jnp.zeros((1, LANES), F32)
    row_b = jnp.zeros((1, LANES), F32)
    for s in range(1, SUBLANES):
        row_f = h_f[s - 1:s] + p_f[s - 1:s] * row_f
        c_f = jnp.where(sub == s, row_f, c_f)
        sb = SUBLANES - 1 - s
        row_b = h_b[sb + 1:sb + 2] + p_b[sb + 1:sb + 2] * row_b
        c_b = jnp.where(sub == sb, row_b, c_b)

    def add_carry(i, _):
        uf[streams(i), :] = uf[streams(i), :] + af[streams(i), :] * c_f
        ub[streams(i), :] = ub[streams(i), :] + ab[streams(i), :] * c_b
        return 0

    lax.fori_loop(0, pitch, add_carry, 0, unroll=4)

    obl_ref[0] = (uf[lc:lc + ll, :] + ub[0:ll, :]) * _gelu_tanh(lgl_ref[0])
    obc_ref[0] = (uf[0:lc, :] + ub[ll:ll + lc, :]) * _gelu_tanh(lgc_ref[0])


def _rglru(lx_c, lx_l, lg_c, lg_l, layer, wts):
    b, lc, w = lx_c.shape
    ll = lx_l.shape[1]
    ng = w // LANES
    pitch = -(-(lc + ll) // SUBLANES)
    pitch += (SUBLANES // 2 - pitch) % SUBLANES
    seq = lambda t: pl.BlockSpec((1, t, LANES), lambda bi, g: (bi, 0, g))
    return pl.pallas_call(
        _lru_kernel,
        grid=(b, ng),
        in_specs=[seq(lc), seq(ll), seq(lc), seq(ll),
                  pl.BlockSpec((None, CONV_W, LANES), lambda bi, g: (layer, 0, g)),
                  pl.BlockSpec((None, 1, LANES), lambda bi, g: (layer, 0, g)),
                  pl.BlockSpec((None, 1, LANES, 4 * LANES), lambda bi, g: (layer, g, 0, 0)),
                  pl.BlockSpec((None, 1, 1, 4 * LANES), lambda bi, g: (layer, g, 0, 0)),
                  pl.BlockSpec((None, 2, LANES), lambda bi, g: (layer, 0, g))],
        out_specs=[seq(ll), seq(lc)],
        out_shape=[jax.ShapeDtypeStruct((b, ll, w), F32), jax.ShapeDtypeStruct((b, lc, w), F32)],
        scratch_shapes=[pltpu.VMEM((SUBLANES * pitch, LANES), F32) for _ in range(4)],
        compiler_params=_params("arbitrary", "arbitrary"),
        name="rglru",
    )(lx_c, lx_l, lg_c, lg_l, wts["conv_w"], wts["conv_b"], wts["w_gate"], wts["b_gate"], wts["lam"])


def _mla_kernel(*refs, with_latent):
    if with_latent:
        q_ref, kc_ref, vtc_ref, kl_ref, vtl_ref, o_ref, s_buf = refs
        sources = ((kc_ref, vtc_ref), (kl_ref, vtl_ref))
    else:
        q_ref, kc_ref, vtc_ref, o_ref, s_buf = refs
        sources = ((kc_ref, vtc_ref),)
    tq = q_ref.shape[1]
    chunks, row = [], 0
    for k_ref, vt_ref in sources:
        n = k_ref.shape[1]
        for k0 in range(0, n, MLA_KEY_CHUNK):
            kn = min(MLA_KEY_CHUNK, n - k0)
            chunks.append((k_ref, vt_ref, k0, kn, row))
            row += kn

    def add(acc, x, op=jnp.add):
        return x if acc is None else op(acc, x)

    def score_chunk(hd, chunk, m8):
        k_ref, _, k0, kn, r0 = chunk
        sl = slice(hd * LANES, (hd + 1) * LANES)
        s = _dot_nt(k_ref[0, k0:k0 + kn, sl], q_ref[0, :, sl])
        s_buf[hd % 2, r0:r0 + kn, :] = s
        return add(m8, jnp.max(s.reshape(kn // SUBLANES, SUBLANES, tq), axis=0), jnp.maximum)

    def attend_chunk(hd, chunk, m, l8, o_t):
        _, vt_ref, k0, kn, r0 = chunk
        p = jnp.exp2(s_buf[hd % 2, r0:r0 + kn, :] - m)
        l8 = add(l8, jnp.sum(p.reshape(kn // SUBLANES, SUBLANES, tq), axis=0))
        o_t = add(o_t, _dot(vt_ref[0, hd * MLA_V:(hd + 1) * MLA_V, k0:k0 + kn], p.astype(BF16)))
        return l8, o_t

    outs = []
    m8 = None
    for chunk in chunks:
        m8 = score_chunk(0, chunk, m8)
    for hd in range(MLA_HEADS_PER_STEP):
        m = jnp.max(m8, axis=0, keepdims=True)
        m8, l8, o_t = None, None, None
        if hd + 1 < MLA_HEADS_PER_STEP:
            for chunk in chunks:
                m8 = score_chunk(hd + 1, chunk, m8)
        for chunk in chunks:
            l8, o_t = attend_chunk(hd, chunk, m, l8, o_t)
        outs.append(o_t * (1.0 / jnp.sum(l8, axis=0, keepdims=True)))
    o_ref[0] = jnp.concatenate(outs, axis=0).T.astype(o_ref.dtype)


def _mla_attention(q, k_c, vt_c, k_l=None, vt_l=None):
    b, lq, _ = q.shape
    lc = k_c.shape[1]
    with_latent = k_l is not None
    tq = min(lq, 512)
    qw = MLA_HEADS_PER_STEP * LANES
    vw = MLA_HEADS_PER_STEP * MLA_V
    nq = MLA_HEADS // MLA_HEADS_PER_STEP
    keys = lambda t: pl.BlockSpec((1, t, qw), lambda bi, hq, i: (bi, 0, hq))
    vals = lambda t: pl.BlockSpec((1, vw, t), lambda bi, hq, i: (bi, hq, 0))
    in_specs = [pl.BlockSpec((1, tq, qw), lambda bi, hq, i: (bi, i, hq)), keys(lc), vals(lc)]
    args = [q, k_c, vt_c]
    if with_latent:
        in_specs += [keys(k_l.shape[1]), vals(k_l.shape[1])]
        args += [k_l, vt_l]
    return pl.pallas_call(
        functools.partial(_mla_kernel, with_latent=with_latent),
        grid=(b, nq, lq // tq),
        in_specs=in_specs,
        out_specs=pl.BlockSpec((1, tq, vw), lambda bi, hq, i: (bi, i, hq)),
        out_shape=jax.ShapeDtypeStruct((b, lq, MLA_VW), BF16),
        scratch_shapes=[pltpu.VMEM((2, lc + (k_l.shape[1] if with_latent else 0), tq), F32)],
        compiler_params=_params("arbitrary", "arbitrary", "arbitrary"),
        name="mla_attention" if with_latent else "mla_attention_ctx",
    )(*args)


def _swa_kernel(*refs, with_window, layer):
    if with_window:
        sink_ref, q_ref, kc_ref, vtc_ref, kl_ref, vtl_ref, o_ref = refs
    else:
        sink_ref, q_ref, kc_ref, vtc_ref, o_ref = refs
    bq = WINDOW
    cols = SWA_PAIRS * bq
    kc = kc_ref[0]
    lo = lax.broadcasted_iota(jnp.int32, (cols, LANES), 1) < SWA_HEAD_DIM
    seg = lax.broadcasted_iota(jnp.int32, (1, cols), 1) // bq

    span = bq + 2 * WINDOW
    sinks = []
    for kvh in range(SWA_KV_HEADS):
        sink = jnp.zeros((1, cols), F32)
        for g in range(SWA_PAIRS):
            sink = jnp.where(seg == g, sink_ref[layer, kvh * SWA_GROUP + g] * LOG2_E, sink)
        sinks.append(sink)

    def scores(sb):
        r0 = sb * bq
        q = jnp.concatenate([q_ref[0, r0:r0 + bq, g * LANES:(g + 1) * LANES] for g in range(SWA_PAIRS)],
                            axis=0).astype(F32)
        w0 = None
        if with_window:
            q0 = pl.program_id(1) * q_ref.shape[1] + r0
            w0 = pl.multiple_of(jnp.clip(q0 - WINDOW, 0, kl_ref.shape[1] - span), LANES)
            kw = kl_ref[0, pl.ds(w0, span), :]
            kj = w0 + lax.broadcasted_iota(jnp.int32, (span, cols), 0)
            qi = q0 + (lax.broadcasted_iota(jnp.int32, (span, cols), 1) & (bq - 1))
            band = jnp.abs(qi - kj) <= WINDOW
        out = []
        for kvh in range(SWA_KV_HEADS):
            qh = (jnp.where(lo, 0.0, q) if kvh else jnp.where(lo, q, 0.0)).astype(BF16)
            s = [_dot_nt(kc, qh)]
            if with_window:
                s.append(jnp.where(band, _dot_nt(kw, qh), NEG_INF))
            out.append(s)
        return w0, out

    def attend(sb, w0, blk_scores):
        o_t = []
        for kvh, s in enumerate(blk_scores):
            sink = sinks[kvh]
            m = functools.reduce(jnp.maximum, [jnp.max(x, axis=0, keepdims=True) for x in s] + [sink])
            p = [jnp.exp2(x - m) for x in s]
            denom = sum(jnp.sum(x, axis=0, keepdims=True) for x in p) + jnp.exp2(sink - m)
            vs = slice(kvh * SWA_HEAD_DIM, (kvh + 1) * SWA_HEAD_DIM)
            o = _dot(vtc_ref[0, vs, :], p[0].astype(BF16))
            if with_window:
                o = o + _dot(vtl_ref[0, vs, pl.ds(w0, span)], p[1].astype(BF16))
            o_t.append(o * (1.0 / denom))
        out = jnp.concatenate(o_t, axis=0).T.astype(o_ref.dtype)
        for g in range(SWA_PAIRS):
            o_ref[0, sb * bq:(sb + 1) * bq, g * LANES:(g + 1) * LANES] = out[g * bq:(g + 1) * bq]

    n_blocks = q_ref.shape[1] // bq
    nxt = scores(0)
    for sb in range(n_blocks):
        cur = nxt
        if sb + 1 < n_blocks:
            nxt = scores(sb + 1)
        attend(sb, *cur)


def _swa_attention(q, k_c, vt_c, sink, layer, k_l=None, vt_l=None):
    b, lq, _ = q.shape
    lc = k_c.shape[1]
    with_window = k_l is not None
    tq = min(lq, SWA_Q_BLOCKS_PER_STEP * WINDOW)
    keys = lambda t: pl.BlockSpec((1, t, SWA_KW), lambda bi, i: (bi, 0, 0))
    vals = lambda t: pl.BlockSpec((1, SWA_KW, t), lambda bi, i: (bi, 0, 0))
    in_specs = [pl.BlockSpec(memory_space=pltpu.SMEM),
                pl.BlockSpec((1, tq, SWA_QW), lambda bi, i: (bi, i, 0)), keys(lc), vals(lc)]
    args = [sink, q, k_c, vt_c]
    if with_window:
        in_specs += [keys(k_l.shape[1]), vals(k_l.shape[1])]
        args += [k_l, vt_l]
    return pl.pallas_call(
        functools.partial(_swa_kernel, with_window=with_window, layer=layer),
        grid=(b, lq // tq),
        in_specs=in_specs,
        out_specs=pl.BlockSpec((1, tq, SWA_QW), lambda bi, i: (bi, i, 0)),
        out_shape=jax.ShapeDtypeStruct((b, lq, SWA_QW), BF16),
        compiler_params=_params("arbitrary", "arbitrary"),
        name="swa_attention" if with_window else "swa_attention_ctx",
    )(*args)


def _postmix_kernel(x_ref, oa_ref, ob_ref, oc_ref, mod_ref, gg_ref, wout_ref, n2g_ref, w1_ref, w2_ref, o_ref):
    d = x_ref.shape[-1]
    gate1 = mod_ref[0, :, 2 * d:3 * d]
    shift2 = mod_ref[0, :, 3 * d:4 * d]
    scale2 = mod_ref[0, :, 4 * d:5 * d]
    gate2 = mod_ref[0, :, 5 * d:6 * d]
    tm = x_ref.shape[1]
    halves = [slice(r, r + tm // 2) for r in (0, tm // 2)]

    def mix(rows):
        y = None
        for gi, ref in enumerate((oa_ref, ob_ref, oc_ref)):
            sl = slice(gi * GROUP_WIDTH, (gi + 1) * GROUP_WIDTH)
            og = _rms(ref[0, rows, :].astype(F32)) * gg_ref[:, sl]
            part = _dot(og.astype(BF16), wout_ref[sl, :])
            y = part if y is None else y + part
        return y

    def residual_norm(rows, y):
        x1 = x_ref[0, rows, :] + gate1 * y
        return x1, (_rms(x1) * n2g_ref[...] * (1.0 + scale2) + shift2).astype(BF16)

    def mlp(h2):
        ff = None
        for j in range(w1_ref.shape[1] // FF_CHUNK):
            sl = slice(j * FF_CHUNK, (j + 1) * FF_CHUNK)
            hid = jnp.maximum(_dot(h2, w1_ref[:, sl]), 0.0)
            part = _dot((hid * hid).astype(BF16), w2_ref[sl, :])
            ff = part if ff is None else ff + part
        return ff

    y = [mix(rows) for rows in halves]
    for rows, y_half in zip(halves, y):
        x1, h2 = residual_norm(rows, y_half)
        o_ref[0, rows, :] = x1 + gate2 * mlp(h2)


def _postmix(x, o_a, o_b, o_c, mod, shared_row, layer, wts):
    b, l, d = x.shape
    tm = min(l, 512)
    tok = lambda w: pl.BlockSpec((1, tm, w), lambda bi, i: (bi, i, 0))
    consts = [wts["gg"], wts["w_out"], wts["n2g"], wts["w_ff1"], wts["w_ff2"]]
    return pl.pallas_call(
        _postmix_kernel,
        grid=(b, l // tm),
        in_specs=[tok(d), tok(GROUP_WIDTH), tok(GROUP_WIDTH), tok(GROUP_WIDTH), _mod_spec(mod, layer, shared_row)]
        + [_layer_spec(c, layer) for c in consts],
        out_specs=tok(d),
        out_shape=jax.ShapeDtypeStruct((b, l, d), F32),
        compiler_params=_params("arbitrary", "arbitrary"),
        name="postmix",
    )(x, o_a, o_b, o_c, mod, *consts)


def _rot_partner(dim):
    quarter = dim // 4
    idx = np.arange(dim)
    is_a = (idx // quarter) % 2 == 0
    return np.where(is_a, idx + quarter, idx - quarter), np.where(is_a, -1.0, 1.0).astype(np.float32)


def _rope_tables(rows, dim, lead, width):
    quarter = dim // 4
    n = rows * GRID_W
    row = jnp.repeat(jnp.arange(rows), GRID_W)
    col = jnp.tile(jnp.arange(GRID_W), rows)
    inv_freq = ROPE_THETA ** (-jnp.arange(quarter, dtype=F32) / quarter)
    ang = jnp.stack([row, col], axis=-1).astype(F32)[:, :, None] * inv_freq
    ang = jnp.broadcast_to(ang[:, :, None, :], (n, 2, 2, quarter)).reshape(n, dim)
    sign = _rot_partner(dim)[1]

    def place(t, fill):
        t = jnp.concatenate([jnp.full((n, lead), fill, F32), t], axis=1)
        t = jnp.tile(t, (1, width // (lead + dim)))
        return jnp.concatenate([t, jnp.full((n, width - t.shape[1]), fill, F32)], axis=1)

    return place(jnp.cos(ang), 1.0), place(jnp.sin(ang) * sign, 0.0)


def _identity_tables(n, width):
    return jnp.ones((n, width), F32), jnp.zeros((n, width), F32)


def _pad_heads(w, heads, width, lead=0):
    shape = w.shape[:-1]
    w = w.reshape(shape + (heads, width))
    w = jnp.pad(w, [(0, 0)] * len(shape) + [(0, 0), (lead, LANES - lead - width)])
    return w.reshape(shape + (heads * LANES,))


def _split_in(w):
    sizes = (MLA_Q_RANK, MLA_KV_RANK, MLA_ROPE, LRU_WIDTH, LRU_WIDTH, SWA_QW, SWA_KW, SWA_KW)
    parts, start = [], 0
    for s in sizes:
        parts.append(w[..., start:start + s])
        start += s
    return parts


_SWA_HEAD_ORDER = np.arange(SWA_HEADS).reshape(SWA_KV_HEADS, SWA_PAIRS).T.reshape(-1)
_SWA_LANE_ORDER = (_SWA_HEAD_ORDER[:, None] * SWA_HEAD_DIM + np.arange(SWA_HEAD_DIM)[None, :]).reshape(-1)


def _prepare_weights(norm1_g, w_in, q_a_g, w_uq, kv_a_g, w_ukv, mla_q_g, mla_k_g, conv_w, conv_b,
                     lru_gate_w, lru_gate_b, lru_lambda, swa_q_g, swa_k_g, group_g, w_out,
                     norm2_g, w_ff1, w_ff2):
    depth = w_in.shape[0]
    row = lambda v: v[:, None, :]
    pm, _ = _rot_partner(MLA_ROPE)
    ps, _ = _rot_partner(SWA_HEAD_DIM)
    cq, ckv, kr, lx, lg, sq, sk, sv = _split_in(w_in)
    sq = sq[..., _SWA_LANE_ORDER]
    per_head = lambda w, heads, perm: w.reshape(w.shape[:-1] + (heads, -1))[..., perm].reshape(w.shape)
    w_in_p = jnp.concatenate(
        [cq, ckv, _pad_heads(kr, 1, MLA_ROPE, MLA_NOPE), _pad_heads(kr[..., pm], 1, MLA_ROPE, MLA_NOPE),
         sq, per_head(sq, SWA_HEADS, ps), sk, per_head(sk, SWA_KV_HEADS, ps), sv, lx, lg], axis=-1)
    uq = w_uq.reshape(depth, MLA_Q_RANK, MLA_HEADS, MLA_QK)
    uq_rot = uq[..., MLA_NOPE:][..., pm].reshape(depth, MLA_Q_RANK, MLA_HEADS * MLA_ROPE)
    w_uq_p = jnp.concatenate([_pad_heads(w_uq, MLA_HEADS, MLA_QK),
                              _pad_heads(uq_rot, MLA_HEADS, MLA_ROPE, MLA_NOPE)], axis=-1)
    kv = w_ukv.reshape(depth, MLA_KV_RANK, MLA_HEADS, MLA_NOPE + MLA_V)
    w_k = _pad_heads(kv[..., :MLA_NOPE].reshape(depth, MLA_KV_RANK, MLA_HEADS * MLA_NOPE), MLA_HEADS, MLA_NOPE)
    w_vt = jnp.swapaxes(kv[..., MLA_NOPE:].reshape(depth, MLA_KV_RANK, MLA_VW), 1, 2)
    nblk = lru_gate_w.shape[3]
    per_group = LANES // LRU_BLOCK_DIM
    groups = nblk // per_group
    eye = jnp.eye(per_group, dtype=F32)
    gw = lru_gate_w.reshape(depth, 2, 2, groups, per_group, LRU_BLOCK_DIM, LRU_BLOCK_DIM)
    w_gate = 0.5 * jnp.einsum("dzgpncm,nk->dpnczgkm", gw, eye).reshape(depth, groups, LANES, 4 * LANES)
    b_gate = 0.5 * lru_gate_b.reshape(depth, 2, 2, groups, LANES).transpose(0, 3, 1, 2, 4).reshape(
        depth, groups, 1, 4 * LANES)
    gq = mla_q_g * (MLA_QK ** -0.5 * LOG2_E)
    gqs = swa_q_g * (SWA_HEAD_DIM ** -0.5 * LOG2_E)
    rope_gain = lambda g: _pad_heads(row(g[:, MLA_NOPE:][:, pm]), 1, MLA_ROPE, MLA_NOPE)
    pair = lambda g: row(jnp.tile(g, (1, LANES // SWA_HEAD_DIM)))
    c0 = 2 * GROUP_WIDTH
    gg = jnp.concatenate([group_g[:, :c0], group_g[:, c0:][:, _SWA_LANE_ORDER]], axis=1)
    w_o = jnp.concatenate([w_out[:, :c0], w_out[:, c0:][:, _SWA_LANE_ORDER]], axis=1)
    return {
        "n1g": row(norm1_g), "w_in": w_in_p.astype(BF16), "qag": row(q_a_g),
        "w_uq": w_uq_p.astype(BF16), "kvag": row(kv_a_g),
        "w_uk": w_k.astype(BF16), "w_vt": w_vt.astype(BF16),
        "gq": _pad_heads(row(gq), 1, MLA_QK), "gq_rot": rope_gain(gq),
        "gk": _pad_heads(row(mla_k_g), 1, MLA_QK), "gk_rot": rope_gain(mla_k_g),
        "gqs": pair(gqs), "gqs_rot": pair(gqs[:, ps]), "gks": pair(swa_k_g), "gks_rot": pair(swa_k_g[:, ps]),
        "conv_w": conv_w, "conv_b": row(conv_b), "w_gate": w_gate.astype(BF16), "b_gate": b_gate,
        "lam": lru_lambda,
        "gg": row(gg), "w_out": w_o.astype(BF16), "n2g": row(norm2_g),
        "w_ff1": w_ff1.astype(BF16), "w_ff2": w_ff2.astype(BF16),
    }


def kernel(x, c, ctx, c_ctx, w_mod, b_mod, norm1_g, w_in, q_a_g, w_uq, kv_a_g, w_ukv, mla_q_g, mla_k_g, conv_w, conv_b, lru_gate_w, lru_gate_b, lru_lambda, swa_q_g, swa_k_g, swa_sink, group_g, w_out, norm2_g, w_ff1, w_ff2):
    b, l, d = x.shape
    lc = ctx.shape[1]
    depth = w_mod.shape[0]
    rows = l // GRID_W

    n_rows = -(-(b + 1) // SUBLANES) * SUBLANES
    cc = jnp.concatenate([c, c_ctx[None], jnp.zeros((n_rows - b - 1, d), F32)], axis=0)
    mod = _modulation(cc, w_mod, b_mod).reshape(depth, n_rows, 1, N_MOD * d)

    rope_m = _rope_tables(rows, MLA_ROPE, MLA_NOPE, LANES)
    rope_s = _rope_tables(rows, SWA_HEAD_DIM, 0, LANES)
    no_rope = _identity_tables(lc, LANES)
    wts = _prepare_weights(norm1_g, w_in, q_a_g, w_uq, kv_a_g, w_ukv, mla_q_g, mla_k_g, conv_w, conv_b,
                           lru_gate_w, lru_gate_b, lru_lambda, swa_q_g, swa_k_g, group_g, w_out,
                           norm2_g, w_ff1, w_ff2)

    xc = ctx
    for layer in range(depth):
        last = layer == depth - 1
        qm, km, vt, lx, lg, qs, ks, vts = _premix(x, mod, None, layer, wts, rope_m, rope_s)
        qm_c, km_c, vt_c, lx_c, lg_c, qs_c, ks_c, vts_c = _premix(xc, mod, b, layer, wts, no_rope, no_rope)

        o_b, o_b_c = _rglru(lx_c, lx, lg_c, lg, layer, wts)
        o_a = _mla_attention(qm, km_c, vt_c, km, vt)
        o_c = _swa_attention(qs, ks_c, vts_c, swa_sink, layer, ks, vts)
        x = _postmix(x, o_a, o_b, o_c, mod, None, layer, wts)
        if not last:
            o_a_c = _mla_attention(qm_c, km_c, vt_c)
            o_c_c = _swa_attention(qs_c, ks_c, vts_c, swa_sink, layer)
            xc = _postmix(xc, o_a_c, o_b_c, o_c_c, mod, b, layer, wts)
    return x
```

```python
import functools

import numpy as np
import jax
import jax.numpy as jnp
from jax import lax
from jax.experimental import pallas as pl
from jax.experimental.pallas import tpu as pltpu

F32 = jnp.float32
BF16 = jnp.bfloat16

GRID_W = 64
WINDOW = 128
ROPE_THETA = 10000.0
EPS = 1e-6
NEG_INF = -1e30
N_MOD = 6
MLA_HEADS = 8
MLA_NOPE = 64
MLA_ROPE = 32
MLA_QK = MLA_NOPE + MLA_ROPE
MLA_V = 64
MLA_Q_RANK = 256
MLA_KV_RANK = 128
LRU_WIDTH = 512
LRU_BLOCK_DIM = 64
LRU_C = 8.0
CONV_W = 4
SWA_HEADS = 8
SWA_KV_HEADS = 2
SWA_GROUP = SWA_HEADS // SWA_KV_HEADS
SWA_HEAD_DIM = 64
GROUP_WIDTH = 512

LANES = 128
SUBLANES = 8
V7X_VMEM_BYTES = 64 * 1024 * 1024
VMEM_LIMIT_BYTES = V7X_VMEM_BYTES - 8 * 1024 * 1024

MLA_QW = MLA_HEADS * LANES
MLA_VW = MLA_HEADS * MLA_V
MLA_HEADS_PER_STEP = 8
MLA_KEY_CHUNK = 512
LOG2_E = 1.4426950408889634
SWA_QW = SWA_HEADS * SWA_HEAD_DIM
SWA_KW = SWA_KV_HEADS * SWA_HEAD_DIM
SWA_PAIRS = SWA_QW // LANES
SWA_Q_BLOCKS_PER_STEP = 8
FF_CHUNK = 1024

OFF_CQ = 0
OFF_CKV = OFF_CQ + MLA_Q_RANK
OFF_KR = OFF_CKV + MLA_KV_RANK
SECTION_2 = OFF_KR + LANES
OFF_KRR = SECTION_2
OFF_SQ = OFF_KRR + LANES
OFF_SQR = OFF_SQ + SWA_QW
OFF_SK = OFF_SQR + SWA_QW
OFF_SKR = OFF_SK + SWA_KW
OFF_SV = OFF_SKR + SWA_KW
OFF_LX = OFF_SV + SWA_KW
OFF_LG = OFF_LX + LRU_WIDTH
W_IN_PAD = OFF_LG + LRU_WIDTH


def _dot(a, b):
    return jnp.dot(a, b, preferred_element_type=F32)


def _dot_nt(a, b):
    return lax.dot_general(a, b, (((1,), (1,)), ((), ())), preferred_element_type=F32)


def _rms(x):
    return x * lax.rsqrt(jnp.mean(x * x, axis=-1, keepdims=True) + EPS)


def _params(*sem, flags=None):
    return pltpu.CompilerParams(dimension_semantics=sem, vmem_limit_bytes=VMEM_LIMIT_BYTES, flags=flags)


def _layer_spec(arr, layer):
    index = (layer,) + (0,) * (arr.ndim - 1)
    return pl.BlockSpec((None,) + arr.shape[1:], lambda *_: index)


def _mod_spec(mod, layer, shared_row):
    if shared_row is None:
        return pl.BlockSpec((None, 1, 1, mod.shape[-1]), lambda bi, i: (layer, bi, 0, 0))
    return pl.BlockSpec((None, 1, 1, mod.shape[-1]), lambda bi, i: (layer, shared_row, 0, 0))


def _mod_kernel(c_ref, w_ref, b_ref, o_ref):
    c = c_ref[...]
    a = c * jax.nn.sigmoid(c)
    w = w_ref[0]
    a_hi = a.astype(BF16)
    a_lo = (a - a_hi.astype(F32)).astype(BF16)
    w_hi = w.astype(BF16)
    w_lo = (w - w_hi.astype(F32)).astype(BF16)
    o_ref[0] = _dot(a_hi, w_hi) + _dot(a_hi, w_lo) + _dot(a_lo, w_hi) + b_ref[0]


def _modulation(cc, w_mod, b_mod):
    depth, d, n = w_mod.shape
    rows = cc.shape[0]
    tn = n // 4
    return pl.pallas_call(
        _mod_kernel,
        grid=(depth, n // tn),
        in_specs=[
            pl.BlockSpec((rows, d), lambda l, j: (0, 0)),
            pl.BlockSpec((1, d, tn), lambda l, j: (l, 0, j)),
            pl.BlockSpec((1, 1, tn), lambda l, j: (l, 0, j)),
        ],
        out_specs=pl.BlockSpec((1, rows, tn), lambda l, j: (l, 0, j)),
        out_shape=jax.ShapeDtypeStruct((depth, rows, n), F32),
        compiler_params=_params("arbitrary", "arbitrary"),
        name="modulation",
    )(cc, w_mod, b_mod.reshape(depth, 1, n))


def _premix_kernel(x_ref, mod_ref, n1g_ref, win_ref, qag_ref, wuq_ref, kvag_ref, wuk_ref, wvt_ref,
                   gq_ref, gqr_ref, gk_ref, gkr_ref, gqs_ref, gqsr_ref, gks_ref, gksr_ref,
                   cm_ref, sm_ref, cs_ref, ss_ref,
                   qm_ref, km_ref, vt_ref, lx_ref, lg_ref, qs_ref, ks_ref, vts_ref):
    d = x_ref.shape[-1]
    tm = x_ref.shape[1]
    shift = mod_ref[0, :, 0:d]
    scale = mod_ref[0, :, d:2 * d]
    for rows in (slice(0, tm // 2), slice(tm // 2, tm)):
        _premix_rows(rows, tm // 2, shift, scale, x_ref, n1g_ref, win_ref, qag_ref, wuq_ref, kvag_ref, wuk_ref, wvt_ref,
                     gq_ref, gqr_ref, gk_ref, gkr_ref, gqs_ref, gqsr_ref, gks_ref, gksr_ref,
                     cm_ref, sm_ref, cs_ref, ss_ref,
                     qm_ref, km_ref, vt_ref, lx_ref, lg_ref, qs_ref, ks_ref, vts_ref)


def _premix_rows(rows, n, shift, scale, x_ref, n1g_ref, win_ref, qag_ref, wuq_ref, kvag_ref, wuk_ref, wvt_ref,
                 gq_ref, gqr_ref, gk_ref, gkr_ref, gqs_ref, gqsr_ref, gks_ref, gksr_ref,
                 cm_ref, sm_ref, cs_ref, ss_ref,
                 qm_ref, km_ref, vt_ref, lx_ref, lg_ref, qs_ref, ks_ref, vts_ref):
    h = (_rms(x_ref[0, rows, :]) * n1g_ref[...] * (1.0 + scale) + shift).astype(BF16)

    cm, sm = cm_ref[rows, :], sm_ref[rows, :]
    cs, ss = cs_ref[rows, :], ss_ref[rows, :]

    p1 = _dot(h, win_ref[:, 0:SECTION_2])
    qn = (_rms(p1[:, OFF_CQ:OFF_CQ + MLA_Q_RANK]) * qag_ref[...]).astype(BF16)
    kvn = (_rms(p1[:, OFF_CKV:OFF_CKV + MLA_KV_RANK]) * kvag_ref[...]).astype(BF16)
    kr = p1[:, OFF_KR:OFF_KR + LANES]
    ss_kr = jnp.sum(kr * kr, axis=-1, keepdims=True)

    p2 = _dot(h, win_ref[:, SECTION_2:OFF_LX])
    qu = _dot(qn, wuq_ref[...])
    kvu = _dot(kvn, wuk_ref[...])
    vt_ref[0, :, rows] = _dot_nt(wvt_ref[...], kvn).astype(BF16)
    ga_k = gk_ref[...] * cm
    shared = kr * ga_k + p2[:, 0:LANES] * (gkr_ref[...] * sm)
    lo = lax.broadcasted_iota(jnp.int32, (n, LANES), 1) < SWA_HEAD_DIM

    def pair_norm_rope(off, off_rot, ga, gb):
        x1 = p2[:, off - SECTION_2:off - SECTION_2 + LANES]
        x2 = p2[:, off_rot - SECTION_2:off_rot - SECTION_2 + LANES]
        sq = x1 * x1
        s_lo = jnp.sum(jnp.where(lo, sq, 0.0), axis=-1, keepdims=True)
        s_hi = jnp.sum(jnp.where(lo, 0.0, sq), axis=-1, keepdims=True)
        inv = 1.0 / SWA_HEAD_DIM
        r = jnp.where(lo, lax.rsqrt(s_lo * inv + EPS), lax.rsqrt(s_hi * inv + EPS))
        return (r * (x1 * ga + x2 * gb)).astype(BF16)

    ga_s, gb_s = gqs_ref[...] * cs, gqsr_ref[...] * ss
    for g in range(SWA_PAIRS):
        qs_ref[0, rows, g * LANES:(g + 1) * LANES] = pair_norm_rope(OFF_SQ + g * LANES, OFF_SQR + g * LANES, ga_s, gb_s)
    ks_ref[0, rows, :] = pair_norm_rope(OFF_SK, OFF_SKR, gks_ref[...] * cs, gksr_ref[...] * ss)
    vts_ref[0, :, rows] = p2[:, OFF_SV - SECTION_2:OFF_SV - SECTION_2 + SWA_KW].T.astype(BF16)

    p3 = _dot(h, win_ref[:, OFF_LX:W_IN_PAD])
    ga_q, gb_q = gq_ref[...] * cm, gqr_ref[...] * sm
    for hd in range(MLA_HEADS):
        x1 = qu[:, hd * LANES:(hd + 1) * LANES]
        x2 = qu[:, MLA_QW + hd * LANES:MLA_QW + (hd + 1) * LANES]
        r = lax.rsqrt(jnp.sum(x1 * x1, axis=-1, keepdims=True) * (1.0 / MLA_QK) + EPS)
        qm_ref[0, rows, hd * LANES:(hd + 1) * LANES] = (r * (x1 * ga_q + x2 * gb_q)).astype(BF16)
    for hd in range(MLA_HEADS):
        x1 = kvu[:, hd * LANES:(hd + 1) * LANES]
        r = lax.rsqrt((jnp.sum(x1 * x1, axis=-1, keepdims=True) + ss_kr) * (1.0 / MLA_QK) + EPS)
        km_ref[0, rows, hd * LANES:(hd + 1) * LANES] = (r * (x1 * ga_k + shared)).astype(BF16)
    lx_ref[0, rows, :] = p3[:, 0:LRU_WIDTH]
    lg_ref[0, rows, :] = p3[:, LRU_WIDTH:2 * LRU_WIDTH]


def _premix(x, mod, shared_row, layer, wts, rope_m, rope_s):
    b, l, d = x.shape
    tm = min(l, 1024)
    consts = [wts[k] for k in ("n1g", "w_in", "qag", "w_uq", "kvag", "w_uk", "w_vt",
                               "gq", "gq_rot", "gk", "gk_rot", "gqs", "gqs_rot", "gks", "gks_rot")]
    tables = list(rope_m) + list(rope_s)
    tok = lambda w, dt: (pl.BlockSpec((1, tm, w), lambda bi, i: (bi, i, 0)), jax.ShapeDtypeStruct((b, l, w), dt))
    tr = lambda w: (pl.BlockSpec((1, w, tm), lambda bi, i: (bi, 0, i)), jax.ShapeDtypeStruct((b, w, l), BF16))
    outs = [tok(MLA_QW, BF16), tok(MLA_QW, BF16), tr(MLA_VW), tok(LRU_WIDTH, F32), tok(LRU_WIDTH, F32),
            tok(SWA_QW, BF16), tok(SWA_KW, BF16), tr(SWA_KW)]
    return pl.pallas_call(
        _premix_kernel,
        grid=(b, l // tm),
        in_specs=[tok(d, F32)[0], _mod_spec(mod, layer, shared_row)]
        + [_layer_spec(c, layer) for c in consts]
        + [pl.BlockSpec((tm, LANES), lambda bi, i: (i, 0)) for _ in tables],
        out_specs=[o[0] for o in outs],
        out_shape=[o[1] for o in outs],
        compiler_params=_params("arbitrary", "arbitrary"),
        name="premix",
    )(x, mod, *consts, *tables)


def _gelu_tanh(x):
    c = 0.7978845608028654
    half_x = 0.5 * x
    return half_x * jnp.tanh(x * (c + (c * 0.044715) * (x * x))) + half_x


def _lru_kernel(lxc_ref, lxl_ref, lgc_ref, lgl_ref, cw_ref, cb_ref, wg_ref, bg_ref, lam_ref,
                obl_ref, obc_ref, af, uf, ab, ub):
    lc = lxc_ref.shape[1]
    ll = lxl_ref.shape[1]
    cw = cw_ref[...]
    cb = cb_ref[...]
    nlam = -lam_ref[...]
    softplus = jnp.maximum(nlam, 0.0) + jnp.log1p(jnp.exp(-jnp.abs(nlam)))
    decay = (-0.5 * LRU_C * LOG2_E) * softplus

    def coefficients(x):
        t = x.shape[0]
        row = lax.broadcasted_iota(jnp.int32, (t, LANES), 0)
        xm2 = jnp.where(row >= 2, pltpu.roll(x, 2, 0), 0.0)
        xm1 = jnp.where(row >= 1, pltpu.roll(x, 1, 0), 0.0)
        xp1 = jnp.where(row < t - 1, pltpu.roll(x, t - 1, 0), 0.0)
        xc = cw[0:1] * xm2 + cw[1:2] * xm1 + cw[2:3] * x + cw[3:4] * xp1 + cb
        th = jnp.tanh(_dot(xc.astype(BF16), wg_ref[0]) + bg_ref[0])
        half_x = 0.5 * xc
        out = []
        for z in range(2):
            t_r = th[:, 2 * z * LANES:(2 * z + 1) * LANES]
            t_i = th[:, (2 * z + 1) * LANES:(2 * z + 2) * LANES]
            a = jnp.exp2(decay[z:z + 1] * t_r + decay[z:z + 1])
            y = 1.0 - a * a
            root = y * lax.rsqrt(jnp.maximum(y, jnp.finfo(F32).tiny))
            out.append((a, root * (half_x * t_i + half_x)))
        return out

    for x_ref, f_off, b_off in ((lxc_ref, 0, ll), (lxl_ref, lc, 0)):
        t = x_ref.shape[1]
        (a_f, u_f), (a_b, u_b) = coefficients(x_ref[0])
        af[f_off:f_off + t, :] = a_f
        uf[f_off:f_off + t, :] = u_f
        ab[b_off:b_off + t, :] = a_b
        ub[b_off:b_off + t, :] = u_b
    pitch = af.shape[0] // SUBLANES
    tail = af.shape[0] - (lc + ll)
    assert tail > 0
    for a_ref, u_ref in ((af, uf), (ab, ub)):
        a_ref[lc + ll:, :] = jnp.ones((tail, LANES), F32)
        u_ref[lc + ll:, :] = jnp.zeros((tail, LANES), F32)

    def streams(i):
        return pl.ds(i, SUBLANES, stride=pitch)

    def two_steps(a_ref, u_ref, i0, i1, h, p):
        a0, u0 = a_ref[streams(i0), :], u_ref[streams(i0), :]
        a1, u1 = a_ref[streams(i1), :], u_ref[streams(i1), :]
        a01 = a1 * a0
        u_ref[streams(i0), :] = a0 * h + u0
        a_ref[streams(i0), :] = a0 * p
        h = a01 * h + (a1 * u0 + u1)
        p = a01 * p
        u_ref[streams(i1), :] = h
        a_ref[streams(i1), :] = p
        return h, p

    def local_scan(i, state):
        h_f, p_f, h_b, p_b = state
        h_f, p_f = two_steps(af, uf, 2 * i, 2 * i + 1, h_f, p_f)
        h_b, p_b = two_steps(ab, ub, pitch - 1 - 2 * i, pitch - 2 - 2 * i, h_b, p_b)
        return h_f, p_f, h_b, p_b

    assert pitch % 2 == 0
    zeros8 = jnp.zeros((SUBLANES, LANES), F32)
    ones8 = jnp.ones((SUBLANES, LANES), F32)
    h_f, p_f, h_b, p_b = lax.fori_loop(0, pitch // 2, local_scan, (zeros8, ones8, zeros8, ones8), unroll=2)

    sub = lax.broadcasted_iota(jnp.int32, (SUBLANES, LANES), 0)
    c_f = zeros8
    c_b = zeros8
    row_f = jnp.zeros((1, LANES), F32)
    row_b = jnp.zeros((1, LANES), F32)
    for s in range(1, SUBLANES):
        row_f = h_f[s - 1:s] + p_f[s - 1:s] * row_f
        c_f = jnp.where(sub == s, row_f, c_f)
        sb = SUBLANES - 1 - s
        row_b = h_b[sb + 1:sb + 2] + p_b[sb + 1:sb + 2] * row_b
        c_b = jnp.where(sub == sb, row_b, c_b)

    def add_carry(i, _):
        uf[streams(i), :] = uf[streams(i), :] + af[streams(i), :] * c_f
        ub[streams(i), :] = ub[streams(i), :] + ab[streams(i), :] * c_b
        return 0

    lax.fori_loop(0, pitch, add_carry, 0, unroll=4)

    obl_ref[0] = (uf[lc:lc + ll, :] + ub[0:ll, :]) * _gelu_tanh(lgl_ref[0])
    obc_ref[0] = (uf[0:lc, :] + ub[ll:ll + lc, :]) * _gelu_tanh(lgc_ref[0])


def _rglru(lx_c, lx_l, lg_c, lg_l, layer, wts):
    b, lc, w = lx_c.shape
    ll = lx_l.shape[1]
    ng = w // LANES
    pitch = -(-(lc + ll) // SUBLANES)
    pitch += (SUBLANES // 2 - pitch) % SUBLANES
    seq = lambda t: pl.BlockSpec((1, t, LANES), lambda bi, g: (bi, 0, g))
    return pl.pallas_call(
        _lru_kernel,
        grid=(b, ng),
        in_specs=[seq(lc), seq(ll), seq(lc), seq(ll),
                  pl.BlockSpec((None, CONV_W, LANES), lambda bi, g: (layer, 0, g)),
                  pl.BlockSpec((None, 1, LANES), lambda bi, g: (layer, 0, g)),
                  pl.BlockSpec((None, 1, LANES, 4 * LANES), lambda bi, g: (layer, g, 0, 0)),
                  pl.BlockSpec((None, 1, 1, 4 * LANES), lambda bi, g: (layer, g, 0, 0)),
                  pl.BlockSpec((None, 2, LANES), lambda bi, g: (layer, 0, g))],
        out_specs=[seq(ll), seq(lc)],
        out_shape=[jax.ShapeDtypeStruct((b, ll, w), F32), jax.ShapeDtypeStruct((b, lc, w), F32)],
        scratch_shapes=[pltpu.VMEM((SUBLANES * pitch, LANES), F32) for _ in range(4)],
        compiler_params=_params("arbitrary", "arbitrary"),
        name="rglru",
    )(lx_c, lx_l, lg_c, lg_l, wts["conv_w"], wts["conv_b"], wts["w_gate"], wts["b_gate"], wts["lam"])


def _mla_kernel(*refs, with_latent):
    if with_latent:
        q_ref, kc_ref, vtc_ref, kl_ref, vtl_ref, o_ref, s_buf = refs
        sources = ((kc_ref, vtc_ref), (kl_ref, vtl_ref))
    else:
        q_ref, kc_ref, vtc_ref, o_ref, s_buf = refs
        sources = ((kc_ref, vtc_ref),)
    tq = q_ref.shape[1]
    chunks, row = [], 0
    for k_ref, vt_ref in sources:
        n = k_ref.shape[1]
        for k0 in range(0, n, MLA_KEY_CHUNK):
            kn = min(MLA_KEY_CHUNK, n - k0)
            chunks.append((k_ref, vt_ref, k0, kn, row))
            row += kn

    def add(acc, x, op=jnp.add):
        return x if acc is None else op(acc, x)

    def score_chunk(hd, chunk, m8):
        k_ref, _, k0, kn, r0 = chunk
        sl = slice(hd * LANES, (hd + 1) * LANES)
        s = _dot_nt(k_ref[0, k0:k0 + kn, sl], q_ref[0, :, sl])
        s_buf[hd % 2, r0:r0 + kn, :] = s
        return add(m8, jnp.max(s.reshape(kn // SUBLANES, SUBLANES, tq), axis=0), jnp.maximum)

    def attend_chunk(hd, chunk, m, l8, o_t):
        _, vt_ref, k0, kn, r0 = chunk
        p = jnp.exp2(s_buf[hd % 2, r0:r0 + kn, :] - m)
        l8 = add(l8, jnp.sum(p.reshape(kn // SUBLANES, SUBLANES, tq), axis=0))
        o_t = add(o_t, _dot(vt_ref[0, hd * MLA_V:(hd + 1) * MLA_V, k0:k0 + kn], p.astype(BF16)))
        return l8, o_t

    outs = []
    m8 = None
    for chunk in chunks:
        m8 = score_chunk(0, chunk, m8)
    for hd in range(MLA_HEADS_PER_STEP):
        m = jnp.max(m8, axis=0, keepdims=True)
        m8, l8, o_t = None, None, None
        if hd + 1 < MLA_HEADS_PER_STEP:
            for chunk in chunks:
                m8 = score_chunk(hd + 1, chunk, m8)
        for chunk in chunks:
            l8, o_t = attend_chunk(hd, chunk, m, l8, o_t)
        outs.append(o_t * (1.0 / jnp.sum(l8, axis=0, keepdims=True)))
    o_ref[0] = jnp.concatenate(outs, axis=0).T.astype(o_ref.dtype)


def _mla_attention(q, k_c, vt_c, k_l=None, vt_l=None):
    b, lq, _ = q.shape
    lc = k_c.shape[1]
    with_latent = k_l is not None
    tq = min(lq, 512)
    qw = MLA_HEADS_PER_STEP * LANES
    vw = MLA_HEADS_PER_STEP * MLA_V
    nq = MLA_HEADS // MLA_HEADS_PER_STEP
    keys = lambda t: pl.BlockSpec((1, t, qw), lambda bi, hq, i: (bi, 0, hq))
    vals = lambda t: pl.BlockSpec((1, vw, t), lambda bi, hq, i: (bi, hq, 0))
    in_specs = [pl.BlockSpec((1, tq, qw), lambda bi, hq, i: (bi, i, hq)), keys(lc), vals(lc)]
    args = [q, k_c, vt_c]
    if with_latent:
        in_specs += [keys(k_l.shape[1]), vals(k_l.shape[1])]
        args += [k_l, vt_l]
    return pl.pallas_call(
        functools.partial(_mla_kernel, with_latent=with_latent),
        grid=(b, nq, lq // tq),
        in_specs=in_specs,
        out_specs=pl.BlockSpec((1, tq, vw), lambda bi, hq, i: (bi, i, hq)),
        out_shape=jax.ShapeDtypeStruct((b, lq, MLA_VW), BF16),
        scratch_shapes=[pltpu.VMEM((2, lc + (k_l.shape[1] if with_latent else 0), tq), F32)],
        compiler_params=_params("arbitrary", "arbitrary", "arbitrary"),
        name="mla_attention" if with_latent else "mla_attention_ctx",
    )(*args)


def _swa_kernel(*refs, with_window, layer):
    if with_window:
        sink_ref, q_ref, kc_ref, vtc_ref, kl_ref, vtl_ref, o_ref = refs
    else:
        sink_ref, q_ref, kc_ref, vtc_ref, o_ref = refs
    bq = WINDOW
    cols = SWA_PAIRS * bq
    kc = kc_ref[0]
    lo = lax.broadcasted_iota(jnp.int32, (cols, LANES), 1) < SWA_HEAD_DIM
    seg = lax.broadcasted_iota(jnp.int32, (1, cols), 1) // bq

    span = bq + 2 * WINDOW
    sinks = []
    for kvh in range(SWA_KV_HEADS):
        sink = jnp.zeros((1, cols), F32)
        for g in range(SWA_PAIRS):
            sink = jnp.where(seg == g, sink_ref[layer, kvh * SWA_GROUP + g] * LOG2_E, sink)
        sinks.append(sink)

    def scores(sb):
        r0 = sb * bq
        q = jnp.concatenate([q_ref[0, r0:r0 + bq, g * LANES:(g + 1) * LANES] for g in range(SWA_PAIRS)],
                            axis=0).astype(F32)
        w0 = None
        if with_window:
            q0 = pl.program_id(1) * q_ref.shape[1] + r0
            w0 = pl.multiple_of(jnp.clip(q0 - WINDOW, 0, kl_ref.shape[1] - span), LANES)
            kw = kl_ref[0, pl.ds(w0, span), :]
            kj = w0 + lax.broadcasted_iota(jnp.int32, (span, cols), 0)
            qi = q0 + (lax.broadcasted_iota(jnp.int32, (span, cols), 1) & (bq - 1))
            band = jnp.abs(qi - kj) <= WINDOW
        out = []
        for kvh in range(SWA_KV_HEADS):
            qh = (jnp.where(lo, 0.0, q) if kvh else jnp.where(lo, q, 0.0)).astype(BF16)
            s = [_dot_nt(kc, qh)]
            if with_window:
                s.append(jnp.where(band, _dot_nt(kw, qh), NEG_INF))
            out.append(s)
        return w0, out

    def attend(sb, w0, blk_scores):
        o_t = []
        for kvh, s in enumerate(blk_scores):
            sink = sinks[kvh]
            m = functools.reduce(jnp.maximum, [jnp.max(x, axis=0, keepdims=True) for x in s] + [sink])
            p = [jnp.exp2(x - m) for x in s]
            denom = sum(jnp.sum(x, axis=0, keepdims=True) for x in p) + jnp.exp2(sink - m)
            vs = slice(kvh * SWA_HEAD_DIM, (kvh + 1) * SWA_HEAD_DIM)
            o = _dot(vtc_ref[0, vs, :], p[0].astype(BF16))
            if with_window:
                o = o + _dot(vtl_ref[0, vs, pl.ds(w0, span)], p[1].astype(BF16))
            o_t.append(o * (1.0 / denom))
        out = jnp.concatenate(o_t, axis=0).T.astype(o_ref.dtype)
        for g in range(SWA_PAIRS):
            o_ref[0, sb * bq:(sb + 1) * bq, g * LANES:(g + 1) * LANES] = out[g * bq:(g + 1) * bq]

    n_blocks = q_ref.shape[1] // bq
    nxt = scores(0)
    for sb in range(n_blocks):
        cur = nxt
        if sb + 1 < n_blocks:
            nxt = scores(sb + 1)
        attend(sb, *cur)


def _swa_attention(q, k_c, vt_c, sink, layer, k_l=None, vt_l=None):
    b, lq, _ = q.shape
    lc = k_c.shape[1]
    with_window = k_l is not None
    tq = min(lq, SWA_Q_BLOCKS_PER_STEP * WINDOW)
    keys = lambda t: pl.BlockSpec((1, t, SWA_KW), lambda bi, i: (bi, 0, 0))
    vals = lambda t: pl.BlockSpec((1, SWA_KW, t), lambda bi, i: (bi, 0, 0))
    in_specs = [pl.BlockSpec(memory_space=pltpu.SMEM),
                pl.BlockSpec((1, tq, SWA_QW), lambda bi, i: (bi, i, 0)), keys(lc), vals(lc)]
    args = [sink, q, k_c, vt_c]
    if with_window:
        in_specs += [keys(k_l.shape[1]), vals(k_l.shape[1])]
        args += [k_l, vt_l]
    return pl.pallas_call(
        functools.partial(_swa_kernel, with_window=with_window, layer=layer),
        grid=(b, lq // tq),
        in_specs=in_specs,
        out_specs=pl.BlockSpec((1, tq, SWA_QW), lambda bi, i: (bi, i, 0)),
        out_shape=jax.ShapeDtypeStruct((b, lq, SWA_QW), BF16),
        compiler_params=_params("arbitrary", "arbitrary"),
        name="swa_attention" if with_window else "swa_attention_ctx",
    )(*args)


def _postmix_kernel(x_ref, oa_ref, ob_ref, oc_ref, mod_ref, gg_ref, wout_ref, n2g_ref, w1_ref, w2_ref, o_ref):
    d = x_ref.shape[-1]
    gate1 = mod_ref[0, :, 2 * d:3 * d]
    shift2 = mod_ref[0, :, 3 * d:4 * d]
    scale2 = mod_ref[0, :, 4 * d:5 * d]
    gate2 = mod_ref[0, :, 5 * d:6 * d]
    tm = x_ref.shape[1]
    halves = [slice(r, r + tm // 2) for r in (0, tm // 2)]

    def mix(rows):
        y = None
        for gi, ref in enumerate((oa_ref, ob_ref, oc_ref)):
            sl = slice(gi * GROUP_WIDTH, (gi + 1) * GROUP_WIDTH)
            og = _rms(ref[0, rows, :].astype(F32)) * gg_ref[:, sl]
            part = _dot(og.astype(BF16), wout_ref[sl, :])
            y = part if y is None else y + part
        return y

    def residual_norm(rows, y):
        x1 = x_ref[0, rows, :] + gate1 * y
        return x1, (_rms(x1) * n2g_ref[...] * (1.0 + scale2) + shift2).astype(BF16)

    def mlp(h2):
        ff = None
        for j in range(w1_ref.shape[1] // FF_CHUNK):
            sl = slice(j * FF_CHUNK, (j + 1) * FF_CHUNK)
            hid = jnp.maximum(_dot(h2, w1_ref[:, sl]), 0.0)
            part = _dot((hid * hid).astype(BF16), w2_ref[sl, :])
            ff = part if ff is None else ff + part
        return ff

    y = [mix(rows) for rows in halves]
    for rows, y_half in zip(halves, y):
        x1, h2 = residual_norm(rows, y_half)
        o_ref[0, rows, :] = x1 + gate2 * mlp(h2)


def _postmix(x, o_a, o_b, o_c, mod, shared_row, layer, wts):
    b, l, d = x.shape
    tm = min(l, 512)
    tok = lambda w: pl.BlockSpec((1, tm, w), lambda bi, i: (bi, i, 0))
    consts = [wts["gg"], wts["w_out"], wts["n2g"], wts["w_ff1"], wts["w_ff2"]]
    return pl.pallas_call(
        _postmix_kernel,
        grid=(b, l // tm),
        in_specs=[tok(d), tok(GROUP_WIDTH), tok(GROUP_WIDTH), tok(GROUP_WIDTH), _mod_spec(mod, layer, shared_row)]
        + [_layer_spec(c, layer) for c in consts],
        out_specs=tok(d),
        out_shape=jax.ShapeDtypeStruct((b, l, d), F32),
        compiler_params=_params("arbitrary", "arbitrary"),
        name="postmix",
    )(x, o_a, o_b, o_c, mod, *consts)


def _rot_sign(dim):
    quarter = dim // 4
    return np.where((np.arange(dim) // quarter) % 2 == 0, -1.0, 1.0).astype(np.float32)


def _rot_partner(w, dim):
    return jnp.flip(w.reshape(w.shape[:-1] + (-1, 2, dim // 4)), axis=-2).reshape(w.shape)


def _pair_heads(w, axis):
    axis = axis % w.ndim
    shape = w.shape
    w = w.reshape(shape[:axis] + (SWA_KV_HEADS, SWA_PAIRS, SWA_HEAD_DIM) + shape[axis + 1:])
    return jnp.swapaxes(w, axis, axis + 1).reshape(shape)


def _rope_tables(rows, dim, lead, width):
    quarter = dim // 4
    n = rows * GRID_W
    row = jnp.repeat(jnp.arange(rows), GRID_W)
    col = jnp.tile(jnp.arange(GRID_W), rows)
    inv_freq = ROPE_THETA ** (-jnp.arange(quarter, dtype=F32) / quarter)
    ang = jnp.stack([row, col], axis=-1).astype(F32)[:, :, None] * inv_freq
    ang = jnp.broadcast_to(ang[:, :, None, :], (n, 2, 2, quarter)).reshape(n, dim)
    sign = _rot_sign(dim)

    def place(t, fill):
        t = jnp.concatenate([jnp.full((n, lead), fill, F32), t], axis=1)
        t = jnp.tile(t, (1, width // (lead + dim)))
        return jnp.concatenate([t, jnp.full((n, width - t.shape[1]), fill, F32)], axis=1)

    return place(jnp.cos(ang), 1.0), place(jnp.sin(ang) * sign, 0.0)


def _identity_tables(n, width):
    return jnp.ones((n, width), F32), jnp.zeros((n, width), F32)


def _pad_heads(w, heads, width, lead=0):
    shape = w.shape[:-1]
    w = w.reshape(shape + (heads, width))
    w = jnp.pad(w, [(0, 0)] * len(shape) + [(0, 0), (lead, LANES - lead - width)])
    return w.reshape(shape + (heads * LANES,))


def _split_in(w):
    sizes = (MLA_Q_RANK, MLA_KV_RANK, MLA_ROPE, LRU_WIDTH, LRU_WIDTH, SWA_QW, SWA_KW, SWA_KW)
    parts, start = [], 0
    for s in sizes:
        parts.append(w[..., start:start + s])
        start += s
    return parts


def _prepare_weights(norm1_g, w_in, q_a_g, w_uq, kv_a_g, w_ukv, mla_q_g, mla_k_g, conv_w, conv_b,
                     lru_gate_w, lru_gate_b, lru_lambda, swa_q_g, swa_k_g, group_g, w_out,
                     norm2_g, w_ff1, w_ff2):
    depth = w_in.shape[0]
    row = lambda v: v[:, None, :]
    cq, ckv, kr, lx, lg, sq, sk, sv = _split_in(w_in)
    sq = _pair_heads(sq, -1)
    w_in_p = jnp.concatenate(
        [cq, ckv, _pad_heads(kr, 1, MLA_ROPE, MLA_NOPE), _pad_heads(_rot_partner(kr, MLA_ROPE), 1, MLA_ROPE, MLA_NOPE),
         sq, _rot_partner(sq, SWA_HEAD_DIM), sk, _rot_partner(sk, SWA_HEAD_DIM), sv, lx, lg], axis=-1)
    uq = w_uq.reshape(depth, MLA_Q_RANK, MLA_HEADS, MLA_QK)
    uq_rot = _rot_partner(uq[..., MLA_NOPE:], MLA_ROPE).reshape(depth, MLA_Q_RANK, MLA_HEADS * MLA_ROPE)
    w_uq_p = jnp.concatenate([_pad_heads(w_uq, MLA_HEADS, MLA_QK),
                              _pad_heads(uq_rot, MLA_HEADS, MLA_ROPE, MLA_NOPE)], axis=-1)
    kv = w_ukv.reshape(depth, MLA_KV_RANK, MLA_HEADS, MLA_NOPE + MLA_V)
    w_k = _pad_heads(kv[..., :MLA_NOPE].reshape(depth, MLA_KV_RANK, MLA_HEADS * MLA_NOPE), MLA_HEADS, MLA_NOPE)
    w_vt = jnp.swapaxes(kv[..., MLA_NOPE:].reshape(depth, MLA_KV_RANK, MLA_VW), 1, 2)
    nblk = lru_gate_w.shape[3]
    per_group = LANES // LRU_BLOCK_DIM
    groups = nblk // per_group
    eye = jnp.eye(per_group, dtype=F32)
    gw = lru_gate_w.reshape(depth, 2, 2, groups, per_group, LRU_BLOCK_DIM, LRU_BLOCK_DIM)
    w_gate = 0.5 * jnp.einsum("dzgpncm,nk->dpnczgkm", gw, eye).reshape(depth, groups, LANES, 4 * LANES)
    b_gate = 0.5 * lru_gate_b.reshape(depth, 2, 2, groups, LANES).transpose(0, 3, 1, 2, 4).reshape(
        depth, groups, 1, 4 * LANES)
    gq = mla_q_g * (MLA_QK ** -0.5 * LOG2_E)
    gqs = swa_q_g * (SWA_HEAD_DIM ** -0.5 * LOG2_E)
    rope_gain = lambda g: _pad_heads(row(_rot_partner(g[:, MLA_NOPE:], MLA_ROPE)), 1, MLA_ROPE, MLA_NOPE)
    pair = lambda g: row(jnp.tile(g, (1, LANES // SWA_HEAD_DIM)))
    c0 = 2 * GROUP_WIDTH
    gg = jnp.concatenate([group_g[:, :c0], _pair_heads(group_g[:, c0:], 1)], axis=1)
    w_o = jnp.concatenate([w_out[:, :c0], _pair_heads(w_out[:, c0:], 1)], axis=1)
    return {
        "n1g": row(norm1_g), "w_in": w_in_p.astype(BF16), "qag": row(q_a_g),
        "w_uq": w_uq_p.astype(BF16), "kvag": row(kv_a_g),
        "w_uk": w_k.astype(BF16), "w_vt": w_vt.astype(BF16),
        "gq": _pad_heads(row(gq), 1, MLA_QK), "gq_rot": rope_gain(gq),
        "gk": _pad_heads(row(mla_k_g), 1, MLA_QK), "gk_rot": rope_gain(mla_k_g),
        "gqs": pair(gqs), "gqs_rot": pair(_rot_partner(gqs, SWA_HEAD_DIM)),
        "gks": pair(swa_k_g), "gks_rot": pair(_rot_partner(swa_k_g, SWA_HEAD_DIM)),
        "conv_w": conv_w, "conv_b": row(conv_b), "w_gate": w_gate.astype(BF16), "b_gate": b_gate,
        "lam": lru_lambda,
        "gg": row(gg), "w_out": w_o.astype(BF16), "n2g": row(norm2_g),
        "w_ff1": w_ff1.astype(BF16), "w_ff2": w_ff2.astype(BF16),
    }


def kernel(x, c, ctx, c_ctx, w_mod, b_mod, norm1_g, w_in, q_a_g, w_uq, kv_a_g, w_ukv, mla_q_g, mla_k_g, conv_w, conv_b, lru_gate_w, lru_gate_b, lru_lambda, swa_q_g, swa_k_g, swa_sink, group_g, w_out, norm2_g, w_ff1, w_ff2):
    b, l, d = x.shape
    lc = ctx.shape[1]
    depth = w_mod.shape[0]
    rows = l // GRID_W

    n_rows = -(-(b + 1) // SUBLANES) * SUBLANES
    cc = jnp.concatenate([c, c_ctx[None], jnp.zeros((n_rows - b - 1, d), F32)], axis=0)
    mod = _modulation(cc, w_mod, b_mod).reshape(depth, n_rows, 1, N_MOD * d)

    rope_m = _rope_tables(rows, MLA_ROPE, MLA_NOPE, LANES)
    rope_s = _rope_tables(rows, SWA_HEAD_DIM, 0, LANES)
    no_rope = _identity_tables(lc, LANES)
    wts = _prepare_weights(norm1_g, w_in, q_a_g, w_uq, kv_a_g, w_ukv, mla_q_g, mla_k_g, conv_w, conv_b,
                           lru_gate_w, lru_gate_b, lru_lambda, swa_q_g, swa_k_g, group_g, w_out,
                           norm2_g, w_ff1, w_ff2)

    xc = ctx
    for layer in range(depth):
        last = layer == depth - 1
        qm, km, vt, lx, lg, qs, ks, vts = _premix(x, mod, None, layer, wts, rope_m, rope_s)
        qm_c, km_c, vt_c, lx_c, lg_c, qs_c, ks_c, vts_c = _premix(xc, mod, b, layer, wts, no_rope, no_rope)

        o_b, o_b_c = _rglru(lx_c, lx, lg_c, lg, layer, wts)
        o_a = _mla_attention(qm, km_c, vt_c, km, vt)
        o_c = _swa_attention(qs, ks_c, vts_c, swa_sink, layer, ks, vts)
        x = _postmix(x, o_a, o_b, o_c, mod, None, layer, wts)
        if not last:
            o_a_c = _mla_attention(qm_c, km_c, vt_c)
            o_c_c = _swa_attention(qs_c, ks_c, vts_c, swa_sink, layer)
            xc = _postmix(xc, o_a_c, o_b_c, o_c_c, mod, b, layer, wts)
    return x
```

```python
import functools

import numpy as np
import jax
import jax.numpy as jnp
from jax import lax
from jax.experimental import pallas as pl
from jax.experimental.pallas import tpu as pltpu

F32 = jnp.float32
BF16 = jnp.bfloat16

GRID_W = 64
WINDOW = 128
ROPE_THETA = 10000.0
EPS = 1e-6
NEG_INF = -1e30
N_MOD = 6
MLA_HEADS = 8
MLA_NOPE = 64
MLA_ROPE = 32
MLA_QK = MLA_NOPE + MLA_ROPE
MLA_V = 64
MLA_Q_RANK = 256
MLA_KV_RANK = 128
LRU_WIDTH = 512
LRU_BLOCK_DIM = 64
LRU_C = 8.0
CONV_W = 4
SWA_HEADS = 8
SWA_KV_HEADS = 2
SWA_GROUP = SWA_HEADS // SWA_KV_HEADS
SWA_HEAD_DIM = 64
GROUP_WIDTH = 512

LANES = 128
SUBLANES = 8
V7X_VMEM_BYTES = 64 * 1024 * 1024
VMEM_LIMIT_BYTES = V7X_VMEM_BYTES - 8 * 1024 * 1024

MLA_QW = MLA_HEADS * LANES
MLA_VW = MLA_HEADS * MLA_V
MLA_HEADS_PER_STEP = 8
MLA_KEY_CHUNK = 512
LOG2_E = 1.4426950408889634
SWA_QW = SWA_HEADS * SWA_HEAD_DIM
SWA_KW = SWA_KV_HEADS * SWA_HEAD_DIM
SWA_PAIRS = SWA_QW // LANES
SWA_Q_BLOCKS_PER_STEP = 8
FF_CHUNK = 1024
PREMIX_ROWS = 512
LRU_GROUPS_PER_STEP = 2

OFF_CQ = 0
OFF_CKV = OFF_CQ + MLA_Q_RANK
OFF_KR = OFF_CKV + MLA_KV_RANK
SECTION_2 = OFF_KR + LANES
OFF_KRR = SECTION_2
OFF_SQ = OFF_KRR + LANES
OFF_SQR = OFF_SQ + SWA_QW
OFF_SK = OFF_SQR + SWA_QW
OFF_SKR = OFF_SK + SWA_KW
OFF_SV = OFF_SKR + SWA_KW
OFF_LX = OFF_SV + SWA_KW
OFF_LG = OFF_LX + LRU_WIDTH
W_IN_PAD = OFF_LG + LRU_WIDTH


def _dot(a, b):
    return jnp.dot(a, b, preferred_element_type=F32)


def _dot_nt(a, b):
    return lax.dot_general(a, b, (((1,), (1,)), ((), ())), preferred_element_type=F32)


def _rms(x):
    return x * lax.rsqrt(jnp.mean(x * x, axis=-1, keepdims=True) + EPS)


def _params(*sem, flags=None):
    return pltpu.CompilerParams(dimension_semantics=sem, vmem_limit_bytes=VMEM_LIMIT_BYTES, flags=flags)


def _layer_spec(arr, layer):
    index = (layer,) + (0,) * (arr.ndim - 1)
    return pl.BlockSpec((None,) + arr.shape[1:], lambda *_: index)


def _mod_spec(mod, layer, shared_row):
    if shared_row is None:
        return pl.BlockSpec((None, 1, 1, mod.shape[-1]), lambda bi, i: (layer, bi, 0, 0))
    return pl.BlockSpec((None, 1, 1, mod.shape[-1]), lambda bi, i: (layer, shared_row, 0, 0))


def _mod_kernel(c_ref, w_ref, b_ref, o_ref):
    c = c_ref[...]
    a = c * jax.nn.sigmoid(c)
    w = w_ref[0]
    a_hi = a.astype(BF16)
    a_lo = (a - a_hi.astype(F32)).astype(BF16)
    w_hi = w.astype(BF16)
    w_lo = (w - w_hi.astype(F32)).astype(BF16)
    o_ref[0] = _dot(a_hi, w_hi) + _dot(a_hi, w_lo) + _dot(a_lo, w_hi) + b_ref[0]


def _modulation(cc, w_mod, b_mod):
    depth, d, n = w_mod.shape
    rows = cc.shape[0]
    tn = n // 4
    return pl.pallas_call(
        _mod_kernel,
        grid=(depth, n // tn),
        in_specs=[
            pl.BlockSpec((rows, d), lambda l, j: (0, 0)),
            pl.BlockSpec((1, d, tn), lambda l, j: (l, 0, j)),
            pl.BlockSpec((1, 1, tn), lambda l, j: (l, 0, j)),
        ],
        out_specs=pl.BlockSpec((1, rows, tn), lambda l, j: (l, 0, j)),
        out_shape=jax.ShapeDtypeStruct((depth, rows, n), F32),
        compiler_params=_params("arbitrary", "arbitrary"),
        name="modulation",
    )(cc, w_mod, b_mod.reshape(depth, 1, n))


def _premix_kernel(x_ref, mod_ref, n1g_ref, win_ref, qag_ref, wuq_ref, kvag_ref, wuk_ref, wvt_ref,
                   gq_ref, gqr_ref, gk_ref, gkr_ref, gqs_ref, gqsr_ref, gks_ref, gksr_ref,
                   cm_ref, sm_ref, cs_ref, ss_ref,
                   qm_ref, km_ref, vt_ref, lx_ref, lg_ref, qs_ref, ks_ref, vts_ref):
    d = x_ref.shape[-1]
    tm = x_ref.shape[1]
    shift = mod_ref[0, :, 0:d]
    scale = mod_ref[0, :, d:2 * d]
    n = PREMIX_ROWS if tm % PREMIX_ROWS == 0 else tm
    for rows in (slice(r, r + n) for r in range(0, tm, n)):
        _premix_rows(rows, n, shift, scale, x_ref, n1g_ref, win_ref, qag_ref, wuq_ref, kvag_ref, wuk_ref, wvt_ref,
                     gq_ref, gqr_ref, gk_ref, gkr_ref, gqs_ref, gqsr_ref, gks_ref, gksr_ref,
                     cm_ref, sm_ref, cs_ref, ss_ref,
                     qm_ref, km_ref, vt_ref, lx_ref, lg_ref, qs_ref, ks_ref, vts_ref)


def _premix_rows(rows, n, shift, scale, x_ref, n1g_ref, win_ref, qag_ref, wuq_ref, kvag_ref, wuk_ref, wvt_ref,
                 gq_ref, gqr_ref, gk_ref, gkr_ref, gqs_ref, gqsr_ref, gks_ref, gksr_ref,
                 cm_ref, sm_ref, cs_ref, ss_ref,
                 qm_ref, km_ref, vt_ref, lx_ref, lg_ref, qs_ref, ks_ref, vts_ref):
    h = (_rms(x_ref[0, rows, :]) * n1g_ref[...] * (1.0 + scale) + shift).astype(BF16)

    cm, sm = cm_ref[rows, :], sm_ref[rows, :]
    cs, ss = cs_ref[rows, :], ss_ref[rows, :]

    p1 = _dot(h, win_ref[:, 0:SECTION_2])
    qn = (_rms(p1[:, OFF_CQ:OFF_CQ + MLA_Q_RANK]) * qag_ref[...]).astype(BF16)
    kvn = (_rms(p1[:, OFF_CKV:OFF_CKV + MLA_KV_RANK]) * kvag_ref[...]).astype(BF16)
    kr = p1[:, OFF_KR:OFF_KR + LANES]
    ss_kr = jnp.sum(kr * kr, axis=-1, keepdims=True)

    p2 = _dot(h, win_ref[:, SECTION_2:OFF_LX])
    qu = _dot(qn, wuq_ref[...])
    kvu = _dot(kvn, wuk_ref[...])
    vt_ref[0, :, rows] = _dot_nt(wvt_ref[...], kvn).astype(BF16)
    ga_k = gk_ref[...] * cm
    shared = kr * ga_k + p2[:, 0:LANES] * (gkr_ref[...] * sm)
    lo = lax.broadcasted_iota(jnp.int32, (n, LANES), 1) < SWA_HEAD_DIM

    def pair_norm_rope(off, off_rot, ga, gb):
        x1 = p2[:, off - SECTION_2:off - SECTION_2 + LANES]
        x2 = p2[:, off_rot - SECTION_2:off_rot - SECTION_2 + LANES]
        sq = x1 * x1
        s_lo = jnp.sum(jnp.where(lo, sq, 0.0), axis=-1, keepdims=True)
        s_hi = jnp.sum(jnp.where(lo, 0.0, sq), axis=-1, keepdims=True)
        inv = 1.0 / SWA_HEAD_DIM
        r = jnp.where(lo, lax.rsqrt(s_lo * inv + EPS), lax.rsqrt(s_hi * inv + EPS))
        return (r * (x1 * ga + x2 * gb)).astype(BF16)

    ga_s, gb_s = gqs_ref[...] * cs, gqsr_ref[...] * ss
    for g in range(SWA_PAIRS):
        qs_ref[0, rows, g * LANES:(g + 1) * LANES] = pair_norm_rope(OFF_SQ + g * LANES, OFF_SQR + g * LANES, ga_s, gb_s)
    ks_ref[0, rows, :] = pair_norm_rope(OFF_SK, OFF_SKR, gks_ref[...] * cs, gksr_ref[...] * ss)
    vts_ref[0, :, rows] = p2[:, OFF_SV - SECTION_2:OFF_SV - SECTION_2 + SWA_KW].T.astype(BF16)

    p3 = _dot(h, win_ref[:, OFF_LX:W_IN_PAD])
    ga_q, gb_q = gq_ref[...] * cm, gqr_ref[...] * sm
    for hd in range(MLA_HEADS):
        x1 = qu[:, hd * LANES:(hd + 1) * LANES]
        x2 = qu[:, MLA_QW + hd * LANES:MLA_QW + (hd + 1) * LANES]
        r = lax.rsqrt(jnp.sum(x1 * x1, axis=-1, keepdims=True) * (1.0 / MLA_QK) + EPS)
        qm_ref[0, rows, hd * LANES:(hd + 1) * LANES] = (r * (x1 * ga_q + x2 * gb_q)).astype(BF16)
    for hd in range(MLA_HEADS):
        x1 = kvu[:, hd * LANES:(hd + 1) * LANES]
        r = lax.rsqrt((jnp.sum(x1 * x1, axis=-1, keepdims=True) + ss_kr) * (1.0 / MLA_QK) + EPS)
        km_ref[0, rows, hd * LANES:(hd + 1) * LANES] = (r * (x1 * ga_k + shared)).astype(BF16)
    lx_ref[0, rows, :] = p3[:, 0:LRU_WIDTH]
    lg_ref[0, rows, :] = p3[:, LRU_WIDTH:2 * LRU_WIDTH]


def _premix(x, mod, shared_row, layer, wts, rope_m, rope_s):
    b, l, d = x.shape
    tm = min(l, 1024)
    consts = [wts[k] for k in ("n1g", "w_in", "qag", "w_uq", "kvag", "w_uk", "w_vt",
                               "gq", "gq_rot", "gk", "gk_rot", "gqs", "gqs_rot", "gks", "gks_rot")]
    tables = list(rope_m) + list(rope_s)
    tok = lambda w, dt: (pl.BlockSpec((1, tm, w), lambda bi, i: (bi, i, 0)), jax.ShapeDtypeStruct((b, l, w), dt))
    tr = lambda w: (pl.BlockSpec((1, w, tm), lambda bi, i: (bi, 0, i)), jax.ShapeDtypeStruct((b, w, l), BF16))
    outs = [tok(MLA_QW, BF16), tok(MLA_QW, BF16), tr(MLA_VW), tok(LRU_WIDTH, F32), tok(LRU_WIDTH, F32),
            tok(SWA_QW, BF16), tok(SWA_KW, BF16), tr(SWA_KW)]
    return pl.pallas_call(
        _premix_kernel,
        grid=(b, l // tm),
        in_specs=[tok(d, F32)[0], _mod_spec(mod, layer, shared_row)]
        + [_layer_spec(c, layer) for c in consts]
        + [pl.BlockSpec((tm, LANES), lambda bi, i: (i, 0)) for _ in tables],
        out_specs=[o[0] for o in outs],
        out_shape=[o[1] for o in outs],
        compiler_params=_params("arbitrary", "arbitrary"),
        name="premix",
    )(x, mod, *consts, *tables)


def _gelu_tanh(x):
    c = 0.7978845608028654
    half_x = 0.5 * x
    return half_x * jnp.tanh(x * (c + (c * 0.044715) * (x * x))) + half_x


LRU_SCRATCH_PER_GROUP = 5


def _lru_kernel(lxc_ref, lxl_ref, lgc_ref, lgl_ref, cw_ref, cb_ref, wg_ref, bg_ref, lam_ref,
                obl_ref, obc_ref, *scratch):
    lc = lxc_ref.shape[1]
    ll = lxl_ref.shape[1]
    groups = [scratch[i:i + LRU_SCRATCH_PER_GROUP] for i in range(0, len(scratch), LRU_SCRATCH_PER_GROUP)]
    pitch = scratch[0].shape[0] // SUBLANES
    tail = scratch[0].shape[0] - (lc + ll)
    assert tail > 0
    assert pitch % 2 == 0

    def coefficients(gi, x, xs):
        lanes = slice(gi * LANES, (gi + 1) * LANES)
        cw = cw_ref[:, lanes]
        nlam = -lam_ref[:, lanes]
        softplus = jnp.maximum(nlam, 0.0) + jnp.log1p(jnp.exp(-jnp.abs(nlam)))
        decay = (-0.5 * LRU_C * LOG2_E) * softplus
        t = x.shape[0]
        pad = jnp.zeros((SUBLANES, LANES), F32)
        xs[0:SUBLANES, :] = pad
        xs[SUBLANES:SUBLANES + t, :] = x
        xs[SUBLANES + t:2 * SUBLANES + t, :] = pad
        xm2, xm1, xp1 = (xs[SUBLANES + o:SUBLANES + o + t, :] for o in (-2, -1, 1))
        xc = cw[0:1] * xm2 + cw[1:2] * xm1 + cw[2:3] * x + cw[3:4] * xp1 + cb_ref[:, lanes]
        th = jnp.tanh(_dot(xc.astype(BF16), wg_ref[gi]) + bg_ref[gi])
        half_x = 0.5 * xc
        out = []
        for z in range(2):
            t_r = th[:, 2 * z * LANES:(2 * z + 1) * LANES]
            t_i = th[:, (2 * z + 1) * LANES:(2 * z + 2) * LANES]
            a = jnp.exp2(decay[z:z + 1] * t_r + decay[z:z + 1])
            y = 1.0 - a * a
            root = y * lax.rsqrt(jnp.maximum(y, jnp.finfo(F32).tiny))
            out.append((a, root * (half_x * t_i + half_x)))
        return out

    for gi, (af, uf, ab, ub, xs) in enumerate(groups):
        for x_ref, f_off, b_off in ((lxc_ref, 0, ll), (lxl_ref, lc, 0)):
            t = x_ref.shape[1]
            (a_f, u_f), (a_b, u_b) = coefficients(gi, x_ref[0, :, gi * LANES:(gi + 1) * LANES], xs)
            af[f_off:f_off + t, :] = a_f
            uf[f_off:f_off + t, :] = u_f
            ab[b_off:b_off + t, :] = a_b
            ub[b_off:b_off + t, :] = u_b
        for a_ref, u_ref in ((af, uf), (ab, ub)):
            a_ref[lc + ll:, :] = jnp.ones((tail, LANES), F32)
            u_ref[lc + ll:, :] = jnp.zeros((tail, LANES), F32)

    def streams(i):
        return pl.ds(i, SUBLANES, stride=pitch)

    def two_steps(a_ref, u_ref, i0, i1, h, p):
        a0, u0 = a_ref[streams(i0), :], u_ref[streams(i0), :]
        a1, u1 = a_ref[streams(i1), :], u_ref[streams(i1), :]
        a01 = a1 * a0
        u_ref[streams(i0), :] = a0 * h + u0
        a_ref[streams(i0), :] = a0 * p
        h = a01 * h + (a1 * u0 + u1)
        p = a01 * p
        u_ref[streams(i1), :] = h
        a_ref[streams(i1), :] = p
        return h, p

    def local_scan(i, state):
        out = []
        for (af, uf, ab, ub, _), (h_f, p_f, h_b, p_b) in zip(groups, state):
            h_f, p_f = two_steps(af, uf, 2 * i, 2 * i + 1, h_f, p_f)
            h_b, p_b = two_steps(ab, ub, pitch - 1 - 2 * i, pitch - 2 - 2 * i, h_b, p_b)
            out.append((h_f, p_f, h_b, p_b))
        return tuple(out)

    zeros8 = jnp.zeros((SUBLANES, LANES), F32)
    ones8 = jnp.ones((SUBLANES, LANES), F32)
    finals = lax.fori_loop(0, pitch // 2, local_scan, ((zeros8, ones8, zeros8, ones8),) * len(groups), unroll=2)

    sub = lax.broadcasted_iota(jnp.int32, (SUBLANES, LANES), 0)
    carries = []
    for h_f, p_f, h_b, p_b in finals:
        c_f = zeros8
        c_b = zeros8
        row_f = jnp.zeros((1, LANES), F32)
        row_b = jnp.zeros((1, LANES), F32)
        for s in range(1, SUBLANES):
            row_f = h_f[s - 1:s] + p_f[s - 1:s] * row_f
            c_f = jnp.where(sub == s, row_f, c_f)
            sb = SUBLANES - 1 - s
            row_b = h_b[sb + 1:sb + 2] + p_b[sb + 1:sb + 2] * row_b
            c_b = jnp.where(sub == sb, row_b, c_b)
        carries.append((c_f, c_b))

    def add_carry(i, _):
        for (af, uf, ab, ub, _), (c_f, c_b) in zip(groups, carries):
            uf[streams(i), :] = uf[streams(i), :] + af[streams(i), :] * c_f
            ub[streams(i), :] = ub[streams(i), :] + ab[streams(i), :] * c_b
        return 0

    lax.fori_loop(0, pitch, add_carry, 0, unroll=4)

    for gi, (_, uf, _, ub, _) in enumerate(groups):
        lanes = slice(gi * LANES, (gi + 1) * LANES)
        obl_ref[0, :, lanes] = (uf[lc:lc + ll, :] + ub[0:ll, :]) * _gelu_tanh(lgl_ref[0, :, lanes])
        obc_ref[0, :, lanes] = (uf[0:lc, :] + ub[ll:ll + lc, :]) * _gelu_tanh(lgc_ref[0, :, lanes])


def _rglru(lx_c, lx_l, lg_c, lg_l, layer, wts):
    b, lc, w = lx_c.shape
    ll = lx_l.shape[1]
    gps = LRU_GROUPS_PER_STEP
    bw = gps * LANES
    pitch = -(-(lc + ll) // SUBLANES)
    pitch += (SUBLANES // 2 - pitch) % SUBLANES
    seq = lambda t: pl.BlockSpec((1, t, bw), lambda bi, g: (bi, 0, g))
    group_scratch = [pltpu.VMEM((SUBLANES * pitch, LANES), F32) for _ in range(LRU_SCRATCH_PER_GROUP - 1)]
    group_scratch.append(pltpu.VMEM((max(lc, ll) + 2 * SUBLANES, LANES), F32))
    return pl.pallas_call(
        _lru_kernel,
        grid=(b, w // bw),
        in_specs=[seq(lc), seq(ll), seq(lc), seq(ll),
                  pl.BlockSpec((None, CONV_W, bw), lambda bi, g: (layer, 0, g)),
                  pl.BlockSpec((None, 1, bw), lambda bi, g: (layer, 0, g)),
                  pl.BlockSpec((None, gps, LANES, 4 * LANES), lambda bi, g: (layer, g, 0, 0)),
                  pl.BlockSpec((None, gps, 1, 4 * LANES), lambda bi, g: (layer, g, 0, 0)),
                  pl.BlockSpec((None, 2, bw), lambda bi, g: (layer, 0, g))],
        out_specs=[seq(ll), seq(lc)],
        out_shape=[jax.ShapeDtypeStruct((b, ll, w), F32), jax.ShapeDtypeStruct((b, lc, w), F32)],
        scratch_shapes=group_scratch * gps,
        compiler_params=_params("arbitrary", "arbitrary"),
        name="rglru",
    )(lx_c, lx_l, lg_c, lg_l, wts["conv_w"], wts["conv_b"], wts["w_gate"], wts["b_gate"], wts["lam"])


def _mla_kernel(*refs, with_latent):
    if with_latent:
        q_ref, kc_ref, vtc_ref, kl_ref, vtl_ref, o_ref, s_buf = refs
        sources = ((kc_ref, vtc_ref), (kl_ref, vtl_ref))
    else:
        q_ref, kc_ref, vtc_ref, o_ref, s_buf = refs
        sources = ((kc_ref, vtc_ref),)
    tq = q_ref.shape[1]
    chunks, row = [], 0
    for k_ref, vt_ref in sources:
        n = k_ref.shape[1]
        for k0 in range(0, n, MLA_KEY_CHUNK):
            kn = min(MLA_KEY_CHUNK, n - k0)
            chunks.append((k_ref, vt_ref, k0, kn, row))
            row += kn

    def add(acc, x, op=jnp.add):
        return x if acc is None else op(acc, x)

    def score_chunk(hd, chunk, m8):
        k_ref, _, k0, kn, r0 = chunk
        sl = slice(hd * LANES, (hd + 1) * LANES)
        s = _dot_nt(k_ref[0, k0:k0 + kn, sl], q_ref[0, :, sl])
        s_buf[hd % 2, r0:r0 + kn, :] = s
        return add(m8, jnp.max(s.reshape(kn // SUBLANES, SUBLANES, tq), axis=0), jnp.maximum)

    def attend_chunk(hd, chunk, m, l8, o_t):
        _, vt_ref, k0, kn, r0 = chunk
        p = jnp.exp2(s_buf[hd % 2, r0:r0 + kn, :] - m)
        l8 = add(l8, jnp.sum(p.reshape(kn // SUBLANES, SUBLANES, tq), axis=0))
        o_t = add(o_t, _dot(vt_ref[0, hd * MLA_V:(hd + 1) * MLA_V, k0:k0 + kn], p.astype(BF16)))
        return l8, o_t

    outs = []
    m8 = None
    for chunk in chunks:
        m8 = score_chunk(0, chunk, m8)
    for hd in range(MLA_HEADS_PER_STEP):
        m = jnp.max(m8, axis=0, keepdims=True)
        m8, l8, o_t = None, None, None
        if hd + 1 < MLA_HEADS_PER_STEP:
            for chunk in chunks:
                m8 = score_chunk(hd + 1, chunk, m8)
        for chunk in chunks:
            l8, o_t = attend_chunk(hd, chunk, m, l8, o_t)
        outs.append(o_t * (1.0 / jnp.sum(l8, axis=0, keepdims=True)))
    o_ref[0] = jnp.concatenate(outs, axis=0).T.astype(o_ref.dtype)


def _mla_attention(q, k_c, vt_c, k_l=None, vt_l=None):
    b, lq, _ = q.shape
    lc = k_c.shape[1]
    with_latent = k_l is not None
    tq = min(lq, 512)
    qw = MLA_HEADS_PER_STEP * LANES
    vw = MLA_HEADS_PER_STEP * MLA_V
    nq = MLA_HEADS // MLA_HEADS_PER_STEP
    keys = lambda t: pl.BlockSpec((1, t, qw), lambda bi, hq, i: (bi, 0, hq))
    vals = lambda t: pl.BlockSpec((1, vw, t), lambda bi, hq, i: (bi, hq, 0))
    in_specs = [pl.BlockSpec((1, tq, qw), lambda bi, hq, i: (bi, i, hq)), keys(lc), vals(lc)]
    args = [q, k_c, vt_c]
    if with_latent:
        in_specs += [keys(k_l.shape[1]), vals(k_l.shape[1])]
        args += [k_l, vt_l]
    return pl.pallas_call(
        functools.partial(_mla_kernel, with_latent=with_latent),
        grid=(b, nq, lq // tq),
        in_specs=in_specs,
        out_specs=pl.BlockSpec((1, tq, vw), lambda bi, hq, i: (bi, i, hq)),
        out_shape=jax.ShapeDtypeStruct((b, lq, MLA_VW), BF16),
        scratch_shapes=[pltpu.VMEM((2, lc + (k_l.shape[1] if with_latent else 0), tq), F32)],
        compiler_params=_params("arbitrary", "arbitrary", "arbitrary"),
        name="mla_attention" if with_latent else "mla_attention_ctx",
    )(*args)


def _swa_kernel(*refs, with_window, layer):
    if with_window:
        sink_ref, q_ref, kc_ref, vtc_ref, kl_ref, vtl_ref, o_ref = refs
    else:
        sink_ref, q_ref, kc_ref, vtc_ref, o_ref = refs
    bq = WINDOW
    cols = SWA_PAIRS * bq
    kc = kc_ref[0]
    lo = lax.broadcasted_iota(jnp.int32, (cols, LANES), 1) < SWA_HEAD_DIM
    seg = lax.broadcasted_iota(jnp.int32, (1, cols), 1) // bq

    span = bq + 2 * WINDOW
    sinks = []
    for kvh in range(SWA_KV_HEADS):
        sink = jnp.zeros((1, cols), F32)
        for g in range(SWA_PAIRS):
            sink = jnp.where(seg == g, sink_ref[layer, kvh * SWA_GROUP + g] * LOG2_E, sink)
        sinks.append(sink)

    def scores(sb):
        r0 = sb * bq
        q = jnp.concatenate([q_ref[0, r0:r0 + bq, g * LANES:(g + 1) * LANES] for g in range(SWA_PAIRS)],
                            axis=0).astype(F32)
        w0 = None
        if with_window:
            q0 = pl.program_id(1) * q_ref.shape[1] + r0
            w0 = pl.multiple_of(jnp.clip(q0 - WINDOW, 0, kl_ref.shape[1] - span), LANES)
            kw = kl_ref[0, pl.ds(w0, span), :]
            kj = w0 + lax.broadcasted_iota(jnp.int32, (span, cols), 0)
            qi = q0 + (lax.broadcasted_iota(jnp.int32, (span, cols), 1) & (bq - 1))
            band = jnp.abs(qi - kj) <= WINDOW
        out = []
        for kvh in range(SWA_KV_HEADS):
            qh = (jnp.where(lo, 0.0, q) if kvh else jnp.where(lo, q, 0.0)).astype(BF16)
            s = [_dot_nt(kc, qh)]
            if with_window:
                s.append(jnp.where(band, _dot_nt(kw, qh), NEG_INF))
            out.append(s)
        return w0, out

    def attend(sb, w0, blk_scores):
        o_t = []
        for kvh, s in enumerate(blk_scores):
            sink = sinks[kvh]
            m = functools.reduce(jnp.maximum, [jnp.max(x, axis=0, keepdims=True) for x in s] + [sink])
            p = [jnp.exp2(x - m) for x in s]
            denom = sum(jnp.sum(x, axis=0, keepdims=True) for x in p) + jnp.exp2(sink - m)
            vs = slice(kvh * SWA_HEAD_DIM, (kvh + 1) * SWA_HEAD_DIM)
            o = _dot(vtc_ref[0, vs, :], p[0].astype(BF16))
            if with_window:
                o = o + _dot(vtl_ref[0, vs, pl.ds(w0, span)], p[1].astype(BF16))
            o_t.append(o * (1.0 / denom))
        out = jnp.concatenate(o_t, axis=0).T.astype(o_ref.dtype)
        for g in range(SWA_PAIRS):
            o_ref[0, sb * bq:(sb + 1) * bq, g * LANES:(g + 1) * LANES] = out[g * bq:(g + 1) * bq]

    n_blocks = q_ref.shape[1] // bq
    nxt = scores(0)
    for sb in range(n_blocks):
        cur = nxt
        if sb + 1 < n_blocks:
            nxt = scores(sb + 1)
        attend(sb, *cur)


def _swa_attention(q, k_c, vt_c, sink, layer, k_l=None, vt_l=None):
    b, lq, _ = q.shape
    lc = k_c.shape[1]
    with_window = k_l is not None
    tq = min(lq, SWA_Q_BLOCKS_PER_STEP * WINDOW)
    keys = lambda t: pl.BlockSpec((1, t, SWA_KW), lambda bi, i: (bi, 0, 0))
    vals = lambda t: pl.BlockSpec((1, SWA_KW, t), lambda bi, i: (bi, 0, 0))
    in_specs = [pl.BlockSpec(memory_space=pltpu.SMEM),
                pl.BlockSpec((1, tq, SWA_QW), lambda bi, i: (bi, i, 0)), keys(lc), vals(lc)]
    args = [sink, q, k_c, vt_c]
    if with_window:
        in_specs += [keys(k_l.shape[1]), vals(k_l.shape[1])]
        args += [k_l, vt_l]
    return pl.pallas_call(
        functools.partial(_swa_kernel, with_window=with_window, layer=layer),
        grid=(b, lq // tq),
        in_specs=in_specs,
        out_specs=pl.BlockSpec((1, tq, SWA_QW), lambda bi, i: (bi, i, 0)),
        out_shape=jax.ShapeDtypeStruct((b, lq, SWA_QW), BF16),
        compiler_params=_params("arbitrary", "arbitrary"),
        name="swa_attention" if with_window else "swa_attention_ctx",
    )(*args)


def _postmix_kernel(x_ref, oa_ref, ob_ref, oc_ref, mod_ref, gg_ref, wout_ref, n2g_ref, w1_ref, w2_ref, o_ref):
    d = x_ref.shape[-1]
    gate1 = mod_ref[0, :, 2 * d:3 * d]
    shift2 = mod_ref[0, :, 3 * d:4 * d]
    scale2 = mod_ref[0, :, 4 * d:5 * d]
    gate2 = mod_ref[0, :, 5 * d:6 * d]
    tm = x_ref.shape[1]
    halves = [slice(r, r + tm // 2) for r in (0, tm // 2)]

    def mix(rows):
        y = None
        for gi, ref in enumerate((oa_ref, ob_ref, oc_ref)):
            sl = slice(gi * GROUP_WIDTH, (gi + 1) * GROUP_WIDTH)
            og = _rms(ref[0, rows, :].astype(F32)) * gg_ref[:, sl]
            part = _dot(og.astype(BF16), wout_ref[sl, :])
            y = part if y is None else y + part
        return y

    def residual_norm(rows, y):
        x1 = x_ref[0, rows, :] + gate1 * y
        return x1, (_rms(x1) * n2g_ref[...] * (1.0 + scale2) + shift2).astype(BF16)

    def mlp(h2):
        ff = None
        for j in range(w1_ref.shape[1] // FF_CHUNK):
            sl = slice(j * FF_CHUNK, (j + 1) * FF_CHUNK)
            hid = jnp.maximum(_dot(h2, w1_ref[:, sl]), 0.0)
            part = _dot((hid * hid).astype(BF16), w2_ref[sl, :])
            ff = part if ff is None else ff + part
        return ff

    y = [mix(rows) for rows in halves]
    for rows, y_half in zip(halves, y):
        x1, h2 = residual_norm(rows, y_half)
        o_ref[0, rows, :] = x1 + gate2 * mlp(h2)


def _postmix(x, o_a, o_b, o_c, mod, shared_row, layer, wts):
    b, l, d = x.shape
    tm = min(l, 512)
    tok = lambda w: pl.BlockSpec((1, tm, w), lambda bi, i: (bi, i, 0))
    consts = [wts["gg"], wts["w_out"], wts["n2g"], wts["w_ff1"], wts["w_ff2"]]
    return pl.pallas_call(
        _postmix_kernel,
        grid=(b, l // tm),
        in_specs=[tok(d), tok(GROUP_WIDTH), tok(GROUP_WIDTH), tok(GROUP_WIDTH), _mod_spec(mod, layer, shared_row)]
        + [_layer_spec(c, layer) for c in consts],
        out_specs=tok(d),
        out_shape=jax.ShapeDtypeStruct((b, l, d), F32),
        compiler_params=_params("arbitrary", "arbitrary"),
        name="postmix",
    )(x, o_a, o_b, o_c, mod, *consts)


def _rot_sign(dim):
    quarter = dim // 4
    return np.where((np.arange(dim) // quarter) % 2 == 0, -1.0, 1.0).astype(np.float32)


def _rot_partner(w, dim):
    return jnp.flip(w.reshape(w.shape[:-1] + (-1, 2, dim // 4)), axis=-2).reshape(w.shape)


def _pair_heads(w, axis):
    axis = axis % w.ndim
    shape = w.shape
    w = w.reshape(shape[:axis] + (SWA_KV_HEADS, SWA_PAIRS, SWA_HEAD_DIM) + shape[axis + 1:])
    return jnp.swapaxes(w, axis, axis + 1).reshape(shape)


def _rope_tables(rows, dim, lead, width):
    quarter = dim // 4
    n = rows * GRID_W
    row = jnp.repeat(jnp.arange(rows), GRID_W)
    col = jnp.tile(jnp.arange(GRID_W), rows)
    inv_freq = ROPE_THETA ** (-jnp.arange(quarter, dtype=F32) / quarter)
    ang = jnp.stack([row, col], axis=-1).astype(F32)[:, :, None] * inv_freq
    ang = jnp.broadcast_to(ang[:, :, None, :], (n, 2, 2, quarter)).reshape(n, dim)
    sign = _rot_sign(dim)

    def place(t, fill):
        t = jnp.concatenate([jnp.full((n, lead), fill, F32), t], axis=1)
        t = jnp.tile(t, (1, width // (lead + dim)))
        return jnp.concatenate([t, jnp.full((n, width - t.shape[1]), fill, F32)], axis=1)

    return place(jnp.cos(ang), 1.0), place(jnp.sin(ang) * sign, 0.0)


def _identity_tables(n, width):
    return jnp.ones((n, width), F32), jnp.zeros((n, width), F32)


def _pad_heads(w, heads, width, lead=0):
    shape = w.shape[:-1]
    w = w.reshape(shape + (heads, width))
    w = jnp.pad(w, [(0, 0)] * len(shape) + [(0, 0), (lead, LANES - lead - width)])
    return w.reshape(shape + (heads * LANES,))


def _split_in(w):
    sizes = (MLA_Q_RANK, MLA_KV_RANK, MLA_ROPE, LRU_WIDTH, LRU_WIDTH, SWA_QW, SWA_KW, SWA_KW)
    parts, start = [], 0
    for s in sizes:
        parts.append(w[..., start:start + s])
        start += s
    return parts


def _prepare_weights(norm1_g, w_in, q_a_g, w_uq, kv_a_g, w_ukv, mla_q_g, mla_k_g, conv_w, conv_b,
                     lru_gate_w, lru_gate_b, lru_lambda, swa_q_g, swa_k_g, group_g, w_out,
                     norm2_g, w_ff1, w_ff2):
    depth = w_in.shape[0]
    row = lambda v: v[:, None, :]
    cq, ckv, kr, lx, lg, sq, sk, sv = _split_in(w_in)
    sq = _pair_heads(sq, -1)
    w_in_p = jnp.concatenate(
        [cq, ckv, _pad_heads(kr, 1, MLA_ROPE, MLA_NOPE), _pad_heads(_rot_partner(kr, MLA_ROPE), 1, MLA_ROPE, MLA_NOPE),
         sq, _rot_partner(sq, SWA_HEAD_DIM), sk, _rot_partner(sk, SWA_HEAD_DIM), sv, lx, lg], axis=-1)
    uq = w_uq.reshape(depth, MLA_Q_RANK, MLA_HEADS, MLA_QK)
    uq_rot = _rot_partner(uq[..., MLA_NOPE:], MLA_ROPE).reshape(depth, MLA_Q_RANK, MLA_HEADS * MLA_ROPE)
    w_uq_p = jnp.concatenate([_pad_heads(w_uq, MLA_HEADS, MLA_QK),
                              _pad_heads(uq_rot, MLA_HEADS, MLA_ROPE, MLA_NOPE)], axis=-1)
    kv = w_ukv.reshape(depth, MLA_KV_RANK, MLA_HEADS, MLA_NOPE + MLA_V)
    w_k = _pad_heads(kv[..., :MLA_NOPE].reshape(depth, MLA_KV_RANK, MLA_HEADS * MLA_NOPE), MLA_HEADS, MLA_NOPE)
    w_vt = jnp.swapaxes(kv[..., MLA_NOPE:].reshape(depth, MLA_KV_RANK, MLA_VW), 1, 2)
    nblk = lru_gate_w.shape[3]
    per_group = LANES // LRU_BLOCK_DIM
    groups = nblk // per_group
    eye = jnp.eye(per_group, dtype=F32)
    gw = lru_gate_w.reshape(depth, 2, 2, groups, per_group, LRU_BLOCK_DIM, LRU_BLOCK_DIM)
    w_gate = 0.5 * jnp.einsum("dzgpncm,nk->dpnczgkm", gw, eye).reshape(depth, groups, LANES, 4 * LANES)
    b_gate = 0.5 * lru_gate_b.reshape(depth, 2, 2, groups, LANES).transpose(0, 3, 1, 2, 4).reshape(
        depth, groups, 1, 4 * LANES)
    gq = mla_q_g * (MLA_QK ** -0.5 * LOG2_E)
    gqs = swa_q_g * (SWA_HEAD_DIM ** -0.5 * LOG2_E)
    rope_gain = lambda g: _pad_heads(row(_rot_partner(g[:, MLA_NOPE:], MLA_ROPE)), 1, MLA_ROPE, MLA_NOPE)
    pair = lambda g: row(jnp.tile(g, (1, LANES // SWA_HEAD_DIM)))
    c0 = 2 * GROUP_WIDTH
    gg = jnp.concatenate([group_g[:, :c0], _pair_heads(group_g[:, c0:], 1)], axis=1)
    w_o = jnp.concatenate([w_out[:, :c0], _pair_heads(w_out[:, c0:], 1)], axis=1)
    return {
        "n1g": row(norm1_g), "w_in": w_in_p.astype(BF16), "qag": row(q_a_g),
        "w_uq": w_uq_p.astype(BF16), "kvag": row(kv_a_g),
        "w_uk": w_k.astype(BF16), "w_vt": w_vt.astype(BF16),
        "gq": _pad_heads(row(gq), 1, MLA_QK), "gq_rot": rope_gain(gq),
        "gk": _pad_heads(row(mla_k_g), 1, MLA_QK), "gk_rot": rope_gain(mla_k_g),
        "gqs": pair(gqs), "gqs_rot": pair(_rot_partner(gqs, SWA_HEAD_DIM)),
        "gks": pair(swa_k_g), "gks_rot": pair(_rot_partner(swa_k_g, SWA_HEAD_DIM)),
        "conv_w": conv_w, "conv_b": row(conv_b), "w_gate": w_gate.astype(BF16), "b_gate": b_gate,
        "lam": lru_lambda,
        "gg": row(gg), "w_out": w_o.astype(BF16), "n2g": row(norm2_g),
        "w_ff1": w_ff1.astype(BF16), "w_ff2": w_ff2.astype(BF16),
    }


def kernel(x, c, ctx, c_ctx, w_mod, b_mod, norm1_g, w_in, q_a_g, w_uq, kv_a_g, w_ukv, mla_q_g, mla_k_g, conv_w, conv_b, lru_gate_w, lru_gate_b, lru_lambda, swa_q_g, swa_k_g, swa_sink, group_g, w_out, norm2_g, w_ff1, w_ff2):
    b, l, d = x.shape
    lc = ctx.shape[1]
    depth = w_mod.shape[0]
    rows = l // GRID_W

    n_rows = -(-(b + 1) // SUBLANES) * SUBLANES
    cc = jnp.concatenate([c, c_ctx[None], jnp.zeros((n_rows - b - 1, d), F32)], axis=0)
    mod = _modulation(cc, w_mod, b_mod).reshape(depth, n_rows, 1, N_MOD * d)

    rope_m = _rope_tables(rows, MLA_ROPE, MLA_NOPE, LANES)
    rope_s = _rope_tables(rows, SWA_HEAD_DIM, 0, LANES)
    no_rope = _identity_tables(lc, LANES)
    wts = _prepare_weights(norm1_g, w_in, q_a_g, w_uq, kv_a_g, w_ukv, mla_q_g, mla_k_g, conv_w, conv_b,
                           lru_gate_w, lru_gate_b, lru_lambda, swa_q_g, swa_k_g, group_g, w_out,
                           norm2_g, w_ff1, w_ff2)

    xc = ctx
    for layer in range(depth):
        last = layer == depth - 1
        qm, km, vt, lx, lg, qs, ks, vts = _premix(x, mod, None, layer, wts, rope_m, rope_s)
        qm_c, km_c, vt_c, lx_c, lg_c, qs_c, ks_c, vts_c = _premix(xc, mod, b, layer, wts, no_rope, no_rope)

        o_b, o_b_c = _rglru(lx_c, lx, lg_c, lg, layer, wts)
        o_a = _mla_attention(qm, km_c, vt_c, km, vt)
        o_c = _swa_attention(qs, ks_c, vts_c, swa_sink, layer, ks, vts)
        x = _postmix(x, o_a, o_b, o_c, mod, None, layer, wts)
        if not last:
            o_a_c = _mla_attention(qm_c, km_c, vt_c)
            o_c_c = _swa_attention(qs_c, ks_c, vts_c, swa_sink, layer)
            xc = _postmix(xc, o_a_c, o_b_c, o_c_c, mod, b, layer, wts)
    return x
```

```python
import functools

import numpy as np
import jax
import jax.numpy as jnp
from jax import lax
from jax.experimental import pallas as pl
from jax.experimental.pallas import tpu as pltpu

F32 = jnp.float32
BF16 = jnp.bfloat16

GRID_W = 64
WINDOW = 128
ROPE_THETA = 10000.0
EPS = 1e-6
NEG_INF = -1e30
N_MOD = 6
MLA_HEADS = 8
MLA_NOPE = 64
MLA_ROPE = 32
MLA_QK = MLA_NOPE + MLA_ROPE
MLA_V = 64
MLA_Q_RANK = 256
MLA_KV_RANK = 128
LRU_WIDTH = 512
LRU_BLOCK_DIM = 64
LRU_C = 8.0
CONV_W = 4
SWA_HEADS = 8
SWA_KV_HEADS = 2
SWA_GROUP = SWA_HEADS // SWA_KV_HEADS
SWA_HEAD_DIM = 64
GROUP_WIDTH = 512

LANES = 128
SUBLANES = 8
V7X_VMEM_BYTES = 64 * 1024 * 1024
VMEM_LIMIT_BYTES = V7X_VMEM_BYTES - 8 * 1024 * 1024

MLA_QW = MLA_HEADS * LANES
MLA_VW = MLA_HEADS * MLA_V
MLA_HEADS_PER_STEP = 8
MLA_KEY_CHUNK = 512
LOG2_E = 1.4426950408889634
SWA_QW = SWA_HEADS * SWA_HEAD_DIM
SWA_KW = SWA_KV_HEADS * SWA_HEAD_DIM
SWA_PAIRS = SWA_QW // LANES
SWA_Q_BLOCKS_PER_STEP = 8
FF_CHUNK = 1024
PREMIX_ROWS = 512
LRU_GROUPS_PER_STEP = 2

OFF_CQ = 0
OFF_CKV = OFF_CQ + MLA_Q_RANK
OFF_KR = OFF_CKV + MLA_KV_RANK
SECTION_2 = OFF_KR + LANES
OFF_KRR = SECTION_2
OFF_SQ = OFF_KRR + LANES
OFF_SQR = OFF_SQ + SWA_QW
OFF_SK = OFF_SQR + SWA_QW
OFF_SKR = OFF_SK + SWA_KW
OFF_SV = OFF_SKR + SWA_KW
OFF_LX = OFF_SV + SWA_KW
OFF_LG = OFF_LX + LRU_WIDTH
W_IN_PAD = OFF_LG + LRU_WIDTH


def _dot(a, b):
    return jnp.dot(a, b, preferred_element_type=F32)


def _dot_nt(a, b):
    return lax.dot_general(a, b, (((1,), (1,)), ((), ())), preferred_element_type=F32)


def _rms(x):
    return x * lax.rsqrt(jnp.mean(x * x, axis=-1, keepdims=True) + EPS)


def _params(*sem, flags=None):
    return pltpu.CompilerParams(dimension_semantics=sem, vmem_limit_bytes=VMEM_LIMIT_BYTES, flags=flags)


def _layer_spec(arr, layer):
    index = (layer,) + (0,) * (arr.ndim - 1)
    return pl.BlockSpec((None,) + arr.shape[1:], lambda *_: index)


def _mod_spec(mod, layer, shared_row):
    if shared_row is None:
        return pl.BlockSpec((None, 1, 1, mod.shape[-1]), lambda bi, i: (layer, bi, 0, 0))
    return pl.BlockSpec((None, 1, 1, mod.shape[-1]), lambda bi, i: (layer, shared_row, 0, 0))


def _mod_kernel(c_ref, w_ref, b_ref, o_ref):
    c = c_ref[...]
    a = c * jax.nn.sigmoid(c)
    w = w_ref[0]
    a_hi = a.astype(BF16)
    a_lo = (a - a_hi.astype(F32)).astype(BF16)
    w_hi = w.astype(BF16)
    w_lo = (w - w_hi.astype(F32)).astype(BF16)
    o_ref[0] = _dot(a_hi, w_hi) + _dot(a_hi, w_lo) + _dot(a_lo, w_hi) + b_ref[0]


def _modulation(cc, w_mod, b_mod):
    depth, d, n = w_mod.shape
    rows = cc.shape[0]
    tn = n // 4
    return pl.pallas_call(
        _mod_kernel,
        grid=(depth, n // tn),
        in_specs=[
            pl.BlockSpec((rows, d), lambda l, j: (0, 0)),
            pl.BlockSpec((1, d, tn), lambda l, j: (l, 0, j)),
            pl.BlockSpec((1, 1, tn), lambda l, j: (l, 0, j)),
        ],
        out_specs=pl.BlockSpec((1, rows, tn), lambda l, j: (l, 0, j)),
        out_shape=jax.ShapeDtypeStruct((depth, rows, n), F32),
        compiler_params=_params("arbitrary", "arbitrary"),
        name="modulation",
    )(cc, w_mod, b_mod.reshape(depth, 1, n))


def _premix_kernel(x_ref, mod_ref, n1g_ref, win_ref, qag_ref, wuq_ref, kvag_ref, wuk_ref, wvt_ref,
                   gq_ref, gqr_ref, gk_ref, gkr_ref, gqs_ref, gqsr_ref, gks_ref, gksr_ref,
                   cm_ref, sm_ref, cs_ref, ss_ref,
                   qm_ref, km_ref, vt_ref, lx_ref, lg_ref, qs_ref, ks_ref, vts_ref):
    d = x_ref.shape[-1]
    tm = x_ref.shape[1]
    shift = mod_ref[0, :, 0:d]
    scale = mod_ref[0, :, d:2 * d]
    n = PREMIX_ROWS if tm % PREMIX_ROWS == 0 else tm
    for rows in (slice(r, r + n) for r in range(0, tm, n)):
        _premix_rows(rows, n, shift, scale, x_ref, n1g_ref, win_ref, qag_ref, wuq_ref, kvag_ref, wuk_ref, wvt_ref,
                     gq_ref, gqr_ref, gk_ref, gkr_ref, gqs_ref, gqsr_ref, gks_ref, gksr_ref,
                     cm_ref, sm_ref, cs_ref, ss_ref,
                     qm_ref, km_ref, vt_ref, lx_ref, lg_ref, qs_ref, ks_ref, vts_ref)


def _premix_rows(rows, n, shift, scale, x_ref, n1g_ref, win_ref, qag_ref, wuq_ref, kvag_ref, wuk_ref, wvt_ref,
                 gq_ref, gqr_ref, gk_ref, gkr_ref, gqs_ref, gqsr_ref, gks_ref, gksr_ref,
                 cm_ref, sm_ref, cs_ref, ss_ref,
                 qm_ref, km_ref, vt_ref, lx_ref, lg_ref, qs_ref, ks_ref, vts_ref):
    h = (_rms(x_ref[0, rows, :]) * n1g_ref[...] * (1.0 + scale) + shift).astype(BF16)

    cm, sm = cm_ref[rows, :], sm_ref[rows, :]
    cs, ss = cs_ref[rows, :], ss_ref[rows, :]

    p1 = _dot(h, win_ref[:, 0:SECTION_2])
    qn = (_rms(p1[:, OFF_CQ:OFF_CQ + MLA_Q_RANK]) * qag_ref[...]).astype(BF16)
    kvn = (_rms(p1[:, OFF_CKV:OFF_CKV + MLA_KV_RANK]) * kvag_ref[...]).astype(BF16)
    kr = p1[:, OFF_KR:OFF_KR + LANES]
    ss_kr = jnp.sum(kr * kr, axis=-1, keepdims=True)

    p2 = _dot(h, win_ref[:, SECTION_2:OFF_LX])
    qu = _dot(qn, wuq_ref[...])
    kvu = _dot(kvn, wuk_ref[...])
    vt_ref[0, :, rows] = _dot_nt(wvt_ref[...], kvn).astype(BF16)
    ga_k = gk_ref[...] * cm
    shared = kr * ga_k + p2[:, 0:LANES] * (gkr_ref[...] * sm)
    lo = lax.broadcasted_iota(jnp.int32, (n, LANES), 1) < SWA_HEAD_DIM

    def pair_norm_rope(off, off_rot, ga, gb):
        x1 = p2[:, off - SECTION_2:off - SECTION_2 + LANES]
        x2 = p2[:, off_rot - SECTION_2:off_rot - SECTION_2 + LANES]
        sq = x1 * x1
        s_lo = jnp.sum(jnp.where(lo, sq, 0.0), axis=-1, keepdims=True)
        s_hi = jnp.sum(jnp.where(lo, 0.0, sq), axis=-1, keepdims=True)
        inv = 1.0 / SWA_HEAD_DIM
        r = jnp.where(lo, lax.rsqrt(s_lo * inv + EPS), lax.rsqrt(s_hi * inv + EPS))
        return (r * (x1 * ga + x2 * gb)).astype(BF16)

    ga_s, gb_s = gqs_ref[...] * cs, gqsr_ref[...] * ss
    for g in range(SWA_PAIRS):
        qs_ref[0, rows, g * LANES:(g + 1) * LANES] = pair_norm_rope(OFF_SQ + g * LANES, OFF_SQR + g * LANES, ga_s, gb_s)
    ks_ref[0, rows, :] = pair_norm_rope(OFF_SK, OFF_SKR, gks_ref[...] * cs, gksr_ref[...] * ss)
    vts_ref[0, :, rows] = p2[:, OFF_SV - SECTION_2:OFF_SV - SECTION_2 + SWA_KW].T.astype(BF16)

    p3 = _dot(h, win_ref[:, OFF_LX:W_IN_PAD])
    ga_q, gb_q = gq_ref[...] * cm, gqr_ref[...] * sm
    for hd in range(MLA_HEADS):
        x1 = qu[:, hd * LANES:(hd + 1) * LANES]
        x2 = qu[:, MLA_QW + hd * LANES:MLA_QW + (hd + 1) * LANES]
        r = lax.rsqrt(jnp.sum(x1 * x1, axis=-1, keepdims=True) * (1.0 / MLA_QK) + EPS)
        qm_ref[0, rows, hd * LANES:(hd + 1) * LANES] = (r * (x1 * ga_q + x2 * gb_q)).astype(BF16)
    for hd in range(MLA_HEADS):
        x1 = kvu[:, hd * LANES:(hd + 1) * LANES]
        r = lax.rsqrt((jnp.sum(x1 * x1, axis=-1, keepdims=True) + ss_kr) * (1.0 / MLA_QK) + EPS)
        km_ref[0, rows, hd * LANES:(hd + 1) * LANES] = (r * (x1 * ga_k + shared)).astype(BF16)
    lx_ref[0, rows, :] = p3[:, 0:LRU_WIDTH]
    lg_ref[0, rows, :] = p3[:, LRU_WIDTH:2 * LRU_WIDTH]


def _premix(x, mod, shared_row, layer, wts, rope_m, rope_s):
    b, l, d = x.shape
    tm = min(l, 1024)
    consts = [wts[k] for k in ("n1g", "w_in", "qag", "w_uq", "kvag", "w_uk", "w_vt",
                               "gq", "gq_rot", "gk", "gk_rot", "gqs", "gqs_rot", "gks", "gks_rot")]
    tables = list(rope_m) + list(rope_s)
    tok = lambda w, dt: (pl.BlockSpec((1, tm, w), lambda bi, i: (bi, i, 0)), jax.ShapeDtypeStruct((b, l, w), dt))
    tr = lambda w: (pl.BlockSpec((1, w, tm), lambda bi, i: (bi, 0, i)), jax.ShapeDtypeStruct((b, w, l), BF16))
    outs = [tok(MLA_QW, BF16), tok(MLA_QW, BF16), tr(MLA_VW), tok(LRU_WIDTH, F32), tok(LRU_WIDTH, F32),
            tok(SWA_QW, BF16), tok(SWA_KW, BF16), tr(SWA_KW)]
    return pl.pallas_call(
        _premix_kernel,
        grid=(b, l // tm),
        in_specs=[tok(d, F32)[0], _mod_spec(mod, layer, shared_row)]
        + [_layer_spec(c, layer) for c in consts]
        + [pl.BlockSpec((tm, LANES), lambda bi, i: (i, 0)) for _ in tables],
        out_specs=[o[0] for o in outs],
        out_shape=[o[1] for o in outs],
        compiler_params=_params("arbitrary", "arbitrary"),
        name="premix",
    )(x, mod, *consts, *tables)


def _gelu_tanh(x):
    c = 0.7978845608028654
    half_x = 0.5 * x
    return half_x * jnp.tanh(x * (c + (c * 0.044715) * (x * x))) + half_x


LRU_SCRATCH_PER_GROUP = 5


def _lru_kernel(lxc_ref, lxl_ref, lgc_ref, lgl_ref, cw_ref, cb_ref, wg_ref, bg_ref, lam_ref,
                obl_ref, obc_ref, *scratch):
    lc = lxc_ref.shape[1]
    ll = lxl_ref.shape[1]
    groups = [scratch[i:i + LRU_SCRATCH_PER_GROUP] for i in range(0, len(scratch), LRU_SCRATCH_PER_GROUP)]
    pitch = scratch[0].shape[0] // SUBLANES
    tail = scratch[0].shape[0] - (lc + ll)
    assert tail > 0
    assert pitch % 2 == 0

    def coefficients(gi, x, xs):
        lanes = slice(gi * LANES, (gi + 1) * LANES)
        cw = cw_ref[:, lanes]
        nlam = -lam_ref[:, lanes]
        softplus = jnp.maximum(nlam, 0.0) + jnp.log1p(jnp.exp(-jnp.abs(nlam)))
        decay = (-0.5 * LRU_C * LOG2_E) * softplus
        t = x.shape[0]
        pad = jnp.zeros((SUBLANES, LANES), F32)
        xs[0:SUBLANES, :] = pad
        xs[SUBLANES:SUBLANES + t, :] = x
        xs[SUBLANES + t:2 * SUBLANES + t, :] = pad
        xm2, xm1, xp1 = (xs[SUBLANES + o:SUBLANES + o + t, :] for o in (-2, -1, 1))
        xc = cw[0:1] * xm2 + cw[1:2] * xm1 + cw[2:3] * x + cw[3:4] * xp1 + cb_ref[:, lanes]
        th = jnp.tanh(_dot(xc.astype(BF16), wg_ref[gi]) + bg_ref[gi])
        half_x = 0.5 * xc
        out = []
        for z in range(2):
            t_r = th[:, 2 * z * LANES:(2 * z + 1) * LANES]
            t_i = th[:, (2 * z + 1) * LANES:(2 * z + 2) * LANES]
            a = jnp.exp2(decay[z:z + 1] * t_r + decay[z:z + 1])
            y = 1.0 - a * a
            root = y * lax.rsqrt(jnp.maximum(y, jnp.finfo(F32).tiny))
            out.append((a, root * (half_x * t_i + half_x)))
        return out

    for gi, (af, uf, ab, ub, xs) in enumerate(groups):
        for x_ref, f_off, b_off in ((lxc_ref, 0, ll), (lxl_ref, lc, 0)):
            t = x_ref.shape[1]
            (a_f, u_f), (a_b, u_b) = coefficients(gi, x_ref[0, :, gi * LANES:(gi + 1) * LANES], xs)
            af[f_off:f_off + t, :] = a_f
            uf[f_off:f_off + t, :] = u_f
            ab[b_off:b_off + t, :] = a_b
            ub[b_off:b_off + t, :] = u_b
        for a_ref, u_ref in ((af, uf), (ab, ub)):
            a_ref[lc + ll:, :] = jnp.ones((tail, LANES), F32)
            u_ref[lc + ll:, :] = jnp.zeros((tail, LANES), F32)

    def streams(i):
        return pl.ds(i, SUBLANES, stride=pitch)

    def two_steps(a_ref, u_ref, i0, i1, h, p):
        a0, u0 = a_ref[streams(i0), :], u_ref[streams(i0), :]
        a1, u1 = a_ref[streams(i1), :], u_ref[streams(i1), :]
        a01 = a1 * a0
        u_ref[streams(i0), :] = a0 * h + u0
        a_ref[streams(i0), :] = a0 * p
        h = a01 * h + (a1 * u0 + u1)
        p = a01 * p
        u_ref[streams(i1), :] = h
        a_ref[streams(i1), :] = p
        return h, p

    def local_scan(i, state):
        out = []
        for (af, uf, ab, ub, _), (h_f, p_f, h_b, p_b) in zip(groups, state):
            h_f, p_f = two_steps(af, uf, 2 * i, 2 * i + 1, h_f, p_f)
            h_b, p_b = two_steps(ab, ub, pitch - 1 - 2 * i, pitch - 2 - 2 * i, h_b, p_b)
            out.append((h_f, p_f, h_b, p_b))
        return tuple(out)

    zeros8 = jnp.zeros((SUBLANES, LANES), F32)
    ones8 = jnp.ones((SUBLANES, LANES), F32)
    finals = lax.fori_loop(0, pitch // 2, local_scan, ((zeros8, ones8, zeros8, ones8),) * len(groups), unroll=2)

    sub = lax.broadcasted_iota(jnp.int32, (SUBLANES, LANES), 0)
    carries = []
    for h_f, p_f, h_b, p_b in finals:
        c_f = zeros8
        c_b = zeros8
        row_f = jnp.zeros((1, LANES), F32)
        row_b = jnp.zeros((1, LANES), F32)
        for s in range(1, SUBLANES):
            row_f = h_f[s - 1:s] + p_f[s - 1:s] * row_f
            c_f = jnp.where(sub == s, row_f, c_f)
            sb = SUBLANES - 1 - s
            row_b = h_b[sb + 1:sb + 2] + p_b[sb + 1:sb + 2] * row_b
            c_b = jnp.where(sub == sb, row_b, c_b)
        carries.append((c_f, c_b))

    def add_carry(i, _):
        for (af, uf, ab, ub, _), (c_f, c_b) in zip(groups, carries):
            uf[streams(i), :] = uf[streams(i), :] + af[streams(i), :] * c_f
            ub[streams(i), :] = ub[streams(i), :] + ab[streams(i), :] * c_b
        return 0

    lax.fori_loop(0, pitch, add_carry, 0, unroll=4)

    for gi, (_, uf, _, ub, _) in enumerate(groups):
        lanes = slice(gi * LANES, (gi + 1) * LANES)
        obl_ref[0, :, lanes] = (uf[lc:lc + ll, :] + ub[0:ll, :]) * _gelu_tanh(lgl_ref[0, :, lanes])
        obc_ref[0, :, lanes] = (uf[0:lc, :] + ub[ll:ll + lc, :]) * _gelu_tanh(lgc_ref[0, :, lanes])


def _rglru(lx_c, lx_l, lg_c, lg_l, layer, wts):
    b, lc, w = lx_c.shape
    ll = lx_l.shape[1]
    gps = LRU_GROUPS_PER_STEP
    bw = gps * LANES
    pitch = -(-(lc + ll) // SUBLANES)
    pitch += (SUBLANES // 2 - pitch) % SUBLANES
    seq = lambda t: pl.BlockSpec((1, t, bw), lambda bi, g: (bi, 0, g))
    group_scratch = [pltpu.VMEM((SUBLANES * pitch, LANES), F32) for _ in range(LRU_SCRATCH_PER_GROUP - 1)]
    group_scratch.append(pltpu.VMEM((max(lc, ll) + 2 * SUBLANES, LANES), F32))
    return pl.pallas_call(
        _lru_kernel,
        grid=(b, w // bw),
        in_specs=[seq(lc), seq(ll), seq(lc), seq(ll),
                  pl.BlockSpec((None, CONV_W, bw), lambda bi, g: (layer, 0, g)),
                  pl.BlockSpec((None, 1, bw), lambda bi, g: (layer, 0, g)),
                  pl.BlockSpec((None, gps, LANES, 4 * LANES), lambda bi, g: (layer, g, 0, 0)),
                  pl.BlockSpec((None, gps, 1, 4 * LANES), lambda bi, g: (layer, g, 0, 0)),
                  pl.BlockSpec((None, 2, bw), lambda bi, g: (layer, 0, g))],
        out_specs=[seq(ll), seq(lc)],
        out_shape=[jax.ShapeDtypeStruct((b, ll, w), F32), jax.ShapeDtypeStruct((b, lc, w), F32)],
        scratch_shapes=group_scratch * gps,
        compiler_params=_params("arbitrary", "arbitrary"),
        name="rglru",
    )(lx_c, lx_l, lg_c, lg_l, wts["conv_w"], wts["conv_b"], wts["w_gate"], wts["b_gate"], wts["lam"])


def _mla_kernel(*refs, with_latent):
    if with_latent:
        q_ref, kc_ref, vtc_ref, kl_ref, vtl_ref, o_ref, s_buf = refs
        sources = ((kc_ref, vtc_ref), (kl_ref, vtl_ref))
    else:
        q_ref, kc_ref, vtc_ref, o_ref, s_buf = refs
        sources = ((kc_ref, vtc_ref),)
    tq = q_ref.shape[1]
    chunks, row = [], 0
    for k_ref, vt_ref in sources:
        n = k_ref.shape[1]
        for k0 in range(0, n, MLA_KEY_CHUNK):
            kn = min(MLA_KEY_CHUNK, n - k0)
            chunks.append((k_ref, vt_ref, k0, kn, row))
            row += kn

    def add(acc, x, op=jnp.add):
        return x if acc is None else op(acc, x)

    slot0 = jnp.minimum(pl.program_id(0), 0)

    def score_chunk(hd, chunk, m8):
        k_ref, _, k0, kn, r0 = chunk
        sl = slice(hd * LANES, (hd + 1) * LANES)
        s = _dot_nt(k_ref[0, k0:k0 + kn, sl], q_ref[0, :, sl])
        s_buf[hd % 2 + slot0, r0:r0 + kn, :] = s
        return add(m8, jnp.max(s.reshape(kn // SUBLANES, SUBLANES, tq), axis=0), jnp.maximum)

    def attend_chunk(hd, chunk, m, l8, o_t):
        _, vt_ref, k0, kn, r0 = chunk
        p = jnp.exp2(s_buf[hd % 2 + slot0, r0:r0 + kn, :] - m)
        l8 = add(l8, jnp.sum(p.reshape(kn // SUBLANES, SUBLANES, tq), axis=0))
        o_t = add(o_t, _dot(vt_ref[0, hd * MLA_V:(hd + 1) * MLA_V, k0:k0 + kn], p.astype(BF16)))
        return l8, o_t

    outs = []
    m8 = None
    for chunk in chunks:
        m8 = score_chunk(0, chunk, m8)
    for hd in range(MLA_HEADS_PER_STEP):
        m = jnp.max(m8, axis=0, keepdims=True)
        m8, l8, o_t = None, None, None
        if hd + 1 < MLA_HEADS_PER_STEP:
            for chunk in chunks:
                m8 = score_chunk(hd + 1, chunk, m8)
        for chunk in chunks:
            l8, o_t = attend_chunk(hd, chunk, m, l8, o_t)
        outs.append(o_t * (1.0 / jnp.sum(l8, axis=0, keepdims=True)))
    o_ref[0] = jnp.concatenate(outs, axis=0).T.astype(o_ref.dtype)


def _mla_attention(q, k_c, vt_c, k_l=None, vt_l=None):
    b, lq, _ = q.shape
    lc = k_c.shape[1]
    with_latent = k_l is not None
    tq = min(lq, 512)
    qw = MLA_HEADS_PER_STEP * LANES
    vw = MLA_HEADS_PER_STEP * MLA_V
    nq = MLA_HEADS // MLA_HEADS_PER_STEP
    keys = lambda t: pl.BlockSpec((1, t, qw), lambda bi, hq, i: (bi, 0, hq))
    vals = lambda t: pl.BlockSpec((1, vw, t), lambda bi, hq, i: (bi, hq, 0))
    in_specs = [pl.BlockSpec((1, tq, qw), lambda bi, hq, i: (bi, i, hq)), keys(lc), vals(lc)]
    args = [q, k_c, vt_c]
    if with_latent:
        in_specs += [keys(k_l.shape[1]), vals(k_l.shape[1])]
        args += [k_l, vt_l]
    return pl.pallas_call(
        functools.partial(_mla_kernel, with_latent=with_latent),
        grid=(b, nq, lq // tq),
        in_specs=in_specs,
        out_specs=pl.BlockSpec((1, tq, vw), lambda bi, hq, i: (bi, i, hq)),
        out_shape=jax.ShapeDtypeStruct((b, lq, MLA_VW), BF16),
        scratch_shapes=[pltpu.VMEM((2, lc + (k_l.shape[1] if with_latent else 0), tq), F32)],
        compiler_params=_params("arbitrary", "arbitrary", "arbitrary"),
        name="mla_attention" if with_latent else "mla_attention_ctx",
    )(*args)


def _swa_kernel(*refs, with_window, layer):
    if with_window:
        sink_ref, q_ref, kc_ref, vtc_ref, kl_ref, vtl_ref, o_ref = refs
    else:
        sink_ref, q_ref, kc_ref, vtc_ref, o_ref = refs
    bq = WINDOW
    cols = SWA_PAIRS * bq
    kc = kc_ref[0]
    lo = lax.broadcasted_iota(jnp.int32, (cols, LANES), 1) < SWA_HEAD_DIM
    seg = lax.broadcasted_iota(jnp.int32, (1, cols), 1) // bq

    span = bq + 2 * WINDOW
    sinks = []
    for kvh in range(SWA_KV_HEADS):
        sink = jnp.zeros((1, cols), F32)
        for g in range(SWA_PAIRS):
            sink = jnp.where(seg == g, sink_ref[layer, kvh * SWA_GROUP + g] * LOG2_E, sink)
        sinks.append(sink)

    def scores(sb):
        r0 = sb * bq
        q = jnp.concatenate([q_ref[0, r0:r0 + bq, g * LANES:(g + 1) * LANES] for g in range(SWA_PAIRS)],
                            axis=0).astype(F32)
        w0 = None
        if with_window:
            q0 = pl.program_id(1) * q_ref.shape[1] + r0
            w0 = pl.multiple_of(jnp.clip(q0 - WINDOW, 0, kl_ref.shape[1] - span), LANES)
            kw = kl_ref[0, pl.ds(w0, span), :]
            kj = w0 + lax.broadcasted_iota(jnp.int32, (span, cols), 0)
            qi = q0 + (lax.broadcasted_iota(jnp.int32, (span, cols), 1) & (bq - 1))
            band = jnp.abs(qi - kj) <= WINDOW
        out = []
        for kvh in range(SWA_KV_HEADS):
            qh = (jnp.where(lo, 0.0, q) if kvh else jnp.where(lo, q, 0.0)).astype(BF16)
            s = [_dot_nt(kc, qh)]
            if with_window:
                s.append(jnp.where(band, _dot_nt(kw, qh), NEG_INF))
            out.append(s)
        return w0, out

    def attend(sb, w0, blk_scores):
        o_t = []
        for kvh, s in enumerate(blk_scores):
            sink = sinks[kvh]
            m = functools.reduce(jnp.maximum, [jnp.max(x, axis=0, keepdims=True) for x in s] + [sink])
            p = [jnp.exp2(x - m) for x in s]
            denom = sum(jnp.sum(x, axis=0, keepdims=True) for x in p) + jnp.exp2(sink - m)
            vs = slice(kvh * SWA_HEAD_DIM, (kvh + 1) * SWA_HEAD_DIM)
            o = _dot(vtc_ref[0, vs, :], p[0].astype(BF16))
            if with_window:
                o = o + _dot(vtl_ref[0, vs, pl.ds(w0, span)], p[1].astype(BF16))
            o_t.append(o * (1.0 / denom))
        out = jnp.concatenate(o_t, axis=0).T.astype(o_ref.dtype)
        for g in range(SWA_PAIRS):
            o_ref[0, sb * bq:(sb + 1) * bq, g * LANES:(g + 1) * LANES] = out[g * bq:(g + 1) * bq]

    n_blocks = q_ref.shape[1] // bq
    nxt = scores(0)
    for sb in range(n_blocks):
        cur = nxt
        if sb + 1 < n_blocks:
            nxt = scores(sb + 1)
        attend(sb, *cur)


def _swa_attention(q, k_c, vt_c, sink, layer, k_l=None, vt_l=None):
    b, lq, _ = q.shape
    lc = k_c.shape[1]
    with_window = k_l is not None
    tq = min(lq, SWA_Q_BLOCKS_PER_STEP * WINDOW)
    keys = lambda t: pl.BlockSpec((1, t, SWA_KW), lambda bi, i: (bi, 0, 0))
    vals = lambda t: pl.BlockSpec((1, SWA_KW, t), lambda bi, i: (bi, 0, 0))
    in_specs = [pl.BlockSpec(memory_space=pltpu.SMEM),
                pl.BlockSpec((1, tq, SWA_QW), lambda bi, i: (bi, i, 0)), keys(lc), vals(lc)]
    args = [sink, q, k_c, vt_c]
    if with_window:
        in_specs += [keys(k_l.shape[1]), vals(k_l.shape[1])]
        args += [k_l, vt_l]
    return pl.pallas_call(
        functools.partial(_swa_kernel, with_window=with_window, layer=layer),
        grid=(b, lq // tq),
        in_specs=in_specs,
        out_specs=pl.BlockSpec((1, tq, SWA_QW), lambda bi, i: (bi, i, 0)),
        out_shape=jax.ShapeDtypeStruct((b, lq, SWA_QW), BF16),
        compiler_params=_params("arbitrary", "arbitrary"),
        name="swa_attention" if with_window else "swa_attention_ctx",
    )(*args)


def _postmix_kernel(x_ref, oa_ref, ob_ref, oc_ref, mod_ref, gg_ref, wout_ref, n2g_ref, w1_ref, w2_ref, o_ref):
    d = x_ref.shape[-1]
    gate1 = mod_ref[0, :, 2 * d:3 * d]
    shift2 = mod_ref[0, :, 3 * d:4 * d]
    scale2 = mod_ref[0, :, 4 * d:5 * d]
    gate2 = mod_ref[0, :, 5 * d:6 * d]
    tm = x_ref.shape[1]
    halves = [slice(r, r + tm // 2) for r in (0, tm // 2)]

    def mix(rows):
        y = None
        for gi, ref in enumerate((oa_ref, ob_ref, oc_ref)):
            sl = slice(gi * GROUP_WIDTH, (gi + 1) * GROUP_WIDTH)
            og = _rms(ref[0, rows, :].astype(F32)) * gg_ref[:, sl]
            part = _dot(og.astype(BF16), wout_ref[sl, :])
            y = part if y is None else y + part
        return y

    def residual_norm(rows, y):
        x1 = x_ref[0, rows, :] + gate1 * y
        return x1, (_rms(x1) * n2g_ref[...] * (1.0 + scale2) + shift2).astype(BF16)

    def mlp(h2):
        ff = None
        for j in range(w1_ref.shape[1] // FF_CHUNK):
            sl = slice(j * FF_CHUNK, (j + 1) * FF_CHUNK)
            hid = jnp.maximum(_dot(h2, w1_ref[:, sl]), 0.0)
            part = _dot((hid * hid).astype(BF16), w2_ref[sl, :])
            ff = part if ff is None else ff + part
        return ff

    y = [mix(rows) for rows in halves]
    for rows, y_half in zip(halves, y):
        x1, h2 = residual_norm(rows, y_half)
        o_ref[0, rows, :] = x1 + gate2 * mlp(h2)


def _postmix(x, o_a, o_b, o_c, mod, shared_row, layer, wts):
    b, l, d = x.shape
    tm = min(l, 512)
    tok = lambda w: pl.BlockSpec((1, tm, w), lambda bi, i: (bi, i, 0))
    consts = [wts["gg"], wts["w_out"], wts["n2g"], wts["w_ff1"], wts["w_ff2"]]
    return pl.pallas_call(
        _postmix_kernel,
        grid=(b, l // tm),
        in_specs=[tok(d), tok(GROUP_WIDTH), tok(GROUP_WIDTH), tok(GROUP_WIDTH), _mod_spec(mod, layer, shared_row)]
        + [_layer_spec(c, layer) for c in consts],
        out_specs=tok(d),
        out_shape=jax.ShapeDtypeStruct((b, l, d), F32),
        compiler_params=_params("arbitrary", "arbitrary"),
        name="postmix",
    )(x, o_a, o_b, o_c, mod, *consts)


def _rot_sign(dim):
    quarter = dim // 4
    return np.where((np.arange(dim) // quarter) % 2 == 0, -1.0, 1.0).astype(np.float32)


def _rot_partner(w, dim):
    return jnp.flip(w.reshape(w.shape[:-1] + (-1, 2, dim // 4)), axis=-2).reshape(w.shape)


def _pair_heads(w, axis):
    axis = axis % w.ndim
    shape = w.shape
    w = w.reshape(shape[:axis] + (SWA_KV_HEADS, SWA_PAIRS, SWA_HEAD_DIM) + shape[axis + 1:])
    return jnp.swapaxes(w, axis, axis + 1).reshape(shape)


def _rope_tables(rows, dim, lead, width):
    quarter = dim // 4
    n = rows * GRID_W
    row = jnp.repeat(jnp.arange(rows), GRID_W)
    col = jnp.tile(jnp.arange(GRID_W), rows)
    inv_freq = ROPE_THETA ** (-jnp.arange(quarter, dtype=F32) / quarter)
    ang = jnp.stack([row, col], axis=-1).astype(F32)[:, :, None] * inv_freq
    ang = jnp.broadcast_to(ang[:, :, None, :], (n, 2, 2, quarter)).reshape(n, dim)
    sign = _rot_sign(dim)

    def place(t, fill):
        t = jnp.concatenate([jnp.full((n, lead), fill, F32), t], axis=1)
        t = jnp.tile(t, (1, width // (lead + dim)))
        return jnp.concatenate([t, jnp.full((n, width - t.shape[1]), fill, F32)], axis=1)

    return place(jnp.cos(ang), 1.0), place(jnp.sin(ang) * sign, 0.0)


def _identity_tables(n, width):
    return jnp.ones((n, width), F32), jnp.zeros((n, width), F32)


def _pad_heads(w, heads, width, lead=0):
    shape = w.shape[:-1]
    w = w.reshape(shape + (heads, width))
    w = jnp.pad(w, [(0, 0)] * len(shape) + [(0, 0), (lead, LANES - lead - width)])
    return w.reshape(shape + (heads * LANES,))


def _split_in(w):
    sizes = (MLA_Q_RANK, MLA_KV_RANK, MLA_ROPE, LRU_WIDTH, LRU_WIDTH, SWA_QW, SWA_KW, SWA_KW)
    parts, start = [], 0
    for s in sizes:
        parts.append(w[..., start:start + s])
        start += s
    return parts


def _prepare_weights(norm1_g, w_in, q_a_g, w_uq, kv_a_g, w_ukv, mla_q_g, mla_k_g, conv_w, conv_b,
                     lru_gate_w, lru_gate_b, lru_lambda, swa_q_g, swa_k_g, group_g, w_out,
                     norm2_g, w_ff1, w_ff2):
    depth = w_in.shape[0]
    row = lambda v: v[:, None, :]
    cq, ckv, kr, lx, lg, sq, sk, sv = _split_in(w_in)
    sq = _pair_heads(sq, -1)
    w_in_p = jnp.concatenate(
        [cq, ckv, _pad_heads(kr, 1, MLA_ROPE, MLA_NOPE), _pad_heads(_rot_partner(kr, MLA_ROPE), 1, MLA_ROPE, MLA_NOPE),
         sq, _rot_partner(sq, SWA_HEAD_DIM), sk, _rot_partner(sk, SWA_HEAD_DIM), sv, lx, lg], axis=-1)
    uq = w_uq.reshape(depth, MLA_Q_RANK, MLA_HEADS, MLA_QK)
    uq_rot = _rot_partner(uq[..., MLA_NOPE:], MLA_ROPE).reshape(depth, MLA_Q_RANK, MLA_HEADS * MLA_ROPE)
    w_uq_p = jnp.concatenate([_pad_heads(w_uq, MLA_HEADS, MLA_QK),
                              _pad_heads(uq_rot, MLA_HEADS, MLA_ROPE, MLA_NOPE)], axis=-1)
    kv = w_ukv.reshape(depth, MLA_KV_RANK, MLA_HEADS, MLA_NOPE + MLA_V)
    w_k = _pad_heads(kv[..., :MLA_NOPE].reshape(depth, MLA_KV_RANK, MLA_HEADS * MLA_NOPE), MLA_HEADS, MLA_NOPE)
    w_vt = jnp.swapaxes(kv[..., MLA_NOPE:].reshape(depth, MLA_KV_RANK, MLA_VW), 1, 2)
    nblk = lru_gate_w.shape[3]
    per_group = LANES // LRU_BLOCK_DIM
    groups = nblk // per_group
    eye = jnp.eye(per_group, dtype=F32)
    gw = lru_gate_w.reshape(depth, 2, 2, groups, per_group, LRU_BLOCK_DIM, LRU_BLOCK_DIM)
    w_gate = 0.5 * jnp.einsum("dzgpncm,nk->dpnczgkm", gw, eye).reshape(depth, groups, LANES, 4 * LANES)
    b_gate = 0.5 * lru_gate_b.reshape(depth, 2, 2, groups, LANES).transpose(0, 3, 1, 2, 4).reshape(
        depth, groups, 1, 4 * LANES)
    gq = mla_q_g * (MLA_QK ** -0.5 * LOG2_E)
    gqs = swa_q_g * (SWA_HEAD_DIM ** -0.5 * LOG2_E)
    rope_gain = lambda g: _pad_heads(row(_rot_partner(g[:, MLA_NOPE:], MLA_ROPE)), 1, MLA_ROPE, MLA_NOPE)
    pair = lambda g: row(jnp.tile(g, (1, LANES // SWA_HEAD_DIM)))
    c0 = 2 * GROUP_WIDTH
    gg = jnp.concatenate([group_g[:, :c0], _pair_heads(group_g[:, c0:], 1)], axis=1)
    w_o = jnp.concatenate([w_out[:, :c0], _pair_heads(w_out[:, c0:], 1)], axis=1)
    return {
        "n1g": row(norm1_g), "w_in": w_in_p.astype(BF16), "qag": row(q_a_g),
        "w_uq": w_uq_p.astype(BF16), "kvag": row(kv_a_g),
        "w_uk": w_k.astype(BF16), "w_vt": w_vt.astype(BF16),
        "gq": _pad_heads(row(gq), 1, MLA_QK), "gq_rot": rope_gain(gq),
        "gk": _pad_heads(row(mla_k_g), 1, MLA_QK), "gk_rot": rope_gain(mla_k_g),
        "gqs": pair(gqs), "gqs_rot": pair(_rot_partner(gqs, SWA_HEAD_DIM)),
        "gks": pair(swa_k_g), "gks_rot": pair(_rot_partner(swa_k_g, SWA_HEAD_DIM)),
        "conv_w": conv_w, "conv_b": row(conv_b), "w_gate": w_gate.astype(BF16), "b_gate": b_gate,
        "lam": lru_lambda,
        "gg": row(gg), "w_out": w_o.astype(BF16), "n2g": row(norm2_g),
        "w_ff1": w_ff1.astype(BF16), "w_ff2": w_ff2.astype(BF16),
    }


def kernel(x, c, ctx, c_ctx, w_mod, b_mod, norm1_g, w_in, q_a_g, w_uq, kv_a_g, w_ukv, mla_q_g, mla_k_g, conv_w, conv_b, lru_gate_w, lru_gate_b, lru_lambda, swa_q_g, swa_k_g, swa_sink, group_g, w_out, norm2_g, w_ff1, w_ff2):
    b, l, d = x.shape
    lc = ctx.shape[1]
    depth = w_mod.shape[0]
    rows = l // GRID_W

    n_rows = -(-(b + 1) // SUBLANES) * SUBLANES
    cc = jnp.concatenate([c, c_ctx[None], jnp.zeros((n_rows - b - 1, d), F32)], axis=0)
    mod = _modulation(cc, w_mod, b_mod).reshape(depth, n_rows, 1, N_MOD * d)

    rope_m = _rope_tables(rows, MLA_ROPE, MLA_NOPE, LANES)
    rope_s = _rope_tables(rows, SWA_HEAD_DIM, 0, LANES)
    no_rope = _identity_tables(lc, LANES)
    wts = _prepare_weights(norm1_g, w_in, q_a_g, w_uq, kv_a_g, w_ukv, mla_q_g, mla_k_g, conv_w, conv_b,
                           lru_gate_w, lru_gate_b, lru_lambda, swa_q_g, swa_k_g, group_g, w_out,
                           norm2_g, w_ff1, w_ff2)

    xc = ctx
    for layer in range(depth):
        last = layer == depth - 1
        qm, km, vt, lx, lg, qs, ks, vts = _premix(x, mod, None, layer, wts, rope_m, rope_s)
        qm_c, km_c, vt_c, lx_c, lg_c, qs_c, ks_c, vts_c = _premix(xc, mod, b, layer, wts, no_rope, no_rope)

        o_b, o_b_c = _rglru(lx_c, lx, lg_c, lg, layer, wts)
        o_a = _mla_attention(qm, km_c, vt_c, km, vt)
        o_c = _swa_attention(qs, ks_c, vts_c, swa_sink, layer, ks, vts)
        x = _postmix(x, o_a, o_b, o_c, mod, None, layer, wts)
        if not last:
            o_a_c = _mla_attention(qm_c, km_c, vt_c)
            o_c_c = _swa_attention(qs_c, ks_c, vts_c, swa_sink, layer)
            xc = _postmix(xc, o_a_c, o_b_c, o_c_c, mod, b, layer, wts)
    return x
```

```python
import functools

import numpy as np
import jax
import jax.numpy as jnp
from jax import lax
from jax.experimental import pallas as pl
from jax.experimental.pallas import tpu as pltpu

F32 = jnp.float32
BF16 = jnp.bfloat16

GRID_W = 64
WINDOW = 128
ROPE_THETA = 10000.0
EPS = 1e-6
NEG_INF = -1e30
N_MOD = 6
MLA_HEADS = 8
MLA_NOPE = 64
MLA_ROPE = 32
MLA_QK = MLA_NOPE + MLA_ROPE
MLA_V = 64
MLA_Q_RANK = 256
MLA_KV_RANK = 128
LRU_WIDTH = 512
LRU_BLOCK_DIM = 64
LRU_C = 8.0
CONV_W = 4
SWA_HEADS = 8
SWA_KV_HEADS = 2
SWA_GROUP = SWA_HEADS // SWA_KV_HEADS
SWA_HEAD_DIM = 64
GROUP_WIDTH = 512

LANES = 128
SUBLANES = 8
V7X_VMEM_BYTES = 64 * 1024 * 1024
VMEM_LIMIT_BYTES = V7X_VMEM_BYTES - 8 * 1024 * 1024

MLA_QW = MLA_HEADS * LANES
MLA_VW = MLA_HEADS * MLA_V
MLA_HEADS_PER_STEP = 8
MLA_KEY_CHUNK = 512
LOG2_E = 1.4426950408889634
SWA_QW = SWA_HEADS * SWA_HEAD_DIM
SWA_KW = SWA_KV_HEADS * SWA_HEAD_DIM
SWA_PAIRS = SWA_QW // LANES
SWA_Q_BLOCKS_PER_STEP = 8
FF_CHUNK = 1024
PREMIX_ROWS = 512
LRU_GROUPS_PER_STEP = 2

OFF_CQ = 0
OFF_CKV = OFF_CQ + MLA_Q_RANK
OFF_KR = OFF_CKV + MLA_KV_RANK
SECTION_2 = OFF_KR + LANES
OFF_KRR = SECTION_2
OFF_SQ = OFF_KRR + LANES
OFF_SQR = OFF_SQ + SWA_QW
OFF_SK = OFF_SQR + SWA_QW
OFF_SKR = OFF_SK + SWA_KW
OFF_SV = OFF_SKR + SWA_KW
OFF_LX = OFF_SV + SWA_KW
OFF_LG = OFF_LX + LRU_WIDTH
W_IN_PAD = OFF_LG + LRU_WIDTH


def _dot(a, b):
    return jnp.dot(a, b, preferred_element_type=F32)


def _dot_nt(a, b):
    return lax.dot_general(a, b, (((1,), (1,)), ((), ())), preferred_element_type=F32)


def _rms(x):
    return x * lax.rsqrt(jnp.mean(x * x, axis=-1, keepdims=True) + EPS)


def _params(*sem, flags=None):
    return pltpu.CompilerParams(dimension_semantics=sem, vmem_limit_bytes=VMEM_LIMIT_BYTES, flags=flags)


def _layer_spec(arr, layer):
    index = (layer,) + (0,) * (arr.ndim - 1)
    return pl.BlockSpec((None,) + arr.shape[1:], lambda *_: index)


def _mod_spec(mod, layer, shared_row):
    if shared_row is None:
        return pl.BlockSpec((None, 1, 1, mod.shape[-1]), lambda bi, i: (layer, bi, 0, 0))
    return pl.BlockSpec((None, 1, 1, mod.shape[-1]), lambda bi, i: (layer, shared_row, 0, 0))


def _mod_kernel(c_ref, w_ref, b_ref, o_ref):
    c = c_ref[...]
    a = c * jax.nn.sigmoid(c)
    w = w_ref[0]
    a_hi = a.astype(BF16)
    a_lo = (a - a_hi.astype(F32)).astype(BF16)
    w_hi = w.astype(BF16)
    w_lo = (w - w_hi.astype(F32)).astype(BF16)
    o_ref[0] = _dot(a_hi, w_hi) + _dot(a_hi, w_lo) + _dot(a_lo, w_hi) + b_ref[0]


def _modulation(cc, w_mod, b_mod):
    depth, d, n = w_mod.shape
    rows = cc.shape[0]
    tn = n // 4
    return pl.pallas_call(
        _mod_kernel,
        grid=(depth, n // tn),
        in_specs=[
            pl.BlockSpec((rows, d), lambda l, j: (0, 0)),
            pl.BlockSpec((1, d, tn), lambda l, j: (l, 0, j)),
            pl.BlockSpec((1, 1, tn), lambda l, j: (l, 0, j)),
        ],
        out_specs=pl.BlockSpec((1, rows, tn), lambda l, j: (l, 0, j)),
        out_shape=jax.ShapeDtypeStruct((depth, rows, n), F32),
        compiler_params=_params("arbitrary", "arbitrary"),
        name="modulation",
    )(cc, w_mod, b_mod.reshape(depth, 1, n))


def _premix_kernel(x_ref, mod_ref, n1g_ref, win_ref, qag_ref, wuq_ref, kvag_ref, wuk_ref, wvt_ref,
                   gq_ref, gqr_ref, gk_ref, gkr_ref, gqs_ref, gqsr_ref, gks_ref, gksr_ref,
                   cm_ref, sm_ref, cs_ref, ss_ref,
                   qm_ref, km_ref, vt_ref, lx_ref, lg_ref, qs_ref, ks_ref, vts_ref):
    d = x_ref.shape[-1]
    tm = x_ref.shape[1]
    shift = mod_ref[0, :, 0:d]
    scale = mod_ref[0, :, d:2 * d]
    n = PREMIX_ROWS if tm % PREMIX_ROWS == 0 else tm
    for rows in (slice(r, r + n) for r in range(0, tm, n)):
        _premix_rows(rows, n, shift, scale, x_ref, n1g_ref, win_ref, qag_ref, wuq_ref, kvag_ref, wuk_ref, wvt_ref,
                     gq_ref, gqr_ref, gk_ref, gkr_ref, gqs_ref, gqsr_ref, gks_ref, gksr_ref,
                     cm_ref, sm_ref, cs_ref, ss_ref,
                     qm_ref, km_ref, vt_ref, lx_ref, lg_ref, qs_ref, ks_ref, vts_ref)


def _premix_rows(rows, n, shift, scale, x_ref, n1g_ref, win_ref, qag_ref, wuq_ref, kvag_ref, wuk_ref, wvt_ref,
                 gq_ref, gqr_ref, gk_ref, gkr_ref, gqs_ref, gqsr_ref, gks_ref, gksr_ref,
                 cm_ref, sm_ref, cs_ref, ss_ref,
                 qm_ref, km_ref, vt_ref, lx_ref, lg_ref, qs_ref, ks_ref, vts_ref):
    h = (_rms(x_ref[0, rows, :]) * n1g_ref[...] * (1.0 + scale) + shift).astype(BF16)

    cm, sm = cm_ref[rows, :], sm_ref[rows, :]
    cs, ss = cs_ref[rows, :], ss_ref[rows, :]

    p1 = _dot(h, win_ref[:, 0:SECTION_2])
    qn = (_rms(p1[:, OFF_CQ:OFF_CQ + MLA_Q_RANK]) * qag_ref[...]).astype(BF16)
    kvn = (_rms(p1[:, OFF_CKV:OFF_CKV + MLA_KV_RANK]) * kvag_ref[...]).astype(BF16)
    kr = p1[:, OFF_KR:OFF_KR + LANES]
    ss_kr = jnp.sum(kr * kr, axis=-1, keepdims=True)

    p2 = _dot(h, win_ref[:, SECTION_2:OFF_LX])
    qu = _dot(qn, wuq_ref[...])
    kvu = _dot(kvn, wuk_ref[...])
    vt_ref[0, :, rows] = _dot_nt(wvt_ref[...], kvn).astype(BF16)
    ga_k = gk_ref[...] * cm
    shared = kr * ga_k + p2[:, 0:LANES] * (gkr_ref[...] * sm)
    lo = lax.broadcasted_iota(jnp.int32, (n, LANES), 1) < SWA_HEAD_DIM

    def pair_norm_rope(off, off_rot, ga, gb):
        x1 = p2[:, off - SECTION_2:off - SECTION_2 + LANES]
        x2 = p2[:, off_rot - SECTION_2:off_rot - SECTION_2 + LANES]
        sq = x1 * x1
        s_lo = jnp.sum(jnp.where(lo, sq, 0.0), axis=-1, keepdims=True)
        s_hi = jnp.sum(jnp.where(lo, 0.0, sq), axis=-1, keepdims=True)
        inv = 1.0 / SWA_HEAD_DIM
        r = jnp.where(lo, lax.rsqrt(s_lo * inv + EPS), lax.rsqrt(s_hi * inv + EPS))
        return (r * (x1 * ga + x2 * gb)).astype(BF16)

    ga_s, gb_s = gqs_ref[...] * cs, gqsr_ref[...] * ss
    for g in range(SWA_PAIRS):
        qs_ref[0, rows, g * LANES:(g + 1) * LANES] = pair_norm_rope(OFF_SQ + g * LANES, OFF_SQR + g * LANES, ga_s, gb_s)
    ks_ref[0, rows, :] = pair_norm_rope(OFF_SK, OFF_SKR, gks_ref[...] * cs, gksr_ref[...] * ss)
    vts_ref[0, :, rows] = p2[:, OFF_SV - SECTION_2:OFF_SV - SECTION_2 + SWA_KW].T.astype(BF16)

    p3 = _dot(h, win_ref[:, OFF_LX:W_IN_PAD])
    ga_q, gb_q = gq_ref[...] * cm, gqr_ref[...] * sm
    for hd in range(MLA_HEADS):
        x1 = qu[:, hd * LANES:(hd + 1) * LANES]
        x2 = qu[:, MLA_QW + hd * LANES:MLA_QW + (hd + 1) * LANES]
        r = lax.rsqrt(jnp.sum(x1 * x1, axis=-1, keepdims=True) * (1.0 / MLA_QK) + EPS)
        qm_ref[0, rows, hd * LANES:(hd + 1) * LANES] = (r * (x1 * ga_q + x2 * gb_q)).astype(BF16)
    for hd in range(MLA_HEADS):
        x1 = kvu[:, hd * LANES:(hd + 1) * LANES]
        r = lax.rsqrt((jnp.sum(x1 * x1, axis=-1, keepdims=True) + ss_kr) * (1.0 / MLA_QK) + EPS)
        km_ref[0, rows, hd * LANES:(hd + 1) * LANES] = (r * (x1 * ga_k + shared)).astype(BF16)
    lx_ref[0, rows, :] = p3[:, 0:LRU_WIDTH]
    lg_ref[0, rows, :] = p3[:, LRU_WIDTH:2 * LRU_WIDTH]


def _premix(x, mod, shared_row, layer, wts, rope_m, rope_s):
    b, l, d = x.shape
    tm = min(l, 1024)
    consts = [wts[k] for k in ("n1g", "w_in", "qag", "w_uq", "kvag", "w_uk", "w_vt",
                               "gq", "gq_rot", "gk", "gk_rot", "gqs", "gqs_rot", "gks", "gks_rot")]
    tables = list(rope_m) + list(rope_s)
    tok = lambda w, dt: (pl.BlockSpec((1, tm, w), lambda bi, i: (bi, i, 0)), jax.ShapeDtypeStruct((b, l, w), dt))
    tr = lambda w: (pl.BlockSpec((1, w, tm), lambda bi, i: (bi, 0, i)), jax.ShapeDtypeStruct((b, w, l), BF16))
    outs = [tok(MLA_QW, BF16), tok(MLA_QW, BF16), tr(MLA_VW), tok(LRU_WIDTH, F32), tok(LRU_WIDTH, F32),
            tok(SWA_QW, BF16), tok(SWA_KW, BF16), tr(SWA_KW)]
    return pl.pallas_call(
        _premix_kernel,
        grid=(b, l // tm),
        in_specs=[tok(d, F32)[0], _mod_spec(mod, layer, shared_row)]
        + [_layer_spec(c, layer) for c in consts]
        + [pl.BlockSpec((tm, LANES), lambda bi, i: (i, 0)) for _ in tables],
        out_specs=[o[0] for o in outs],
        out_shape=[o[1] for o in outs],
        compiler_params=_params("arbitrary", "arbitrary"),
        name="premix",
    )(x, mod, *consts, *tables)


def _gelu_tanh(x):
    c = 0.7978845608028654
    half_x = 0.5 * x
    return half_x * jnp.tanh(x * (c + (c * 0.044715) * (x * x))) + half_x


LRU_SCRATCH_PER_GROUP = 5


def _lru_kernel(lxc_ref, lxl_ref, lgc_ref, lgl_ref, cw_ref, cb_ref, wg_ref, bg_ref, lam_ref,
                obl_ref, obc_ref, *scratch):
    lc = lxc_ref.shape[1]
    ll = lxl_ref.shape[1]
    groups = [scratch[i:i + LRU_SCRATCH_PER_GROUP] for i in range(0, len(scratch), LRU_SCRATCH_PER_GROUP)]
    pitch = scratch[0].shape[0] // SUBLANES
    tail = scratch[0].shape[0] - (lc + ll)
    assert tail > 0
    assert pitch % 2 == 0

    def coefficients(gi, x, xs):
        lanes = slice(gi * LANES, (gi + 1) * LANES)
        cw = cw_ref[:, lanes]
        nlam = -lam_ref[:, lanes]
        softplus = jnp.maximum(nlam, 0.0) + jnp.log1p(jnp.exp(-jnp.abs(nlam)))
        decay = (-0.5 * LRU_C * LOG2_E) * softplus
        t = x.shape[0]
        pad = jnp.zeros((SUBLANES, LANES), F32)
        xs[0:SUBLANES, :] = pad
        xs[SUBLANES:SUBLANES + t, :] = x
        xs[SUBLANES + t:2 * SUBLANES + t, :] = pad
        xm2, xm1, xp1 = (xs[SUBLANES + o:SUBLANES + o + t, :] for o in (-2, -1, 1))
        xc = cw[0:1] * xm2 + cw[1:2] * xm1 + cw[2:3] * x + cw[3:4] * xp1 + cb_ref[:, lanes]
        th = jnp.tanh(_dot(xc.astype(BF16), wg_ref[gi]) + bg_ref[gi])
        half_x = 0.5 * xc
        out = []
        for z in range(2):
            t_r = th[:, 2 * z * LANES:(2 * z + 1) * LANES]
            t_i = th[:, (2 * z + 1) * LANES:(2 * z + 2) * LANES]
            a = jnp.exp2(decay[z:z + 1] * t_r + decay[z:z + 1])
            y = 1.0 - a * a
            root = y * lax.rsqrt(jnp.maximum(y, jnp.finfo(F32).tiny))
            out.append((a, root * (half_x * t_i + half_x)))
        return out

    for gi, (af, uf, ab, ub, xs) in enumerate(groups):
        for x_ref, f_off, b_off in ((lxc_ref, 0, ll), (lxl_ref, lc, 0)):
            t = x_ref.shape[1]
            (a_f, u_f), (a_b, u_b) = coefficients(gi, x_ref[0, :, gi * LANES:(gi + 1) * LANES], xs)
            af[f_off:f_off + t, :] = a_f
            uf[f_off:f_off + t, :] = u_f
            ab[b_off:b_off + t, :] = a_b
            ub[b_off:b_off + t, :] = u_b
        for a_ref, u_ref in ((af, uf), (ab, ub)):
            a_ref[lc + ll:, :] = jnp.ones((tail, LANES), F32)
            u_ref[lc + ll:, :] = jnp.zeros((tail, LANES), F32)

    def streams(i):
        return pl.ds(i, SUBLANES, stride=pitch)

    def two_steps(a_ref, u_ref, i0, i1, h, p):
        a0, u0 = a_ref[streams(i0), :], u_ref[streams(i0), :]
        a1, u1 = a_ref[streams(i1), :], u_ref[streams(i1), :]
        a01 = a1 * a0
        u_ref[streams(i0), :] = a0 * h + u0
        a_ref[streams(i0), :] = a0 * p
        h = a01 * h + (a1 * u0 + u1)
        p = a01 * p
        u_ref[streams(i1), :] = h
        a_ref[streams(i1), :] = p
        return h, p

    def local_scan(i, state):
        out = []
        for (af, uf, ab, ub, _), (h_f, p_f, h_b, p_b) in zip(groups, state):
            h_f, p_f = two_steps(af, uf, 2 * i, 2 * i + 1, h_f, p_f)
            h_b, p_b = two_steps(ab, ub, pitch - 1 - 2 * i, pitch - 2 - 2 * i, h_b, p_b)
            out.append((h_f, p_f, h_b, p_b))
        return tuple(out)

    zeros8 = jnp.zeros((SUBLANES, LANES), F32)
    ones8 = jnp.ones((SUBLANES, LANES), F32)
    finals = lax.fori_loop(0, pitch // 2, local_scan, ((zeros8, ones8, zeros8, ones8),) * len(groups), unroll=2)

    sub = lax.broadcasted_iota(jnp.int32, (SUBLANES, LANES), 0)
    carries = []
    for h_f, p_f, h_b, p_b in finals:
        c_f = zeros8
        c_b = zeros8
        row_f = jnp.zeros((1, LANES), F32)
        row_b = jnp.zeros((1, LANES), F32)
        for s in range(1, SUBLANES):
            row_f = h_f[s - 1:s] + p_f[s - 1:s] * row_f
            c_f = jnp.where(sub == s, row_f, c_f)
            sb = SUBLANES - 1 - s
            row_b = h_b[sb + 1:sb + 2] + p_b[sb + 1:sb + 2] * row_b
            c_b = jnp.where(sub == sb, row_b, c_b)
        carries.append((c_f, c_b))

    def add_carry(i, _):
        for (af, uf, ab, ub, _), (c_f, c_b) in zip(groups, carries):
            uf[streams(i), :] = uf[streams(i), :] + af[streams(i), :] * c_f
            ub[streams(i), :] = ub[streams(i), :] + ab[streams(i), :] * c_b
        return 0

    lax.fori_loop(0, pitch, add_carry, 0, unroll=4)

    for gi, (_, uf, _, ub, _) in enumerate(groups):
        lanes = slice(gi * LANES, (gi + 1) * LANES)
        obl_ref[0, :, lanes] = (uf[lc:lc + ll, :] + ub[0:ll, :]) * _gelu_tanh(lgl_ref[0, :, lanes])
        obc_ref[0, :, lanes] = (uf[0:lc, :] + ub[ll:ll + lc, :]) * _gelu_tanh(lgc_ref[0, :, lanes])


def _rglru(lx_c, lx_l, lg_c, lg_l, layer, wts):
    b, lc, w = lx_c.shape
    ll = lx_l.shape[1]
    gps = LRU_GROUPS_PER_STEP
    bw = gps * LANES
    pitch = -(-(lc + ll) // SUBLANES)
    pitch += (SUBLANES // 2 - pitch) % SUBLANES
    seq = lambda t: pl.BlockSpec((1, t, bw), lambda bi, g: (bi, 0, g))
    group_scratch = [pltpu.VMEM((SUBLANES * pitch, LANES), F32) for _ in range(LRU_SCRATCH_PER_GROUP - 1)]
    group_scratch.append(pltpu.VMEM((max(lc, ll) + 2 * SUBLANES, LANES), F32))
    return pl.pallas_call(
        _lru_kernel,
        grid=(b, w // bw),
        in_specs=[seq(lc), seq(ll), seq(lc), seq(ll),
                  pl.BlockSpec((None, CONV_W, bw), lambda bi, g: (layer, 0, g)),
                  pl.BlockSpec((None, 1, bw), lambda bi, g: (layer, 0, g)),
                  pl.BlockSpec((None, gps, LANES, 4 * LANES), lambda bi, g: (layer, g, 0, 0)),
                  pl.BlockSpec((None, gps, 1, 4 * LANES), lambda bi, g: (layer, g, 0, 0)),
                  pl.BlockSpec((None, 2, bw), lambda bi, g: (layer, 0, g))],
        out_specs=[seq(ll), seq(lc)],
        out_shape=[jax.ShapeDtypeStruct((b, ll, w), F32), jax.ShapeDtypeStruct((b, lc, w), F32)],
        scratch_shapes=group_scratch * gps,
        compiler_params=_params("arbitrary", "arbitrary"),
        name="rglru",
    )(lx_c, lx_l, lg_c, lg_l, wts["conv_w"], wts["conv_b"], wts["w_gate"], wts["b_gate"], wts["lam"])


def _mla_kernel(*refs, with_latent):
    if with_latent:
        q_ref, kc_ref, vtc_ref, kl_ref, vtl_ref, o_ref, s_buf = refs
        sources = ((kc_ref, vtc_ref), (kl_ref, vtl_ref))
    else:
        q_ref, kc_ref, vtc_ref, o_ref, s_buf = refs
        sources = ((kc_ref, vtc_ref),)
    tq = q_ref.shape[1]
    chunks, row = [], 0
    for k_ref, vt_ref in sources:
        n = k_ref.shape[1]
        for k0 in range(0, n, MLA_KEY_CHUNK):
            kn = min(MLA_KEY_CHUNK, n - k0)
            chunks.append((k_ref, vt_ref, k0, kn, row))
            row += kn

    def add(acc, x, op=jnp.add):
        return x if acc is None else op(acc, x)

    slot0 = jnp.minimum(pl.program_id(0), 0)

    def score_chunk(hd, chunk, m8):
        k_ref, _, k0, kn, r0 = chunk
        sl = slice(hd * LANES, (hd + 1) * LANES)
        s = _dot_nt(k_ref[0, k0:k0 + kn, sl], q_ref[0, :, sl])
        s_buf[hd % 2 + slot0, r0:r0 + kn, :] = s
        return add(m8, jnp.max(s.reshape(kn // SUBLANES, SUBLANES, tq), axis=0), jnp.maximum)

    def attend_chunk(hd, chunk, m, l8, o_t):
        _, vt_ref, k0, kn, r0 = chunk
        p = jnp.exp2(s_buf[hd % 2 + slot0, r0:r0 + kn, :] - m)
        l8 = add(l8, jnp.sum(p.reshape(kn // SUBLANES, SUBLANES, tq), axis=0))
        o_t = add(o_t, _dot(vt_ref[0, hd * MLA_V:(hd + 1) * MLA_V, k0:k0 + kn], p.astype(BF16)))
        return l8, o_t

    outs = []
    m8 = None
    for chunk in chunks:
        m8 = score_chunk(0, chunk, m8)
    for hd in range(MLA_HEADS_PER_STEP):
        m = jnp.max(m8, axis=0, keepdims=True)
        m8, l8, o_t = None, None, None
        if hd + 1 < MLA_HEADS_PER_STEP:
            for chunk in chunks:
                m8 = score_chunk(hd + 1, chunk, m8)
        for chunk in chunks:
            l8, o_t = attend_chunk(hd, chunk, m, l8, o_t)
        outs.append(o_t * (1.0 / jnp.sum(l8, axis=0, keepdims=True)))
    o_ref[0] = jnp.concatenate(outs, axis=0).T.astype(o_ref.dtype)


def _mla_attention(q, k_c, vt_c, k_l=None, vt_l=None):
    b, lq, _ = q.shape
    lc = k_c.shape[1]
    with_latent = k_l is not None
    tq = min(lq, 512)
    qw = MLA_HEADS_PER_STEP * LANES
    vw = MLA_HEADS_PER_STEP * MLA_V
    nq = MLA_HEADS // MLA_HEADS_PER_STEP
    keys = lambda t: pl.BlockSpec((1, t, qw), lambda bi, hq, i: (bi, 0, hq))
    vals = lambda t: pl.BlockSpec((1, vw, t), lambda bi, hq, i: (bi, hq, 0))
    in_specs = [pl.BlockSpec((1, tq, qw), lambda bi, hq, i: (bi, i, hq)), keys(lc), vals(lc)]
    args = [q, k_c, vt_c]
    if with_latent:
        in_specs += [keys(k_l.shape[1]), vals(k_l.shape[1])]
        args += [k_l, vt_l]
    return pl.pallas_call(
        functools.partial(_mla_kernel, with_latent=with_latent),
        grid=(b, nq, lq // tq),
        in_specs=in_specs,
        out_specs=pl.BlockSpec((1, tq, vw), lambda bi, hq, i: (bi, i, hq)),
        out_shape=jax.ShapeDtypeStruct((b, lq, MLA_VW), BF16),
        scratch_shapes=[pltpu.VMEM((2, lc + (k_l.shape[1] if with_latent else 0), tq), F32)],
        compiler_params=_params("arbitrary", "arbitrary", "arbitrary"),
        name="mla_attention" if with_latent else "mla_attention_ctx",
    )(*args)


def _swa_kernel(*refs, with_window, layer):
    if with_window:
        sink_ref, q_ref, kc_ref, vtc_ref, kl_ref, vtl_ref, o_ref, s_buf = refs
    else:
        sink_ref, q_ref, kc_ref, vtc_ref, o_ref, s_buf = refs
    bq = WINDOW
    cols = SWA_PAIRS * bq
    kc = kc_ref[0]
    lc = kc.shape[0]
    slot0 = jnp.minimum(pl.program_id(0), 0)
    lo = lax.broadcasted_iota(jnp.int32, (cols, LANES), 1) < SWA_HEAD_DIM
    seg = lax.broadcasted_iota(jnp.int32, (1, cols), 1) // bq

    span = bq + 2 * WINDOW
    sinks = []
    for kvh in range(SWA_KV_HEADS):
        sink = jnp.zeros((1, cols), F32)
        for g in range(SWA_PAIRS):
            sink = jnp.where(seg == g, sink_ref[layer, kvh * SWA_GROUP + g] * LOG2_E, sink)
        sinks.append(sink)

    def scores(sb):
        r0 = sb * bq
        q = jnp.concatenate([q_ref[0, r0:r0 + bq, g * LANES:(g + 1) * LANES] for g in range(SWA_PAIRS)],
                            axis=0).astype(F32)
        w0 = None
        if with_window:
            q0 = pl.program_id(1) * q_ref.shape[1] + r0
            w0 = pl.multiple_of(jnp.clip(q0 - WINDOW, 0, kl_ref.shape[1] - span), LANES)
            kw = kl_ref[0, pl.ds(w0, span), :]
            kj = w0 + lax.broadcasted_iota(jnp.int32, (span, cols), 0)
            qi = q0 + (lax.broadcasted_iota(jnp.int32, (span, cols), 1) & (bq - 1))
            band = jnp.abs(qi - kj) <= WINDOW
        maxes = []
        for kvh in range(SWA_KV_HEADS):
            qh = (jnp.where(lo, 0.0, q) if kvh else jnp.where(lo, q, 0.0)).astype(BF16)
            s = [_dot_nt(kc, qh)]
            if with_window:
                s.append(jnp.where(band, _dot_nt(kw, qh), NEG_INF))
            m, row = sinks[kvh], 0
            for x in s:
                s_buf[sb % 2 + slot0, kvh, row:row + x.shape[0], :] = x
                m = jnp.maximum(m, jnp.max(x, axis=0, keepdims=True))
                row += x.shape[0]
            maxes.append(m)
        return w0, maxes

    def attend(sb, w0, maxes):
        o_t = []
        for kvh, m in enumerate(maxes):
            sink = sinks[kvh]
            rows = [(0, lc)] + ([(lc, lc + span)] if with_window else [])
            p = [jnp.exp2(s_buf[sb % 2 + slot0, kvh, a:b, :] - m) for a, b in rows]
            denom = sum(jnp.sum(x, axis=0, keepdims=True) for x in p) + jnp.exp2(sink - m)
            vs = slice(kvh * SWA_HEAD_DIM, (kvh + 1) * SWA_HEAD_DIM)
            o = _dot(vtc_ref[0, vs, :], p[0].astype(BF16))
            if with_window:
                o = o + _dot(vtl_ref[0, vs, pl.ds(w0, span)], p[1].astype(BF16))
            o_t.append(o * (1.0 / denom))
        out = jnp.concatenate(o_t, axis=0).T.astype(o_ref.dtype)
        for g in range(SWA_PAIRS):
            o_ref[0, sb * bq:(sb + 1) * bq, g * LANES:(g + 1) * LANES] = out[g * bq:(g + 1) * bq]

    n_blocks = q_ref.shape[1] // bq
    nxt = scores(0)
    for sb in range(n_blocks):
        cur = nxt
        if sb + 1 < n_blocks:
            nxt = scores(sb + 1)
        attend(sb, *cur)


def _swa_attention(q, k_c, vt_c, sink, layer, k_l=None, vt_l=None):
    b, lq, _ = q.shape
    lc = k_c.shape[1]
    with_window = k_l is not None
    tq = min(lq, SWA_Q_BLOCKS_PER_STEP * WINDOW)
    keys = lambda t: pl.BlockSpec((1, t, SWA_KW), lambda bi, i: (bi, 0, 0))
    vals = lambda t: pl.BlockSpec((1, SWA_KW, t), lambda bi, i: (bi, 0, 0))
    in_specs = [pl.BlockSpec(memory_space=pltpu.SMEM),
                pl.BlockSpec((1, tq, SWA_QW), lambda bi, i: (bi, i, 0)), keys(lc), vals(lc)]
    args = [sink, q, k_c, vt_c]
    if with_window:
        in_specs += [keys(k_l.shape[1]), vals(k_l.shape[1])]
        args += [k_l, vt_l]
    return pl.pallas_call(
        functools.partial(_swa_kernel, with_window=with_window, layer=layer),
        grid=(b, lq // tq),
        in_specs=in_specs,
        out_specs=pl.BlockSpec((1, tq, SWA_QW), lambda bi, i: (bi, i, 0)),
        out_shape=jax.ShapeDtypeStruct((b, lq, SWA_QW), BF16),
        scratch_shapes=[pltpu.VMEM((2, SWA_KV_HEADS, lc + (3 * WINDOW if with_window else 0), SWA_PAIRS * WINDOW), F32)],
        compiler_params=_params("arbitrary", "arbitrary"),
        name="swa_attention" if with_window else "swa_attention_ctx",
    )(*args)


def _postmix_kernel(x_ref, oa_ref, ob_ref, oc_ref, mod_ref, gg_ref, wout_ref, n2g_ref, w1_ref, w2_ref, o_ref):
    d = x_ref.shape[-1]
    gate1 = mod_ref[0, :, 2 * d:3 * d]
    shift2 = mod_ref[0, :, 3 * d:4 * d]
    scale2 = mod_ref[0, :, 4 * d:5 * d]
    gate2 = mod_ref[0, :, 5 * d:6 * d]
    tm = x_ref.shape[1]
    halves = [slice(r, r + tm // 2) for r in (0, tm // 2)]

    def mix(rows):
        y = None
        for gi, ref in enumerate((oa_ref, ob_ref, oc_ref)):
            sl = slice(gi * GROUP_WIDTH, (gi + 1) * GROUP_WIDTH)
            og = _rms(ref[0, rows, :].astype(F32)) * gg_ref[:, sl]
            part = _dot(og.astype(BF16), wout_ref[sl, :])
            y = part if y is None else y + part
        return y

    def residual_norm(rows, y):
        x1 = x_ref[0, rows, :] + gate1 * y
        return x1, (_rms(x1) * n2g_ref[...] * (1.0 + scale2) + shift2).astype(BF16)

    def mlp(h2):
        ff = None
        for j in range(w1_ref.shape[1] // FF_CHUNK):
            sl = slice(j * FF_CHUNK, (j + 1) * FF_CHUNK)
            hid = jnp.maximum(_dot(h2, w1_ref[:, sl]), 0.0)
            part = _dot((hid * hid).astype(BF16), w2_ref[sl, :])
            ff = part if ff is None else ff + part
        return ff

    y = [mix(rows) for rows in halves]
    for rows, y_half in zip(halves, y):
        x1, h2 = residual_norm(rows, y_half)
        o_ref[0, rows, :] = x1 + gate2 * mlp(h2)


def _postmix(x, o_a, o_b, o_c, mod, shared_row, layer, wts):
    b, l, d = x.shape
    tm = min(l, 512)
    tok = lambda w: pl.BlockSpec((1, tm, w), lambda bi, i: (bi, i, 0))
    consts = [wts["gg"], wts["w_out"], wts["n2g"], wts["w_ff1"], wts["w_ff2"]]
    return pl.pallas_call(
        _postmix_kernel,
        grid=(b, l // tm),
        in_specs=[tok(d), tok(GROUP_WIDTH), tok(GROUP_WIDTH), tok(GROUP_WIDTH), _mod_spec(mod, layer, shared_row)]
        + [_layer_spec(c, layer) for c in consts],
        out_specs=tok(d),
        out_shape=jax.ShapeDtypeStruct((b, l, d), F32),
        compiler_params=_params("arbitrary", "arbitrary"),
        name="postmix",
    )(x, o_a, o_b, o_c, mod, *consts)


def _rot_sign(dim):
    quarter = dim // 4
    return np.where((np.arange(dim) // quarter) % 2 == 0, -1.0, 1.0).astype(np.float32)


def _rot_partner(w, dim):
    return jnp.flip(w.reshape(w.shape[:-1] + (-1, 2, dim // 4)), axis=-2).reshape(w.shape)


def _pair_heads(w, axis):
    axis = axis % w.ndim
    shape = w.shape
    w = w.reshape(shape[:axis] + (SWA_KV_HEADS, SWA_PAIRS, SWA_HEAD_DIM) + shape[axis + 1:])
    return jnp.swapaxes(w, axis, axis + 1).reshape(shape)


def _rope_tables(rows, dim, lead, width):
    quarter = dim // 4
    n = rows * GRID_W
    row = jnp.repeat(jnp.arange(rows), GRID_W)
    col = jnp.tile(jnp.arange(GRID_W), rows)
    inv_freq = ROPE_THETA ** (-jnp.arange(quarter, dtype=F32) / quarter)
    ang = jnp.stack([row, col], axis=-1).astype(F32)[:, :, None] * inv_freq
    ang = jnp.broadcast_to(ang[:, :, None, :], (n, 2, 2, quarter)).reshape(n, dim)
    sign = _rot_sign(dim)

    def place(t, fill):
        t = jnp.concatenate([jnp.full((n, lead), fill, F32), t], axis=1)
        t = jnp.tile(t, (1, width // (lead + dim)))
        return jnp.concatenate([t, jnp.full((n, width - t.shape[1]), fill, F32)], axis=1)

    return place(jnp.cos(ang), 1.0), place(jnp.sin(ang) * sign, 0.0)


def _identity_tables(n, width):
    return jnp.ones((n, width), F32), jnp.zeros((n, width), F32)


def _pad_heads(w, heads, width, lead=0):
    shape = w.shape[:-1]
    w = w.reshape(shape + (heads, width))
    w = jnp.pad(w, [(0, 0)] * len(shape) + [(0, 0), (lead, LANES - lead - width)])
    return w.reshape(shape + (heads * LANES,))


def _split_in(w):
    sizes = (MLA_Q_RANK, MLA_KV_RANK, MLA_ROPE, LRU_WIDTH, LRU_WIDTH, SWA_QW, SWA_KW, SWA_KW)
    parts, start = [], 0
    for s in sizes:
        parts.append(w[..., start:start + s])
        start += s
    return parts


def _prepare_weights(norm1_g, w_in, q_a_g, w_uq, kv_a_g, w_ukv, mla_q_g, mla_k_g, conv_w, conv_b,
                     lru_gate_w, lru_gate_b, lru_lambda, swa_q_g, swa_k_g, group_g, w_out,
                     norm2_g, w_ff1, w_ff2):
    depth = w_in.shape[0]
    row = lambda v: v[:, None, :]
    cq, ckv, kr, lx, lg, sq, sk, sv = _split_in(w_in)
    sq = _pair_heads(sq, -1)
    w_in_p = jnp.concatenate(
        [cq, ckv, _pad_heads(kr, 1, MLA_ROPE, MLA_NOPE), _pad_heads(_rot_partner(kr, MLA_ROPE), 1, MLA_ROPE, MLA_NOPE),
         sq, _rot_partner(sq, SWA_HEAD_DIM), sk, _rot_partner(sk, SWA_HEAD_DIM), sv, lx, lg], axis=-1)
    uq = w_uq.reshape(depth, MLA_Q_RANK, MLA_HEADS, MLA_QK)
    uq_rot = _rot_partner(uq[..., MLA_NOPE:], MLA_ROPE).reshape(depth, MLA_Q_RANK, MLA_HEADS * MLA_ROPE)
    w_uq_p = jnp.concatenate([_pad_heads(w_uq, MLA_HEADS, MLA_QK),
                              _pad_heads(uq_rot, MLA_HEADS, MLA_ROPE, MLA_NOPE)], axis=-1)
    kv = w_ukv.reshape(depth, MLA_KV_RANK, MLA_HEADS, MLA_NOPE + MLA_V)
    w_k = _pad_heads(kv[..., :MLA_NOPE].reshape(depth, MLA_KV_RANK, MLA_HEADS * MLA_NOPE), MLA_HEADS, MLA_NOPE)
    w_vt = jnp.swapaxes(kv[..., MLA_NOPE:].reshape(depth, MLA_KV_RANK, MLA_VW), 1, 2)
    nblk = lru_gate_w.shape[3]
    per_group = LANES // LRU_BLOCK_DIM
    groups = nblk // per_group
    eye = jnp.eye(per_group, dtype=F32)
    gw = lru_gate_w.reshape(depth, 2, 2, groups, per_group, LRU_BLOCK_DIM, LRU_BLOCK_DIM)
    w_gate = 0.5 * jnp.einsum("dzgpncm,nk->dpnczgkm", gw, eye).reshape(depth, groups, LANES, 4 * LANES)
    b_gate = 0.5 * lru_gate_b.reshape(depth, 2, 2, groups, LANES).transpose(0, 3, 1, 2, 4).reshape(
        depth, groups, 1, 4 * LANES)
    gq = mla_q_g * (MLA_QK ** -0.5 * LOG2_E)
    gqs = swa_q_g * (SWA_HEAD_DIM ** -0.5 * LOG2_E)
    rope_gain = lambda g: _pad_heads(row(_rot_partner(g[:, MLA_NOPE:], MLA_ROPE)), 1, MLA_ROPE, MLA_NOPE)
    pair = lambda g: row(jnp.tile(g, (1, LANES // SWA_HEAD_DIM)))
    c0 = 2 * GROUP_WIDTH
    gg = jnp.concatenate([group_g[:, :c0], _pair_heads(group_g[:, c0:], 1)], axis=1)
    w_o = jnp.concatenate([w_out[:, :c0], _pair_heads(w_out[:, c0:], 1)], axis=1)
    return {
        "n1g": row(norm1_g), "w_in": w_in_p.astype(BF16), "qag": row(q_a_g),
        "w_uq": w_uq_p.astype(BF16), "kvag": row(kv_a_g),
        "w_uk": w_k.astype(BF16), "w_vt": w_vt.astype(BF16),
        "gq": _pad_heads(row(gq), 1, MLA_QK), "gq_rot": rope_gain(gq),
        "gk": _pad_heads(row(mla_k_g), 1, MLA_QK), "gk_rot": rope_gain(mla_k_g),
        "gqs": pair(gqs), "gqs_rot": pair(_rot_partner(gqs, SWA_HEAD_DIM)),
        "gks": pair(swa_k_g), "gks_rot": pair(_rot_partner(swa_k_g, SWA_HEAD_DIM)),
        "conv_w": conv_w, "conv_b": row(conv_b), "w_gate": w_gate.astype(BF16), "b_gate": b_gate,
        "lam": lru_lambda,
        "gg": row(gg), "w_out": w_o.astype(BF16), "n2g": row(norm2_g),
        "w_ff1": w_ff1.astype(BF16), "w_ff2": w_ff2.astype(BF16),
    }


def kernel(x, c, ctx, c_ctx, w_mod, b_mod, norm1_g, w_in, q_a_g, w_uq, kv_a_g, w_ukv, mla_q_g, mla_k_g, conv_w, conv_b, lru_gate_w, lru_gate_b, lru_lambda, swa_q_g, swa_k_g, swa_sink, group_g, w_out, norm2_g, w_ff1, w_ff2):
    b, l, d = x.shape
    lc = ctx.shape[1]
    depth = w_mod.shape[0]
    rows = l // GRID_W

    n_rows = -(-(b + 1) // SUBLANES) * SUBLANES
    cc = jnp.concatenate([c, c_ctx[None], jnp.zeros((n_rows - b - 1, d), F32)], axis=0)
    mod = _modulation(cc, w_mod, b_mod).reshape(depth, n_rows, 1, N_MOD * d)

    rope_m = _rope_tables(rows, MLA_ROPE, MLA_NOPE, LANES)
    rope_s = _rope_tables(rows, SWA_HEAD_DIM, 0, LANES)
    no_rope = _identity_tables(lc, LANES)
    wts = _prepare_weights(norm1_g, w_in, q_a_g, w_uq, kv_a_g, w_ukv, mla_q_g, mla_k_g, conv_w, conv_b,
                           lru_gate_w, lru_gate_b, lru_lambda, swa_q_g, swa_k_g, group_g, w_out,
                           norm2_g, w_ff1, w_ff2)

    xc = ctx
    for layer in range(depth):
        last = layer == depth - 1
        qm, km, vt, lx, lg, qs, ks, vts = _premix(x, mod, None, layer, wts, rope_m, rope_s)
        qm_c, km_c, vt_c, lx_c, lg_c, qs_c, ks_c, vts_c = _premix(xc, mod, b, layer, wts, no_rope, no_rope)

        o_b, o_b_c = _rglru(lx_c, lx, lg_c, lg, layer, wts)
        o_a = _mla_attention(qm, km_c, vt_c, km, vt)
        o_c = _swa_attention(qs, ks_c, vts_c, swa_sink, layer, ks, vts)
        x = _postmix(x, o_a, o_b, o_c, mod, None, layer, wts)
        if not last:
            o_a_c = _mla_attention(qm_c, km_c, vt_c)
            o_c_c = _swa_attention(qs_c, ks_c, vts_c, swa_sink, layer)
            xc = _postmix(xc, o_a_c, o_b_c, o_c_c, mod, b, layer, wts)
    return x
```

```python
import functools

import numpy as np
import jax
import jax.numpy as jnp
from jax import lax
from jax.experimental import pallas as pl
from jax.experimental.pallas import tpu as pltpu

F32 = jnp.float32
BF16 = jnp.bfloat16

GRID_W = 64
WINDOW = 128
ROPE_THETA = 10000.0
EPS = 1e-6
NEG_INF = -1e30
N_MOD = 6
MLA_HEADS = 8
MLA_NOPE = 64
MLA_ROPE = 32
MLA_QK = MLA_NOPE + MLA_ROPE
MLA_V = 64
MLA_Q_RANK = 256
MLA_KV_RANK = 128
LRU_WIDTH = 512
LRU_BLOCK_DIM = 64
LRU_C = 8.0
CONV_W = 4
SWA_HEADS = 8
SWA_KV_HEADS = 2
SWA_GROUP = SWA_HEADS // SWA_KV_HEADS
SWA_HEAD_DIM = 64
GROUP_WIDTH = 512

LANES = 128
SUBLANES = 8
V7X_VMEM_BYTES = 64 * 1024 * 1024
VMEM_LIMIT_BYTES = V7X_VMEM_BYTES - 8 * 1024 * 1024

MLA_QW = MLA_HEADS * LANES
MLA_VW = MLA_HEADS * MLA_V
MLA_HEADS_PER_STEP = 8
MLA_KEY_CHUNK = 512
LOG2_E = 1.4426950408889634
SWA_QW = SWA_HEADS * SWA_HEAD_DIM
SWA_KW = SWA_KV_HEADS * SWA_HEAD_DIM
SWA_PAIRS = SWA_QW // LANES
SWA_Q_BLOCKS_PER_STEP = 8
FF_CHUNK = 1024
PREMIX_ROWS = 512
LRU_GROUPS_PER_STEP = 2

OFF_CQ = 0
OFF_CKV = OFF_CQ + MLA_Q_RANK
OFF_KR = OFF_CKV + MLA_KV_RANK
SECTION_2 = OFF_KR + LANES
OFF_KRR = SECTION_2
OFF_SQ = OFF_KRR + LANES
OFF_SQR = OFF_SQ + SWA_QW
OFF_SK = OFF_SQR + SWA_QW
OFF_SKR = OFF_SK + SWA_KW
OFF_SV = OFF_SKR + SWA_KW
OFF_LX = OFF_SV + SWA_KW
OFF_LG = OFF_LX + LRU_WIDTH
W_IN_PAD = OFF_LG + LRU_WIDTH


def _dot(a, b):
    return jnp.dot(a, b, preferred_element_type=F32)


def _dot_nt(a, b):
    return lax.dot_general(a, b, (((1,), (1,)), ((), ())), preferred_element_type=F32)


def _rms(x):
    return x * lax.rsqrt(jnp.mean(x * x, axis=-1, keepdims=True) + EPS)


def _params(*sem, flags=None):
    return pltpu.CompilerParams(dimension_semantics=sem, vmem_limit_bytes=VMEM_LIMIT_BYTES, flags=flags)


def _layer_spec(arr, layer):
    index = (layer,) + (0,) * (arr.ndim - 1)
    return pl.BlockSpec((None,) + arr.shape[1:], lambda *_: index)


def _mod_spec(mod, layer, shared_row):
    if shared_row is None:
        return pl.BlockSpec((None, 1, 1, mod.shape[-1]), lambda bi, i: (layer, bi, 0, 0))
    return pl.BlockSpec((None, 1, 1, mod.shape[-1]), lambda bi, i: (layer, shared_row, 0, 0))


def _mod_kernel(c_ref, w_ref, b_ref, o_ref):
    c = c_ref[...]
    a = c * jax.nn.sigmoid(c)
    w = w_ref[0]
    a_hi = a.astype(BF16)
    a_lo = (a - a_hi.astype(F32)).astype(BF16)
    w_hi = w.astype(BF16)
    w_lo = (w - w_hi.astype(F32)).astype(BF16)
    o_ref[0] = _dot(a_hi, w_hi) + _dot(a_hi, w_lo) + _dot(a_lo, w_hi) + b_ref[0]


def _modulation(cc, w_mod, b_mod):
    depth, d, n = w_mod.shape
    rows = cc.shape[0]
    tn = n // 4
    return pl.pallas_call(
        _mod_kernel,
        grid=(depth, n // tn),
        in_specs=[
            pl.BlockSpec((rows, d), lambda l, j: (0, 0)),
            pl.BlockSpec((1, d, tn), lambda l, j: (l, 0, j)),
            pl.BlockSpec((1, 1, tn), lambda l, j: (l, 0, j)),
        ],
        out_specs=pl.BlockSpec((1, rows, tn), lambda l, j: (l, 0, j)),
        out_shape=jax.ShapeDtypeStruct((depth, rows, n), F32),
        compiler_params=_params("arbitrary", "arbitrary"),
        name="modulation",
    )(cc, w_mod, b_mod.reshape(depth, 1, n))


def _premix_kernel(x_ref, mod_ref, n1g_ref, win_ref, qag_ref, wuq_ref, kvag_ref, wuk_ref, wvt_ref,
                   gq_ref, gqr_ref, gk_ref, gkr_ref, gqs_ref, gqsr_ref, gks_ref, gksr_ref,
                   cm_ref, sm_ref, cs_ref, ss_ref,
                   qm_ref, km_ref, vt_ref, lx_ref, lg_ref, qs_ref, ks_ref, vts_ref):
    d = x_ref.shape[-1]
    tm = x_ref.shape[1]
    shift = mod_ref[0, :, 0:d]
    scale = mod_ref[0, :, d:2 * d]
    n = PREMIX_ROWS if tm % PREMIX_ROWS == 0 else tm
    for rows in (slice(r, r + n) for r in range(0, tm, n)):
        _premix_rows(rows, n, shift, scale, x_ref, n1g_ref, win_ref, qag_ref, wuq_ref, kvag_ref, wuk_ref, wvt_ref,
                     gq_ref, gqr_ref, gk_ref, gkr_ref, gqs_ref, gqsr_ref, gks_ref, gksr_ref,
                     cm_ref, sm_ref, cs_ref, ss_ref,
                     qm_ref, km_ref, vt_ref, lx_ref, lg_ref, qs_ref, ks_ref, vts_ref)


def _premix_rows(rows, n, shift, scale, x_ref, n1g_ref, win_ref, qag_ref, wuq_ref, kvag_ref, wuk_ref, wvt_ref,
                 gq_ref, gqr_ref, gk_ref, gkr_ref, gqs_ref, gqsr_ref, gks_ref, gksr_ref,
                 cm_ref, sm_ref, cs_ref, ss_ref,
                 qm_ref, km_ref, vt_ref, lx_ref, lg_ref, qs_ref, ks_ref, vts_ref):
    h = (_rms(x_ref[0, rows, :]) * n1g_ref[...] * (1.0 + scale) + shift).astype(BF16)

    cm, sm = cm_ref[rows, :], sm_ref[rows, :]
    cs, ss = cs_ref[rows, :], ss_ref[rows, :]

    p1 = _dot(h, win_ref[:, 0:SECTION_2])
    qn = (_rms(p1[:, OFF_CQ:OFF_CQ + MLA_Q_RANK]) * qag_ref[...]).astype(BF16)
    kvn = (_rms(p1[:, OFF_CKV:OFF_CKV + MLA_KV_RANK]) * kvag_ref[...]).astype(BF16)
    kr = p1[:, OFF_KR:OFF_KR + LANES]
    ss_kr = jnp.sum(kr * kr, axis=-1, keepdims=True)

    p2 = _dot(h, win_ref[:, SECTION_2:OFF_LX])
    qu = _dot(qn, wuq_ref[...])
    kvu = _dot(kvn, wuk_ref[...])
    vt_ref[0, :, rows] = _dot_nt(wvt_ref[...], kvn).astype(BF16)
    ga_k = gk_ref[...] * cm
    shared = kr * ga_k + p2[:, 0:LANES] * (gkr_ref[...] * sm)
    lo = lax.broadcasted_iota(jnp.int32, (n, LANES), 1) < SWA_HEAD_DIM

    def pair_norm_rope(off, off_rot, ga, gb):
        x1 = p2[:, off - SECTION_2:off - SECTION_2 + LANES]
        x2 = p2[:, off_rot - SECTION_2:off_rot - SECTION_2 + LANES]
        sq = x1 * x1
        s_lo = jnp.sum(jnp.where(lo, sq, 0.0), axis=-1, keepdims=True)
        s_hi = jnp.sum(jnp.where(lo, 0.0, sq), axis=-1, keepdims=True)
        inv = 1.0 / SWA_HEAD_DIM
        r = jnp.where(lo, lax.rsqrt(s_lo * inv + EPS), lax.rsqrt(s_hi * inv + EPS))
        return (r * (x1 * ga + x2 * gb)).astype(BF16)

    ga_s, gb_s = gqs_ref[...] * cs, gqsr_ref[...] * ss
    for g in range(SWA_PAIRS):
        qs_ref[0, rows, g * LANES:(g + 1) * LANES] = pair_norm_rope(OFF_SQ + g * LANES, OFF_SQR + g * LANES, ga_s, gb_s)
    ks_ref[0, rows, :] = pair_norm_rope(OFF_SK, OFF_SKR, gks_ref[...] * cs, gksr_ref[...] * ss)
    vts_ref[0, :, rows] = p2[:, OFF_SV - SECTION_2:OFF_SV - SECTION_2 + SWA_KW].T.astype(BF16)

    p3 = _dot(h, win_ref[:, OFF_LX:W_IN_PAD])
    ga_q, gb_q = gq_ref[...] * cm, gqr_ref[...] * sm
    for hd in range(MLA_HEADS):
        x1 = qu[:, hd * LANES:(hd + 1) * LANES]
        x2 = qu[:, MLA_QW + hd * LANES:MLA_QW + (hd + 1) * LANES]
        r = lax.rsqrt(jnp.sum(x1 * x1, axis=-1, keepdims=True) * (1.0 / MLA_QK) + EPS)
        qm_ref[0, rows, hd * LANES:(hd + 1) * LANES] = (r * (x1 * ga_q + x2 * gb_q)).astype(BF16)
    for hd in range(MLA_HEADS):
        x1 = kvu[:, hd * LANES:(hd + 1) * LANES]
        r = lax.rsqrt((jnp.sum(x1 * x1, axis=-1, keepdims=True) + ss_kr) * (1.0 / MLA_QK) + EPS)
        km_ref[0, rows, hd * LANES:(hd + 1) * LANES] = (r * (x1 * ga_k + shared)).astype(BF16)
    lx_ref[0, rows, :] = p3[:, 0:LRU_WIDTH]
    lg_ref[0, rows, :] = p3[:, LRU_WIDTH:2 * LRU_WIDTH]


def _premix(x, mod, shared_row, layer, wts, rope_m, rope_s):
    b, l, d = x.shape
    tm = min(l, 1024)
    consts = [wts[k] for k in ("n1g", "w_in", "qag", "w_uq", "kvag", "w_uk", "w_vt",
                               "gq", "gq_rot", "gk", "gk_rot", "gqs", "gqs_rot", "gks", "gks_rot")]
    tables = list(rope_m) + list(rope_s)
    tok = lambda w, dt: (pl.BlockSpec((1, tm, w), lambda bi, i: (bi, i, 0)), jax.ShapeDtypeStruct((b, l, w), dt))
    tr = lambda w: (pl.BlockSpec((1, w, tm), lambda bi, i: (bi, 0, i)), jax.ShapeDtypeStruct((b, w, l), BF16))
    outs = [tok(MLA_QW, BF16), tok(MLA_QW, BF16), tr(MLA_VW), tok(LRU_WIDTH, F32), tok(LRU_WIDTH, F32),
            tok(SWA_QW, BF16), tok(SWA_KW, BF16), tr(SWA_KW)]
    return pl.pallas_call(
        _premix_kernel,
        grid=(b, l // tm),
        in_specs=[tok(d, F32)[0], _mod_spec(mod, layer, shared_row)]
        + [_layer_spec(c, layer) for c in consts]
        + [pl.BlockSpec((tm, LANES), lambda bi, i: (i, 0)) for _ in tables],
        out_specs=[o[0] for o in outs],
        out_shape=[o[1] for o in outs],
        compiler_params=_params("arbitrary", "arbitrary"),
        name="premix",
    )(x, mod, *consts, *tables)


def _gelu_tanh(x):
    c = 0.7978845608028654
    half_x = 0.5 * x
    return half_x * jnp.tanh(x * (c + (c * 0.044715) * (x * x))) + half_x


LRU_SCRATCH_PER_GROUP = 5


def _lru_kernel(lxc_ref, lxl_ref, lgc_ref, lgl_ref, cw_ref, cb_ref, wg_ref, bg_ref, lam_ref,
                obl_ref, obc_ref, *scratch):
    lc = lxc_ref.shape[1]
    ll = lxl_ref.shape[1]
    groups = [scratch[i:i + LRU_SCRATCH_PER_GROUP] for i in range(0, len(scratch), LRU_SCRATCH_PER_GROUP)]
    pitch = scratch[0].shape[0] // SUBLANES
    tail = scratch[0].shape[0] - (lc + ll)
    assert tail > 0
    assert pitch % 2 == 0

    def coefficients(gi, x, xs):
        lanes = slice(gi * LANES, (gi + 1) * LANES)
        cw = cw_ref[:, lanes]
        nlam = -lam_ref[:, lanes]
        softplus = jnp.maximum(nlam, 0.0) + jnp.log1p(jnp.exp(-jnp.abs(nlam)))
        decay = (-0.5 * LRU_C * LOG2_E) * softplus
        t = x.shape[0]
        pad = jnp.zeros((SUBLANES, LANES), F32)
        xs[0:SUBLANES, :] = pad
        xs[SUBLANES:SUBLANES + t, :] = x
        xs[SUBLANES + t:2 * SUBLANES + t, :] = pad
        xm2, xm1, xp1 = (xs[SUBLANES + o:SUBLANES + o + t, :] for o in (-2, -1, 1))
        xc = cw[0:1] * xm2 + cw[1:2] * xm1 + cw[2:3] * x + cw[3:4] * xp1 + cb_ref[:, lanes]
        th = jnp.tanh(_dot(xc.astype(BF16), wg_ref[gi]) + bg_ref[gi])
        half_x = 0.5 * xc
        out = []
        for z in range(2):
            t_r = th[:, 2 * z * LANES:(2 * z + 1) * LANES]
            t_i = th[:, (2 * z + 1) * LANES:(2 * z + 2) * LANES]
            a = jnp.exp2(decay[z:z + 1] * t_r + decay[z:z + 1])
            y = 1.0 - a * a
            root = y * lax.rsqrt(jnp.maximum(y, jnp.finfo(F32).tiny))
            out.append((a, root * (half_x * t_i + half_x)))
        return out

    for gi, (af, uf, ab, ub, xs) in enumerate(groups):
        for x_ref, f_off, b_off in ((lxc_ref, 0, ll), (lxl_ref, lc, 0)):
            t = x_ref.shape[1]
            (a_f, u_f), (a_b, u_b) = coefficients(gi, x_ref[0, :, gi * LANES:(gi + 1) * LANES], xs)
            af[f_off:f_off + t, :] = a_f
            uf[f_off:f_off + t, :] = u_f
            ab[b_off:b_off + t, :] = a_b
            ub[b_off:b_off + t, :] = u_b
        for a_ref, u_ref in ((af, uf), (ab, ub)):
            a_ref[lc + ll:, :] = jnp.ones((tail, LANES), F32)
            u_ref[lc + ll:, :] = jnp.zeros((tail, LANES), F32)

    def streams(i):
        return pl.ds(i, SUBLANES, stride=pitch)

    def two_steps(a_ref, u_ref, i0, i1, h, p):
        a0, u0 = a_ref[streams(i0), :], u_ref[streams(i0), :]
        a1, u1 = a_ref[streams(i1), :], u_ref[streams(i1), :]
        a01 = a1 * a0
        u_ref[streams(i0), :] = a0 * h + u0
        a_ref[streams(i0), :] = a0 * p
        h = a01 * h + (a1 * u0 + u1)
        p = a01 * p
        u_ref[streams(i1), :] = h
        a_ref[streams(i1), :] = p
        return h, p

    def local_scan(i, state):
        out = []
        for (af, uf, ab, ub, _), (h_f, p_f, h_b, p_b) in zip(groups, state):
            h_f, p_f = two_steps(af, uf, 2 * i, 2 * i + 1, h_f, p_f)
            h_b, p_b = two_steps(ab, ub, pitch - 1 - 2 * i, pitch - 2 - 2 * i, h_b, p_b)
            out.append((h_f, p_f, h_b, p_b))
        return tuple(out)

    zeros8 = jnp.zeros((SUBLANES, LANES), F32)
    ones8 = jnp.ones((SUBLANES, LANES), F32)
    finals = lax.fori_loop(0, pitch // 2, local_scan, ((zeros8, ones8, zeros8, ones8),) * len(groups), unroll=2)

    sub = lax.broadcasted_iota(jnp.int32, (SUBLANES, LANES), 0)
    carries = []
    for h_f, p_f, h_b, p_b in finals:
        c_f = zeros8
        c_b = zeros8
        row_f = jnp.zeros((1, LANES), F32)
        row_b = jnp.zeros((1, LANES), F32)
        for s in range(1, SUBLANES):
            row_f = h_f[s - 1:s] + p_f[s - 1:s] * row_f
            c_f = jnp.where(sub == s, row_f, c_f)
            sb = SUBLANES - 1 - s
            row_b = h_b[sb + 1:sb + 2] + p_b[sb + 1:sb + 2] * row_b
            c_b = jnp.where(sub == sb, row_b, c_b)
        carries.append((c_f, c_b))

    def add_carry(i, _):
        for (af, uf, ab, ub, _), (c_f, c_b) in zip(groups, carries):
            uf[streams(i), :] = uf[streams(i), :] + af[streams(i), :] * c_f
            ub[streams(i), :] = ub[streams(i), :] + ab[streams(i), :] * c_b
        return 0

    lax.fori_loop(0, pitch, add_carry, 0, unroll=4)

    for gi, (_, uf, _, ub, _) in enumerate(groups):
        lanes = slice(gi * LANES, (gi + 1) * LANES)
        obl_ref[0, :, lanes] = (uf[lc:lc + ll, :] + ub[0:ll, :]) * _gelu_tanh(lgl_ref[0, :, lanes])
        obc_ref[0, :, lanes] = (uf[0:lc, :] + ub[ll:ll + lc, :]) * _gelu_tanh(lgc_ref[0, :, lanes])


def _rglru(lx_c, lx_l, lg_c, lg_l, layer, wts):
    b, lc, w = lx_c.shape
    ll = lx_l.shape[1]
    gps = LRU_GROUPS_PER_STEP
    bw = gps * LANES
    pitch = -(-(lc + ll) // SUBLANES)
    pitch += (SUBLANES // 2 - pitch) % SUBLANES
    seq = lambda t: pl.BlockSpec((1, t, bw), lambda bi, g: (bi, 0, g))
    group_scratch = [pltpu.VMEM((SUBLANES * pitch, LANES), F32) for _ in range(LRU_SCRATCH_PER_GROUP - 1)]
    group_scratch.append(pltpu.VMEM((max(lc, ll) + 2 * SUBLANES, LANES), F32))
    return pl.pallas_call(
        _lru_kernel,
        grid=(b, w // bw),
        in_specs=[seq(lc), seq(ll), seq(lc), seq(ll),
                  pl.BlockSpec((None, CONV_W, bw), lambda bi, g: (layer, 0, g)),
                  pl.BlockSpec((None, 1, bw), lambda bi, g: (layer, 0, g)),
                  pl.BlockSpec((None, gps, LANES, 4 * LANES), lambda bi, g: (layer, g, 0, 0)),
                  pl.BlockSpec((None, gps, 1, 4 * LANES), lambda bi, g: (layer, g, 0, 0)),
                  pl.BlockSpec((None, 2, bw), lambda bi, g: (layer, 0, g))],
        out_specs=[seq(ll), seq(lc)],
        out_shape=[jax.ShapeDtypeStruct((b, ll, w), F32), jax.ShapeDtypeStruct((b, lc, w), F32)],
        scratch_shapes=group_scratch * gps,
        compiler_params=_params("arbitrary", "arbitrary"),
        name="rglru",
    )(lx_c, lx_l, lg_c, lg_l, wts["conv_w"], wts["conv_b"], wts["w_gate"], wts["b_gate"], wts["lam"])


def _mla_kernel(*refs, with_latent):
    if with_latent:
        q_ref, kc_ref, vtc_ref, kl_ref, vtl_ref, o_ref, s_buf = refs
        sources = ((kc_ref, vtc_ref), (kl_ref, vtl_ref))
    else:
        q_ref, kc_ref, vtc_ref, o_ref, s_buf = refs
        sources = ((kc_ref, vtc_ref),)
    tq = q_ref.shape[1]
    chunks, row = [], 0
    for k_ref, vt_ref in sources:
        n = k_ref.shape[1]
        for k0 in range(0, n, MLA_KEY_CHUNK):
            kn = min(MLA_KEY_CHUNK, n - k0)
            chunks.append((k_ref, vt_ref, k0, kn, row))
            row += kn

    def add(acc, x, op=jnp.add):
        return x if acc is None else op(acc, x)

    slot0 = jnp.minimum(pl.program_id(0), 0)

    def score_chunk(hd, chunk, m8):
        k_ref, _, k0, kn, r0 = chunk
        sl = slice(hd * LANES, (hd + 1) * LANES)
        s = _dot_nt(k_ref[0, k0:k0 + kn, sl], q_ref[0, :, sl])
        s_buf[hd % 2 + slot0, r0:r0 + kn, :] = s
        return add(m8, jnp.max(s.reshape(kn // SUBLANES, SUBLANES, tq), axis=0), jnp.maximum)

    def attend_chunk(hd, chunk, m, o_t):
        _, vt_ref, k0, kn, r0 = chunk
        p = jnp.exp2(s_buf[hd % 2 + slot0, r0:r0 + kn, :] - m).astype(BF16)
        vt1 = jnp.concatenate([vt_ref[0, hd * MLA_V:(hd + 1) * MLA_V, k0:k0 + kn],
                               jnp.ones((2 * SUBLANES, kn), BF16)], axis=0)
        return add(o_t, _dot(vt1, p))

    outs = []
    m8 = None
    for chunk in chunks:
        m8 = score_chunk(0, chunk, m8)
    for hd in range(MLA_HEADS_PER_STEP):
        m = jnp.max(m8, axis=0, keepdims=True)
        m8, o_t = None, None
        if hd + 1 < MLA_HEADS_PER_STEP:
            for chunk in chunks:
                m8 = score_chunk(hd + 1, chunk, m8)
        for chunk in chunks:
            o_t = attend_chunk(hd, chunk, m, o_t)
        outs.append(o_t[0:MLA_V] * (1.0 / o_t[MLA_V:MLA_V + 1]))
    o_ref[0] = jnp.concatenate(outs, axis=0).T.astype(o_ref.dtype)


def _mla_attention(q, k_c, vt_c, k_l=None, vt_l=None):
    b, lq, _ = q.shape
    lc = k_c.shape[1]
    with_latent = k_l is not None
    tq = min(lq, 512)
    qw = MLA_HEADS_PER_STEP * LANES
    vw = MLA_HEADS_PER_STEP * MLA_V
    nq = MLA_HEADS // MLA_HEADS_PER_STEP
    keys = lambda t: pl.BlockSpec((1, t, qw), lambda bi, hq, i: (bi, 0, hq))
    vals = lambda t: pl.BlockSpec((1, vw, t), lambda bi, hq, i: (bi, hq, 0))
    in_specs = [pl.BlockSpec((1, tq, qw), lambda bi, hq, i: (bi, i, hq)), keys(lc), vals(lc)]
    args = [q, k_c, vt_c]
    if with_latent:
        in_specs += [keys(k_l.shape[1]), vals(k_l.shape[1])]
        args += [k_l, vt_l]
    return pl.pallas_call(
        functools.partial(_mla_kernel, with_latent=with_latent),
        grid=(b, nq, lq // tq),
        in_specs=in_specs,
        out_specs=pl.BlockSpec((1, tq, vw), lambda bi, hq, i: (bi, i, hq)),
        out_shape=jax.ShapeDtypeStruct((b, lq, MLA_VW), BF16),
        scratch_shapes=[pltpu.VMEM((2, lc + (k_l.shape[1] if with_latent else 0), tq), F32)],
        compiler_params=_params("arbitrary", "arbitrary", "arbitrary"),
        name="mla_attention" if with_latent else "mla_attention_ctx",
    )(*args)


def _swa_kernel(*refs, with_window, layer):
    if with_window:
        sink_ref, q_ref, kc_ref, vtc_ref, kl_ref, vtl_ref, o_ref, s_buf = refs
    else:
        sink_ref, q_ref, kc_ref, vtc_ref, o_ref, s_buf = refs
    bq = WINDOW
    cols = SWA_PAIRS * bq
    kc = kc_ref[0]
    lc = kc.shape[0]
    slot0 = jnp.minimum(pl.program_id(0), 0)
    lo = lax.broadcasted_iota(jnp.int32, (cols, LANES), 1) < SWA_HEAD_DIM
    seg = lax.broadcasted_iota(jnp.int32, (1, cols), 1) // bq

    span = bq + 2 * WINDOW
    sinks = []
    for kvh in range(SWA_KV_HEADS):
        sink = jnp.zeros((1, cols), F32)
        for g in range(SWA_PAIRS):
            sink = jnp.where(seg == g, sink_ref[layer, kvh * SWA_GROUP + g] * LOG2_E, sink)
        sinks.append(sink)

    def scores(sb):
        r0 = sb * bq
        q = jnp.concatenate([q_ref[0, r0:r0 + bq, g * LANES:(g + 1) * LANES] for g in range(SWA_PAIRS)],
                            axis=0).astype(F32)
        w0 = None
        if with_window:
            q0 = pl.program_id(1) * q_ref.shape[1] + r0
            w0 = pl.multiple_of(jnp.clip(q0 - WINDOW, 0, kl_ref.shape[1] - span), LANES)
            kw = kl_ref[0, pl.ds(w0, span), :]
            kj = w0 + lax.broadcasted_iota(jnp.int32, (span, cols), 0)
            qi = q0 + (lax.broadcasted_iota(jnp.int32, (span, cols), 1) & (bq - 1))
            band = jnp.abs(qi - kj) <= WINDOW
        maxes = []
        for kvh in range(SWA_KV_HEADS):
            qh = (jnp.where(lo, 0.0, q) if kvh else jnp.where(lo, q, 0.0)).astype(BF16)
            s = [_dot_nt(kc, qh)]
            if with_window:
                s.append(jnp.where(band, _dot_nt(kw, qh), NEG_INF))
            m, row = sinks[kvh], 0
            for x in s:
                s_buf[sb % 2 + slot0, kvh, row:row + x.shape[0], :] = x
                m = jnp.maximum(m, jnp.max(x, axis=0, keepdims=True))
                row += x.shape[0]
            maxes.append(m)
        return w0, maxes

    def attend(sb, w0, maxes):
        o_t = []
        for kvh, m in enumerate(maxes):
            sink = sinks[kvh]
            rows = [(0, lc)] + ([(lc, lc + span)] if with_window else [])
            p = [jnp.exp2(s_buf[sb % 2 + slot0, kvh, a:b, :] - m) for a, b in rows]
            denom = sum(jnp.sum(x, axis=0, keepdims=True) for x in p) + jnp.exp2(sink - m)
            vs = slice(kvh * SWA_HEAD_DIM, (kvh + 1) * SWA_HEAD_DIM)
            o = _dot(vtc_ref[0, vs, :], p[0].astype(BF16))
            if with_window:
                o = o + _dot(vtl_ref[0, vs, pl.ds(w0, span)], p[1].astype(BF16))
            o_t.append(o * (1.0 / denom))
        out = jnp.concatenate(o_t, axis=0).T.astype(o_ref.dtype)
        for g in range(SWA_PAIRS):
            o_ref[0, sb * bq:(sb + 1) * bq, g * LANES:(g + 1) * LANES] = out[g * bq:(g + 1) * bq]

    n_blocks = q_ref.shape[1] // bq
    nxt = scores(0)
    for sb in range(n_blocks):
        cur = nxt
        if sb + 1 < n_blocks:
            nxt = scores(sb + 1)
        attend(sb, *cur)


def _swa_attention(q, k_c, vt_c, sink, layer, k_l=None, vt_l=None):
    b, lq, _ = q.shape
    lc = k_c.shape[1]
    with_window = k_l is not None
    tq = min(lq, SWA_Q_BLOCKS_PER_STEP * WINDOW)
    keys = lambda t: pl.BlockSpec((1, t, SWA_KW), lambda bi, i: (bi, 0, 0))
    vals = lambda t: pl.BlockSpec((1, SWA_KW, t), lambda bi, i: (bi, 0, 0))
    in_specs = [pl.BlockSpec(memory_space=pltpu.SMEM),
                pl.BlockSpec((1, tq, SWA_QW), lambda bi, i: (bi, i, 0)), keys(lc), vals(lc)]
    args = [sink, q, k_c, vt_c]
    if with_window:
        in_specs += [keys(k_l.shape[1]), vals(k_l.shape[1])]
        args += [k_l, vt_l]
    return pl.pallas_call(
        functools.partial(_swa_kernel, with_window=with_window, layer=layer),
        grid=(b, lq // tq),
        in_specs=in_specs,
        out_specs=pl.BlockSpec((1, tq, SWA_QW), lambda bi, i: (bi, i, 0)),
        out_shape=jax.ShapeDtypeStruct((b, lq, SWA_QW), BF16),
        scratch_shapes=[pltpu.VMEM((2, SWA_KV_HEADS, lc + (3 * WINDOW if with_window else 0), SWA_PAIRS * WINDOW), F32)],
        compiler_params=_params("arbitrary", "arbitrary"),
        name="swa_attention" if with_window else "swa_attention_ctx",
    )(*args)


def _postmix_kernel(x_ref, oa_ref, ob_ref, oc_ref, mod_ref, gg_ref, wout_ref, n2g_ref, w1_ref, w2_ref, o_ref):
    d = x_ref.shape[-1]
    gate1 = mod_ref[0, :, 2 * d:3 * d]
    shift2 = mod_ref[0, :, 3 * d:4 * d]
    scale2 = mod_ref[0, :, 4 * d:5 * d]
    gate2 = mod_ref[0, :, 5 * d:6 * d]
    tm = x_ref.shape[1]
    halves = [slice(r, r + tm // 2) for r in (0, tm // 2)]

    def mix(rows):
        y = None
        for gi, ref in enumerate((oa_ref, ob_ref, oc_ref)):
            sl = slice(gi * GROUP_WIDTH, (gi + 1) * GROUP_WIDTH)
            og = _rms(ref[0, rows, :].astype(F32)) * gg_ref[:, sl]
            part = _dot(og.astype(BF16), wout_ref[sl, :])
            y = part if y is None else y + part
        return y

    def residual_norm(rows, y):
        x1 = x_ref[0, rows, :] + gate1 * y
        return x1, (_rms(x1) * n2g_ref[...] * (1.0 + scale2) + shift2).astype(BF16)

    def mlp(h2):
        ff = None
        for j in range(w1_ref.shape[1] // FF_CHUNK):
            sl = slice(j * FF_CHUNK, (j + 1) * FF_CHUNK)
            hid = jnp.maximum(_dot(h2, w1_ref[:, sl]), 0.0)
            part = _dot((hid * hid).astype(BF16), w2_ref[sl, :])
            ff = part if ff is None else ff + part
        return ff

    y = [mix(rows) for rows in halves]
    for rows, y_half in zip(halves, y):
        x1, h2 = residual_norm(rows, y_half)
        o_ref[0, rows, :] = x1 + gate2 * mlp(h2)


def _postmix(x, o_a, o_b, o_c, mod, shared_row, layer, wts):
    b, l, d = x.shape
    tm = min(l, 512)
    tok = lambda w: pl.BlockSpec((1, tm, w), lambda bi, i: (bi, i, 0))
    consts = [wts["gg"], wts["w_out"], wts["n2g"], wts["w_ff1"], wts["w_ff2"]]
    return pl.pallas_call(
        _postmix_kernel,
        grid=(b, l // tm),
        in_specs=[tok(d), tok(GROUP_WIDTH), tok(GROUP_WIDTH), tok(GROUP_WIDTH), _mod_spec(mod, layer, shared_row)]
        + [_layer_spec(c, layer) for c in consts],
        out_specs=tok(d),
        out_shape=jax.ShapeDtypeStruct((b, l, d), F32),
        compiler_params=_params("arbitrary", "arbitrary"),
        name="postmix",
    )(x, o_a, o_b, o_c, mod, *consts)


def _rot_sign(dim):
    quarter = dim // 4
    return np.where((np.arange(dim) // quarter) % 2 == 0, -1.0, 1.0).astype(np.float32)


def _rot_partner(w, dim):
    return jnp.flip(w.reshape(w.shape[:-1] + (-1, 2, dim // 4)), axis=-2).reshape(w.shape)


def _pair_heads(w, axis):
    axis = axis % w.ndim
    shape = w.shape
    w = w.reshape(shape[:axis] + (SWA_KV_HEADS, SWA_PAIRS, SWA_HEAD_DIM) + shape[axis + 1:])
    return jnp.swapaxes(w, axis, axis + 1).reshape(shape)


def _rope_tables(rows, dim, lead, width):
    quarter = dim // 4
    n = rows * GRID_W
    row = jnp.repeat(jnp.arange(rows), GRID_W)
    col = jnp.tile(jnp.arange(GRID_W), rows)
    inv_freq = ROPE_THETA ** (-jnp.arange(quarter, dtype=F32) / quarter)
    ang = jnp.stack([row, col], axis=-1).astype(F32)[:, :, None] * inv_freq
    ang = jnp.broadcast_to(ang[:, :, None, :], (n, 2, 2, quarter)).reshape(n, dim)
    sign = _rot_sign(dim)

    def place(t, fill):
        t = jnp.concatenate([jnp.full((n, lead), fill, F32), t], axis=1)
        t = jnp.tile(t, (1, width // (lead + dim)))
        return jnp.concatenate([t, jnp.full((n, width - t.shape[1]), fill, F32)], axis=1)

    return place(jnp.cos(ang), 1.0), place(jnp.sin(ang) * sign, 0.0)


def _identity_tables(n, width):
    return jnp.ones((n, width), F32), jnp.zeros((n, width), F32)


def _pad_heads(w, heads, width, lead=0):
    shape = w.shape[:-1]
    w = w.reshape(shape + (heads, width))
    w = jnp.pad(w, [(0, 0)] * len(shape) + [(0, 0), (lead, LANES - lead - width)])
    return w.reshape(shape + (heads * LANES,))


def _split_in(w):
    sizes = (MLA_Q_RANK, MLA_KV_RANK, MLA_ROPE, LRU_WIDTH, LRU_WIDTH, SWA_QW, SWA_KW, SWA_KW)
    parts, start = [], 0
    for s in sizes:
        parts.append(w[..., start:start + s])
        start += s
    return parts


def _prepare_weights(norm1_g, w_in, q_a_g, w_uq, kv_a_g, w_ukv, mla_q_g, mla_k_g, conv_w, conv_b,
                     lru_gate_w, lru_gate_b, lru_lambda, swa_q_g, swa_k_g, group_g, w_out,
                     norm2_g, w_ff1, w_ff2):
    depth = w_in.shape[0]
    row = lambda v: v[:, None, :]
    cq, ckv, kr, lx, lg, sq, sk, sv = _split_in(w_in)
    sq = _pair_heads(sq, -1)
    w_in_p = jnp.concatenate(
        [cq, ckv, _pad_heads(kr, 1, MLA_ROPE, MLA_NOPE), _pad_heads(_rot_partner(kr, MLA_ROPE), 1, MLA_ROPE, MLA_NOPE),
         sq, _rot_partner(sq, SWA_HEAD_DIM), sk, _rot_partner(sk, SWA_HEAD_DIM), sv, lx, lg], axis=-1)
    uq = w_uq.reshape(depth, MLA_Q_RANK, MLA_HEADS, MLA_QK)
    uq_rot = _rot_partner(uq[..., MLA_NOPE:], MLA_ROPE).reshape(depth, MLA_Q_RANK, MLA_HEADS * MLA_ROPE)
    w_uq_p = jnp.concatenate([_pad_heads(w_uq, MLA_HEADS, MLA_QK),
                              _pad_heads(uq_rot, MLA_HEADS, MLA_ROPE, MLA_NOPE)], axis=-1)
    kv = w_ukv.reshape(depth, MLA_KV_RANK, MLA_HEADS, MLA_NOPE + MLA_V)
    w_k = _pad_heads(kv[..., :MLA_NOPE].reshape(depth, MLA_KV_RANK, MLA_HEADS * MLA_NOPE), MLA_HEADS, MLA_NOPE)
    w_vt = jnp.swapaxes(kv[..., MLA_NOPE:].reshape(depth, MLA_KV_RANK, MLA_VW), 1, 2)
    nblk = lru_gate_w.shape[3]
    per_group = LANES // LRU_BLOCK_DIM
    groups = nblk // per_group
    eye = jnp.eye(per_group, dtype=F32)
    gw = lru_gate_w.reshape(depth, 2, 2, groups, per_group, LRU_BLOCK_DIM, LRU_BLOCK_DIM)
    w_gate = 0.5 * jnp.einsum("dzgpncm,nk->dpnczgkm", gw, eye).reshape(depth, groups, LANES, 4 * LANES)
    b_gate = 0.5 * lru_gate_b.reshape(depth, 2, 2, groups, LANES).transpose(0, 3, 1, 2, 4).reshape(
        depth, groups, 1, 4 * LANES)
    gq = mla_q_g * (MLA_QK ** -0.5 * LOG2_E)
    gqs = swa_q_g * (SWA_HEAD_DIM ** -0.5 * LOG2_E)
    rope_gain = lambda g: _pad_heads(row(_rot_partner(g[:, MLA_NOPE:], MLA_ROPE)), 1, MLA_ROPE, MLA_NOPE)
    pair = lambda g: row(jnp.tile(g, (1, LANES // SWA_HEAD_DIM)))
    c0 = 2 * GROUP_WIDTH
    gg = jnp.concatenate([group_g[:, :c0], _pair_heads(group_g[:, c0:], 1)], axis=1)
    w_o = jnp.concatenate([w_out[:, :c0], _pair_heads(w_out[:, c0:], 1)], axis=1)
    return {
        "n1g": row(norm1_g), "w_in": w_in_p.astype(BF16), "qag": row(q_a_g),
        "w_uq": w_uq_p.astype(BF16), "kvag": row(kv_a_g),
        "w_uk": w_k.astype(BF16), "w_vt": w_vt.astype(BF16),
        "gq": _pad_heads(row(gq), 1, MLA_QK), "gq_rot": rope_gain(gq),
        "gk": _pad_heads(row(mla_k_g), 1, MLA_QK), "gk_rot": rope_gain(mla_k_g),
        "gqs": pair(gqs), "gqs_rot": pair(_rot_partner(gqs, SWA_HEAD_DIM)),
        "gks": pair(swa_k_g), "gks_rot": pair(_rot_partner(swa_k_g, SWA_HEAD_DIM)),
        "conv_w": conv_w, "conv_b": row(conv_b), "w_gate": w_gate.astype(BF16), "b_gate": b_gate,
        "lam": lru_lambda,
        "gg": row(gg), "w_out": w_o.astype(BF16), "n2g": row(norm2_g),
        "w_ff1": w_ff1.astype(BF16), "w_ff2": w_ff2.astype(BF16),
    }


def kernel(x, c, ctx, c_ctx, w_mod, b_mod, norm1_g, w_in, q_a_g, w_uq, kv_a_g, w_ukv, mla_q_g, mla_k_g, conv_w, conv_b, lru_gate_w, lru_gate_b, lru_lambda, swa_q_g, swa_k_g, swa_sink, group_g, w_out, norm2_g, w_ff1, w_ff2):
    b, l, d = x.shape
    lc = ctx.shape[1]
    depth = w_mod.shape[0]
    rows = l // GRID_W

    n_rows = -(-(b + 1) // SUBLANES) * SUBLANES
    cc = jnp.concatenate([c, c_ctx[None], jnp.zeros((n_rows - b - 1, d), F32)], axis=0)
    mod = _modulation(cc, w_mod, b_mod).reshape(depth, n_rows, 1, N_MOD * d)

    rope_m = _rope_tables(rows, MLA_ROPE, MLA_NOPE, LANES)
    rope_s = _rope_tables(rows, SWA_HEAD_DIM, 0, LANES)
    no_rope = _identity_tables(lc, LANES)
    wts = _prepare_weights(norm1_g, w_in, q_a_g, w_uq, kv_a_g, w_ukv, mla_q_g, mla_k_g, conv_w, conv_b,
                           lru_gate_w, lru_gate_b, lru_lambda, swa_q_g, swa_k_g, group_g, w_out,
                           norm2_g, w_ff1, w_ff2)

    xc = ctx
    for layer in range(depth):
        last = layer == depth - 1
        qm, km, vt, lx, lg, qs, ks, vts = _premix(x, mod, None, layer, wts, rope_m, rope_s)
        qm_c, km_c, vt_c, lx_c, lg_c, qs_c, ks_c, vts_c = _premix(xc, mod, b, layer, wts, no_rope, no_rope)

        o_b, o_b_c = _rglru(lx_c, lx, lg_c, lg, layer, wts)
        o_a = _mla_attention(qm, km_c, vt_c, km, vt)
        o_c = _swa_attention(qs, ks_c, vts_c, swa_sink, layer, ks, vts)
        x = _postmix(x, o_a, o_b, o_c, mod, None, layer, wts)
        if not last:
            o_a_c = _mla_attention(qm_c, km_c, vt_c)
            o_c_c = _swa_attention(qs_c, ks_c, vts_c, swa_sink, layer)
            xc = _postmix(xc, o_a_c, o_b_c, o_c_c, mod, b, layer, wts)
    return x
```
